```python
import math
import numpy as np
import jax
import jax.numpy as jnp
from jax import lax

D_MODEL = 1024
BATCH = 4
SEQ = 4096
DEPTH = 2

HEAD_DIM = 64
ROT_DIM = HEAD_DIM // 4
ROPE_THETA = 500000.0
RMS_EPS = 1e-6
MASK_VALUE = -1e30

A_HEADS = 4
A_VDIM = 2 * HEAD_DIM
A_QBLOCK = 128
A_WIDTH = A_HEADS * A_VDIM

B_PATTERNS = ((128, 1), (512, 4), (2048, 16))
B_GROUPS = 3
B_HEADS = 4
B_BAND = 64
B_WIDTH = B_HEADS * HEAD_DIM

GRID_W = 64
C_HEADS = 4
NA_KH = 8
NA_KW = 16
NA_QB = 16
NA_KSPAN = 2 * NA_KW
C_WIDTH = C_HEADS * HEAD_DIM

N_BRANCHES = 3
BR_WIDTH = A_WIDTH + B_WIDTH + C_WIDTH
A_QK_COLS = A_HEADS * 2 * HEAD_DIM
B_COLS = B_GROUPS * B_HEADS * HEAD_DIM
IN_SPLITS = (A_QK_COLS, A_QK_COLS, A_WIDTH, B_COLS, B_COLS, B_COLS, C_WIDTH, C_WIDTH, C_WIDTH, BR_WIDTH, N_BRANCHES * D_MODEL)
IN_COLS = 2 * A_QK_COLS + A_WIDTH + 3 * B_COLS + 3 * C_WIDTH + BR_WIDTH + N_BRANCHES * D_MODEL

kernel_name = 'hybrid_gated_mixer_encoder'


def rms_norm(x, gain):
    xf = x.astype(jnp.float32)
    y = xf * lax.rsqrt(jnp.mean(xf * xf, axis=-1, keepdims=True) + RMS_EPS)
    return (y * gain.astype(jnp.float32)).astype(x.dtype)


def rope_tables(positions):
    inv = np.float32(ROPE_THETA) ** (-np.arange(0, ROT_DIM, 2, dtype=np.float32) / np.float32(ROT_DIM))
    ang = positions.astype(jnp.float32)[..., None] * jnp.asarray(inv, dtype=jnp.float32)
    return jnp.cos(ang), jnp.sin(ang)


def apply_partial_rope(t, cos, sin):
    shp = cos.shape[:2] + (1,) * (t.ndim - 3) + cos.shape[-1:]
    cos = cos.reshape(shp).astype(t.dtype)
    sin = sin.reshape(shp).astype(t.dtype)
    half = ROT_DIM // 2
    t1, t2, rest = t[..., :half], t[..., half:ROT_DIM], t[..., ROT_DIM:]
    return jnp.concatenate([t1 * cos - t2 * sin, t2 * cos + t1 * sin, rest], axis=-1)


def split_cols(t, sizes):
    idx, acc = [], 0
    for s in sizes[:-1]:
        acc += s
        idx.append(acc)
    return jnp.split(t, idx, axis=-1)


def diff_attention(q, k, v, lam, lambda_init, subln_gain):
    bn, s_len = q.shape[:2]
    nq = s_len // A_QBLOCK
    scale = HEAD_DIM ** -0.5
    qb = jnp.moveaxis(q.reshape(bn, nq, A_QBLOCK, A_HEADS, 2, HEAD_DIM), 1, 0)

    def one_block(qi):
        s = jnp.einsum('bqhcd,bkhcd->bhcqk', qi, k).astype(jnp.float32) * scale
        p = jax.nn.softmax(s, axis=-1)
        a = (p[:, :, 0] - lam * p[:, :, 1]).astype(v.dtype)
        return jnp.einsum('bhqk,bkhe->bqhe', a, v)

    o = lax.map(one_block, qb)
    o = jnp.moveaxis(o, 0, 1).reshape(bn, s_len, A_HEADS, A_VDIM)
    o = rms_norm(o, subln_gain) * (1.0 - lambda_init)
    return o.reshape(bn, s_len, A_WIDTH)


def dilated_group(q, k, v, dilation, radius):
    bn, s_len, h, dh = q.shape
    L = s_len // dilation

    def to_sub(t):
        return jnp.moveaxis(t.reshape(bn, L, dilation, h, dh), 2, 1)

    qs, ks, vs = to_sub(q), to_sub(k), to_sub(v)
    nb = -(-L // B_BAND)
    lp = nb * B_BAND
    qb = jnp.pad(qs, ((0, 0), (0, 0), (0, lp - L), (0, 0), (0, 0))).reshape(bn, dilation, nb, B_BAND, h, dh)
    pad_kv = ((0, 0), (0, 0), (B_BAND, lp - L + B_BAND), (0, 0), (0, 0))
    kp = jnp.pad(ks, pad_kv).reshape(bn, dilation, nb + 2, B_BAND, h, dh)
    vp = jnp.pad(vs, pad_kv).reshape(bn, dilation, nb + 2, B_BAND, h, dh)
    kw = jnp.concatenate([kp[:, :, :-2], kp[:, :, 1:-1], kp[:, :, 2:]], axis=3)
    vw = jnp.concatenate([vp[:, :, :-2], vp[:, :, 1:-1], vp[:, :, 2:]], axis=3)
    s = jnp.einsum('bmnqhd,bmnkhd->bmnhqk', qb, kw).astype(jnp.float32) * (HEAD_DIM ** -0.5)
    qi = np.arange(nb)[:, None, None] * B_BAND + np.arange(B_BAND)[None, :, None]
    kj = np.arange(nb)[:, None, None] * B_BAND + np.arange(3 * B_BAND)[None, None, :] - B_BAND
    valid = (np.abs(kj - qi) <= radius) & (kj >= 0) & (kj < L)
    s = jnp.where(valid[None, None, :, None], s, MASK_VALUE)
    m = jnp.max(s, axis=-1, keepdims=True)
    e = jnp.exp(s - m)
    den = jnp.sum(e, axis=-1, keepdims=True)
    o = jnp.einsum('bmnhqk,bmnkhd->bmnqhd', (e / den).astype(v.dtype), vw)
    lse = jnp.moveaxis((m + jnp.log(den))[..., 0], 3, 4)
    o = o.reshape(bn, dilation, lp, h, dh)[:, :, :L]
    o = jnp.moveaxis(o, 1, 2).reshape(bn, s_len, h, dh)
    lse = lse.reshape(bn, dilation, lp, h)[:, :, :L]
    lse = jnp.moveaxis(lse, 1, 2).reshape(bn, s_len, h)
    return o, lse


def dilated_mixture(q, k, v):
    outs, lses = [], []
    for g, (window, dilation) in enumerate(B_PATTERNS):
        o, lse = dilated_group(q[:, :, g], k[:, :, g], v[:, :, g], dilation, (window // 2) // dilation)
        outs.append(o)
        lses.append(lse)
    alpha = jax.nn.softmax(jnp.stack(lses, axis=0), axis=0)
    o = jnp.einsum('gbsh,gbshd->bshd', alpha.astype(outs[0].dtype), jnp.stack(outs, axis=0))
    bn, s_len = q.shape[:2]
    return o.reshape(bn, s_len, B_WIDTH)


def neighborhood_attention(q, k, v, rpb):
    bn, s_len, h, dh = q.shape
    rows = s_len // GRID_W
    kh = min(NA_KH, rows)
    ncb = GRID_W // NA_QB
    r_idx = np.arange(rows)
    row_start = np.clip(r_idx - kh // 2, 0, rows - kh)
    row_idx = row_start[:, None] + np.arange(kh)[None, :]
    blk_start = np.clip(np.arange(ncb) * NA_QB - NA_KW // 2, 0, GRID_W - NA_KSPAN)
    col_idx = blk_start[:, None] + np.arange(NA_KSPAN)[None, :]
    qcol = np.arange(ncb)[:, None] * NA_QB + np.arange(NA_QB)[None, :]
    col_start = np.clip(qcol - NA_KW // 2, 0, GRID_W - NA_KW)
    col_valid = (col_idx[:, None, :] >= col_start[..., None]) & (col_idx[:, None, :] < col_start[..., None] + NA_KW)
    mask = np.broadcast_to(col_valid[:, :, None, :], (ncb, NA_QB, kh, NA_KSPAN)).reshape(ncb, NA_QB, kh * NA_KSPAN)

    def gather_kv(t):
        g = t.reshape(bn, rows, GRID_W, h, dh)[:, row_idx]
        g = g[:, :, :, col_idx]
        g = jnp.moveaxis(g, 3, 2)
        return g.reshape(bn, rows, ncb, kh * NA_KSPAN, h, dh)

    kg, vg = gather_kv(k), gather_kv(v)
    qb = q.reshape(bn, rows, ncb, NA_QB, h, dh)
    s = jnp.einsum('brnqhd,brnkhd->brnhqk', qb, kg).astype(jnp.float32) * (HEAD_DIM ** -0.5)
    dr_i = row_idx - r_idx[:, None] + (NA_KH - 1)
    dc_i = np.clip(col_idx[:, None, :] - qcol[..., None] + (NA_KW - 1), 0, 2 * NA_KW - 2)
    bias = rpb[:, dr_i[:, None, None, :, None], dc_i[None, :, :, None, :]]
    bias = jnp.transpose(bias, (1, 2, 0, 3, 4, 5)).reshape(rows, ncb, h, NA_QB, kh * NA_KSPAN)
    s = jnp.where(mask[None, None, :, None], s + bias.astype(jnp.float32), MASK_VALUE)
    p = jax.nn.softmax(s, axis=-1).astype(v.dtype)
    o = jnp.einsum('brnhqk,brnkhd->brnqhd', p, vg)
    return o.reshape(bn, s_len, C_WIDTH)


def hybrid_layer(x, c_act, cos, sin, layer, norm_gain, w_ada, b_ada, w_in, diff_lambda, diff_subln_gain, na_rpb, w_branch, w_out):
    bn, s_len, _ = x.shape
    shift, scale, gate = jnp.split(c_act @ w_ada + b_ada, 3, axis=-1)
    h = rms_norm(x, norm_gain) * (1.0 + scale[:, None]) + shift[:, None]
    proj = h @ w_in
    a_q, a_k, a_v, b_q, b_k, b_v, c_q, c_k, c_v, z, g = split_cols(proj, IN_SPLITS)

    a_q = apply_partial_rope(a_q.reshape(bn, s_len, A_HEADS, 2, HEAD_DIM), cos, sin)
    a_k = apply_partial_rope(a_k.reshape(bn, s_len, A_HEADS, 2, HEAD_DIM), cos, sin)
    a_v = a_v.reshape(bn, s_len, A_HEADS, A_VDIM)
    lambda_init = 0.8 - 0.6 * math.exp(-0.3 * layer)
    lq1, lk1, lq2, lk2 = diff_lambda.astype(jnp.float32)
    lam = jnp.exp(jnp.sum(lq1 * lk1)) - jnp.exp(jnp.sum(lq2 * lk2)) + lambda_init
    y_a = diff_attention(a_q, a_k, a_v, lam, lambda_init, diff_subln_gain)

    b_q = apply_partial_rope(b_q.reshape(bn, s_len, B_GROUPS, B_HEADS, HEAD_DIM), cos, sin)
    b_k = apply_partial_rope(b_k.reshape(bn, s_len, B_GROUPS, B_HEADS, HEAD_DIM), cos, sin)
    b_v = b_v.reshape(bn, s_len, B_GROUPS, B_HEADS, HEAD_DIM)
    y_b = dilated_mixture(b_q, b_k, b_v)

    y_c = neighborhood_attention(c_q.reshape(bn, s_len, C_HEADS, HEAD_DIM), c_k.reshape(bn, s_len, C_HEADS, HEAD_DIM), c_v.reshape(bn, s_len, C_HEADS, HEAD_DIM), na_rpb)

    y = jnp.concatenate([y_a, y_b, y_c], axis=-1) * jax.nn.silu(z)
    gates = jax.nn.sigmoid(g.reshape(bn, s_len, N_BRANCHES, D_MODEL))
    bounds = ((0, A_WIDTH), (A_WIDTH, A_WIDTH + B_WIDTH), (A_WIDTH + B_WIDTH, BR_WIDTH))
    merged = gates[:, :, 0] * (y[..., bounds[0][0]:bounds[0][1]] @ w_branch[bounds[0][0]:bounds[0][1]])
    merged = merged + gates[:, :, 1] * (y[..., bounds[1][0]:bounds[1][1]] @ w_branch[bounds[1][0]:bounds[1][1]])
    merged = merged + gates[:, :, 2] * (y[..., bounds[2][0]:bounds[2][1]] @ w_branch[bounds[2][0]:bounds[2][1]])
    out = merged @ w_out
    return x + gate[:, None] * out


def setup_inputs(seed: int = 0) -> dict:
    key = jax.random.key(seed)
    ks = jax.random.split(key, 13)
    f32 = jnp.float32
    nrm = jax.random.normal
    x = nrm(ks[0], (BATCH, SEQ, D_MODEL), f32)
    c = nrm(ks[1], (BATCH, D_MODEL), f32)
    offs = jax.random.randint(ks[2], (BATCH, 1), 0, 1024, dtype=jnp.int32)
    positions = jnp.arange(SEQ, dtype=jnp.int32)[None, :] + offs
    norm_gain = 1.0 + 0.05 * nrm(ks[3], (DEPTH, D_MODEL), f32)
    w_ada = (0.5 * D_MODEL ** -0.5) * nrm(ks[4], (DEPTH, D_MODEL, 3 * D_MODEL), f32)
    b_ada = 0.01 * nrm(ks[5], (DEPTH, 3 * D_MODEL), f32)
    w_in = (D_MODEL ** -0.5) * nrm(ks[6], (DEPTH, D_MODEL, IN_COLS), f32)
    diff_lambda = 0.1 * nrm(ks[7], (DEPTH, 4, HEAD_DIM), f32)
    diff_subln_gain = 1.0 + 0.05 * nrm(ks[8], (DEPTH, A_VDIM), f32)
    na_rpb = 0.2 * nrm(ks[9], (DEPTH, C_HEADS, 2 * NA_KH - 1, 2 * NA_KW - 1), f32)
    br_scale = jnp.concatenate([jnp.full((A_WIDTH,), A_WIDTH ** -0.5, f32), jnp.full((B_WIDTH,), B_WIDTH ** -0.5, f32), jnp.full((C_WIDTH,), C_WIDTH ** -0.5, f32)])
    w_branch = nrm(ks[10], (DEPTH, BR_WIDTH, D_MODEL), f32) * br_scale[None, :, None]
    w_out = (D_MODEL ** -0.5) * nrm(ks[11], (DEPTH, D_MODEL, D_MODEL), f32)
    final_gain = 1.0 + 0.05 * nrm(ks[12], (D_MODEL,), f32)
    return {'x': x, 'c': c, 'positions': positions, 'norm_gain': norm_gain, 'w_ada': w_ada, 'b_ada': b_ada, 'w_in': w_in, 'diff_lambda': diff_lambda, 'diff_subln_gain': diff_subln_gain, 'na_rpb': na_rpb, 'w_branch': w_branch, 'w_out': w_out, 'final_gain': final_gain}


def reference(x, c, positions, norm_gain, w_ada, b_ada, w_in, diff_lambda, diff_subln_gain, na_rpb, w_branch, w_out, final_gain):
    cos, sin = rope_tables(positions)
    c_act = jax.nn.silu(c)
    for layer in range(DEPTH):
        x = hybrid_layer(x, c_act, cos, sin, layer, norm_gain[layer], w_ada[layer], b_ada[layer], w_in[layer], diff_lambda[layer], diff_subln_gain[layer], na_rpb[layer], w_branch[layer], w_out[layer])
    return rms_norm(x, final_gain)
```

```python
import functools
import math

import numpy as np
import jax
import jax.numpy as jnp
from jax import lax
from jax.experimental import pallas as pl
from jax.experimental.pallas import tpu as pltpu

D_MODEL = 1024
HEAD_DIM = 64
ROT_DIM = HEAD_DIM // 4
ROT_HALF = ROT_DIM // 2
ROPE_THETA = 500000.0
RMS_EPS = 1e-6
MASK_VALUE = -1e30

A_HEADS = 4
A_VDIM = 2 * HEAD_DIM
A_WIDTH = A_HEADS * A_VDIM
A_QK_COLS = A_HEADS * 2 * HEAD_DIM

B_PATTERNS = ((128, 1), (512, 4), (2048, 16))
B_GROUPS = 3
B_HEADS = 4
B_WIDTH = B_HEADS * HEAD_DIM
B_COLS = B_GROUPS * B_WIDTH

GRID_W = 64
C_HEADS = 4
NA_KH = 8
NA_KW = 16
C_WIDTH = C_HEADS * HEAD_DIM

N_BRANCHES = 3
BR_WIDTH = A_WIDTH + B_WIDTH + C_WIDTH

LANES = 128
HEAD_BLOCK = 4 * HEAD_DIM
VMEM_LIMIT = 56 * 1024 * 1024

COL_AQ = 0
COL_AK = COL_AQ + A_QK_COLS
COL_BQ = COL_AK + A_QK_COLS
COL_BK = COL_BQ + B_COLS
COL_BV = COL_BK + B_COLS
COL_CQ = COL_BV + B_COLS
COL_CK = COL_CQ + C_WIDTH
COL_CV = COL_CK + C_WIDTH
COL_Z = COL_CV + C_WIDTH
COL_G = COL_Z + BR_WIDTH
MAIN_COLS = COL_G + N_BRANCHES * D_MODEL
ROPE_COLS = COL_BV

PROJ_TM = 1024
PROJ_TN = 1024
A_TQ = 256
B_TQ = 128
B_REACH = 64
B_WIN = B_TQ + 2 * B_REACH
C_PAIR = 2 * GRID_W
C_KROWS = 10
C_WIN = C_KROWS * GRID_W
C_CASES = 5
C_RB = 8
MERGE_TM = 512

_NT = (((1,), (1,)), ((), ()))


def _sigmoid(x):
    return 1.0 / (1.0 + jnp.exp(-x))


def _rope_table_kernel(pos_ref, inv_ref, c_ref, s1_ref, s2_ref):
    ang = pos_ref[0] * inv_ref[...]
    cs = jnp.cos(ang)
    sn = jnp.sin(ang)
    lane = lax.broadcasted_iota(jnp.int32, ang.shape, 1) & (HEAD_DIM - 1)
    c_ref[0] = jnp.where(lane < ROT_DIM, cs, 1.0)
    s1_ref[0] = jnp.where((lane >= ROT_HALF) & (lane < ROT_DIM), sn, 0.0)
    s2_ref[0] = jnp.where(lane < ROT_HALF, -sn, 0.0)


def _rope_tables(positions):
    bn, s_len = positions.shape
    inv = np.float32(ROPE_THETA) ** (-np.arange(0, ROT_DIM, 2, dtype=np.float32) / np.float32(ROT_DIM))
    inv_lane = np.zeros((1, LANES), np.float32)
    for l in range(LANES):
        if (l % HEAD_DIM) < ROT_DIM:
            inv_lane[0, l] = inv[(l % HEAD_DIM) % ROT_HALF]
    pos = positions.astype(jnp.float32)[..., None]
    ts = 1024
    tab = jax.ShapeDtypeStruct((bn, s_len, LANES), jnp.float32)
    spec = pl.BlockSpec((1, ts, LANES), lambda b, i: (b, i, 0))
    return pl.pallas_call(
        _rope_table_kernel,
        out_shape=(tab, tab, tab),
        grid=(bn, s_len // ts),
        in_specs=[pl.BlockSpec((1, ts, 1), lambda b, i: (b, i, 0)),
                  pl.BlockSpec((1, LANES), lambda b, i: (0, 0))],
        out_specs=(spec, spec, spec),
        name="rope_tables",
    )(pos, jnp.asarray(inv_lane))


def _ada_kernel(c_ref, w_ref, b_ref, o_ref):
    c = c_ref[...]
    c_act = c * _sigmoid(c)
    o_ref[0] = jnp.dot(c_act, w_ref[0], precision=lax.Precision.HIGHEST,
                       preferred_element_type=jnp.float32) + b_ref[0]


def _ada_all_layers(c, w_ada, b_ada):
    depth = w_ada.shape[0]
    bn = c.shape[0]
    rows = 8
    c_pad = jnp.pad(c, ((0, rows - bn), (0, 0)))
    tn = 1024
    out = pl.pallas_call(
        _ada_kernel,
        out_shape=jax.ShapeDtypeStruct((depth, rows, 3 * D_MODEL), jnp.float32),
        grid=(depth, 3 * D_MODEL // tn),
        in_specs=[pl.BlockSpec((rows, D_MODEL), lambda l, j: (0, 0)),
                  pl.BlockSpec((1, D_MODEL, tn), lambda l, j: (l, 0, j)),
                  pl.BlockSpec((1, 1, tn), lambda l, j: (l, 0, j))],
        out_specs=pl.BlockSpec((1, rows, tn), lambda l, j: (l, 0, j)),
        name="adaln",
    )(c_pad, w_ada, b_ada.reshape(depth, 1, 3 * D_MODEL))
    return out[:, :bn]


def _rope_store(acc, c_ref, s1_ref, s2_ref, out_ref, col0, ncols):
    cc = c_ref[0]
    s1 = s1_ref[0]
    s2 = s2_ref[0]
    for c in range(ncols // LANES):
        lo = col0 + c * LANES
        t = acc[:, lo:lo + LANES]
        r = t * cc + pltpu.roll(t, ROT_HALF, 1) * s1 + pltpu.roll(t, LANES - ROT_HALF, 1) * s2
        out_ref[0, :, lo:lo + LANES] = r.astype(out_ref.dtype)


def _in_proj_kernel(x_ref, shift_ref, scale_ref, gain_ref, c_ref, s1_ref, s2_ref, w_ref, wvt_ref,
                    proj_ref, vt_ref, h_ref):
    j = pl.program_id(2)

    @pl.when(j == 0)
    def _():
        xf = x_ref[0]
        ms = jnp.mean(xf * xf, axis=-1, keepdims=True)
        y = xf * lax.rsqrt(ms + RMS_EPS) * gain_ref[...]
        h = (y * (1.0 + scale_ref[0]) + shift_ref[0]).astype(jnp.bfloat16)
        h_ref[...] = h
        vt = lax.dot_general(wvt_ref[...], h, _NT, preferred_element_type=jnp.float32)
        vt_ref[0] = vt.astype(vt_ref.dtype)

    acc = jnp.dot(h_ref[...], w_ref[...], preferred_element_type=jnp.float32)
    full_rope_tiles = ROPE_COLS // PROJ_TN
    part_cols = ROPE_COLS - full_rope_tiles * PROJ_TN

    @pl.when(j < full_rope_tiles)
    def _():
        _rope_store(acc, c_ref, s1_ref, s2_ref, proj_ref, 0, PROJ_TN)

    @pl.when(j == full_rope_tiles)
    def _():
        _rope_store(acc, c_ref, s1_ref, s2_ref, proj_ref, 0, part_cols)
        proj_ref[0, :, part_cols:] = acc[:, part_cols:].astype(proj_ref.dtype)

    @pl.when(j > full_rope_tiles)
    def _():
        proj_ref[0] = acc.astype(proj_ref.dtype)


def _in_proj(x, shift, scale, gain, tabs, w_main, w_vt):
    bn, s_len, _ = x.shape
    tm, tn = PROJ_TM, PROJ_TN
    tab_spec = pl.BlockSpec((1, tm, LANES), lambda b, i, j: (b, i, 0))
    mod_spec = pl.BlockSpec((1, 1, D_MODEL), lambda b, i, j: (b, 0, 0))
    return pl.pallas_call(
        _in_proj_kernel,
        out_shape=(jax.ShapeDtypeStruct((bn, s_len, MAIN_COLS), jnp.bfloat16),
                   jax.ShapeDtypeStruct((bn, A_WIDTH, s_len), jnp.bfloat16)),
        grid=(bn, s_len // tm, MAIN_COLS // tn),
        in_specs=[pl.BlockSpec((1, tm, D_MODEL), lambda b, i, j: (b, i, 0)),
                  mod_spec, mod_spec,
                  pl.BlockSpec((1, D_MODEL), lambda b, i, j: (0, 0)),
                  tab_spec, tab_spec, tab_spec,
                  pl.BlockSpec((D_MODEL, tn), lambda b, i, j: (0, j)),
                  pl.BlockSpec((A_WIDTH, D_MODEL), lambda b, i, j: (0, 0))],
        out_specs=(pl.BlockSpec((1, tm, tn), lambda b, i, j: (b, i, j)),
                   pl.BlockSpec((1, A_WIDTH, tm), lambda b, i, j: (b, 0, i))),
        scratch_shapes=[pltpu.VMEM((tm, D_MODEL), jnp.bfloat16)],
        compiler_params=pltpu.CompilerParams(
            dimension_semantics=("parallel", "parallel", "arbitrary"),
            vmem_limit_bytes=VMEM_LIMIT),
        name="in_proj",
    )(x, shift, scale, gain, *tabs, w_main, w_vt)


def _diff_attn_kernel(q_ref, k_ref, vt_ref, dl_ref, gain_ref, o_ref, *, lambda_init):
    h = pl.program_id(1)
    q = q_ref[0]
    k = k_ref[0]
    vt = vt_ref[0]
    pair = lax.broadcasted_iota(jnp.int32, q.shape, 1) // HEAD_DIM
    base = 2 * (h % 2)
    maps = []
    for c in range(2):
        qm = jnp.where(pair == base + c, q, jnp.zeros_like(q)) * (HEAD_DIM ** -0.5)
        st = lax.dot_general(k, qm, _NT, preferred_element_type=jnp.float32)
        m = jnp.max(st, axis=0, keepdims=True)
        e = jnp.exp(st - m)
        den = jnp.sum(e, axis=0, keepdims=True)
        ot = jnp.dot(vt, e.astype(jnp.bfloat16), preferred_element_type=jnp.float32)
        maps.append(ot * (1.0 / den))
    dl = dl_ref[...]
    lam = (jnp.exp(jnp.sum(dl[0:1] * dl[1:2], axis=1, keepdims=True))
           - jnp.exp(jnp.sum(dl[2:3] * dl[3:4], axis=1, keepdims=True)) + lambda_init)
    ot = maps[0] - lam * maps[1]
    ms = jnp.mean(ot * ot, axis=0, keepdims=True)
    y = ot * lax.rsqrt(ms + RMS_EPS) * gain_ref[...] * (1.0 - lambda_init)
    o_ref[0] = y.T.astype(o_ref.dtype)


def _diff_attention(proj, vt, diff_lambda, subln_gain, lambda_init):
    bn, s_len, _ = proj.shape
    tq = A_TQ
    qblk = COL_AQ // HEAD_BLOCK
    kblk = COL_AK // HEAD_BLOCK
    return pl.pallas_call(
        functools.partial(_diff_attn_kernel, lambda_init=lambda_init),
        out_shape=jax.ShapeDtypeStruct((bn, s_len, A_WIDTH), jnp.bfloat16),
        grid=(bn, A_HEADS, s_len // tq),
        in_specs=[pl.BlockSpec((1, tq, HEAD_BLOCK), lambda b, h, i: (b, i, qblk + h // 2)),
                  pl.BlockSpec((1, s_len, HEAD_BLOCK), lambda b, h, i: (b, 0, kblk + h // 2)),
                  pl.BlockSpec((1, A_VDIM, s_len), lambda b, h, i: (b, h, 0)),
                  pl.BlockSpec((4, HEAD_DIM), lambda b, h, i: (0, 0)),
                  pl.BlockSpec((A_VDIM, 1), lambda b, h, i: (0, 0))],
        out_specs=pl.BlockSpec((1, tq, A_VDIM), lambda b, h, i: (b, i, h)),
        compiler_params=pltpu.CompilerParams(
            dimension_semantics=("parallel", "parallel", "arbitrary"),
            vmem_limit_bytes=VMEM_LIMIT),
        name="diff_attn",
    )(proj, proj, vt, diff_lambda, subln_gain.reshape(A_VDIM, 1))


def _head_lane_masks(shape):
    head = lax.broadcasted_iota(jnp.int32, shape, 1) // HEAD_DIM
    return [head == h for h in range(HEAD_BLOCK // HEAD_DIM)]


def _band_attn_kernel(q_ref, k_ref, v_ref, o_ref, lse_ref, *, length):
    nblk = length // B_TQ
    qmask = _head_lane_masks((B_TQ, HEAD_BLOCK))

    def body(i, carry):
        q0 = pl.multiple_of(i * B_TQ, B_TQ)
        ks = pl.multiple_of(jnp.clip(q0 - B_REACH, 0, length - B_WIN), B_REACH)
        q = q_ref[0, pl.ds(q0, B_TQ), :]
        kw = k_ref[0, pl.ds(ks, B_WIN), :]
        vw = v_ref[0, pl.ds(ks, B_WIN), :]
        qi = q0 + lax.broadcasted_iota(jnp.int32, (B_TQ, B_WIN), 0)
        kj = ks + lax.broadcasted_iota(jnp.int32, (B_TQ, B_WIN), 1)
        valid = jnp.abs(kj - qi) <= B_REACH
        o_acc = jnp.zeros((B_TQ, HEAD_BLOCK), jnp.float32)
        l_acc = jnp.zeros((B_TQ, HEAD_BLOCK), jnp.float32)
        for h in range(B_HEADS):
            qm = jnp.where(qmask[h], q, jnp.zeros_like(q)) * (HEAD_DIM ** -0.5)
            s = lax.dot_general(qm, kw, _NT, preferred_element_type=jnp.float32)
            s = jnp.where(valid, s, MASK_VALUE)
            m = jnp.max(s, axis=-1, keepdims=True)
            e = jnp.exp(s - m)
            den = jnp.sum(e, axis=-1, keepdims=True)
            o = jnp.dot(e.astype(jnp.bfloat16), vw, preferred_element_type=jnp.float32)
            o_acc = jnp.where(qmask[h], o * (1.0 / den), o_acc)
            l_acc = jnp.where(qmask[h], m + jnp.log(den), l_acc)
        o_ref[0, pl.ds(q0, B_TQ), :] = o_acc
        lse_ref[0, pl.ds(q0, B_TQ), :] = l_acc
        return carry

    lax.fori_loop(0, nblk, body, 0)


def _band_attention(proj, group, dilation):
    bn, s_len, cols = proj.shape
    length = s_len // dilation
    assert length >= B_WIN and length % B_TQ == 0
    blocks_per_row = cols // HEAD_BLOCK
    view = proj.reshape(bn, length, dilation * cols)
    qblk = COL_BQ // HEAD_BLOCK + group
    kblk = COL_BK // HEAD_BLOCK + group
    vblk = COL_BV // HEAD_BLOCK + group

    def in_spec(blk):
        return pl.BlockSpec((1, length, HEAD_BLOCK), lambda b, m: (b, 0, m * blocks_per_row + blk))

    out_sds = jax.ShapeDtypeStruct((bn, length, dilation * HEAD_BLOCK), jnp.float32)
    out_spec = pl.BlockSpec((1, length, HEAD_BLOCK), lambda b, m: (b, 0, m))
    o, lse = pl.pallas_call(
        functools.partial(_band_attn_kernel, length=length),
        out_shape=(out_sds, out_sds),
        grid=(bn, dilation),
        in_specs=[in_spec(qblk), in_spec(kblk), in_spec(vblk)],
        out_specs=(out_spec, out_spec),
        compiler_params=pltpu.CompilerParams(
            dimension_semantics=("parallel", "parallel"),
            vmem_limit_bytes=VMEM_LIMIT),
        name=f"band_attn_d{dilation}",
    )(view, view, view)
    return o.reshape(bn, s_len, HEAD_BLOCK), lse.reshape(bn, s_len, HEAD_BLOCK)


def _na_table_kernel(rpb_ref, tab_ref, *, rows):
    h = pl.program_id(0)
    n_dr = 2 * NA_KH - 1
    n_dc = 2 * NA_KW - 1
    shape = (GRID_W, C_WIN)
    qc = lax.broadcasted_iota(jnp.int32, shape, 0)
    kk = lax.broadcasted_iota(jnp.int32, shape, 1)
    kc = kk & (GRID_W - 1)
    kr_l = kk // GRID_W
    dc_i = jnp.clip(kc - qc + (NA_KW - 1), 0, n_dc - 1)
    cs = jnp.clip(qc - NA_KW // 2, 0, GRID_W - NA_KW)
    col_valid = (kc >= cs) & (kc < cs + NA_KW)

    pair_rep = (0, 1, 2, rows // 2 - 2, rows // 2 - 1)
    dr_maps = []
    valid_maps = []
    for case in range(C_CASES):
        ip = pair_rep[case]
        ws = min(max(2 * ip - NA_KH // 2, 0), rows - C_KROWS)
        for qr_l in range(2):
            r = 2 * ip + qr_l
            rs = min(max(r - NA_KH // 2, 0), rows - NA_KH)
            kr = ws + kr_l
            valid_maps.append(col_valid & (kr >= rs) & (kr < rs + NA_KH))
            dr_maps.append(kr - r + (NA_KH - 1))

    def dr_body(dr, accs):
        base = (h * n_dr + dr) * n_dc
        row = jnp.zeros(shape, jnp.float32)
        for dc in range(n_dc):
            row = jnp.where(dc_i == dc, rpb_ref[base + dc], row)
        return tuple(jnp.where(dr_maps[n] == dr, row, accs[n]) for n in range(len(accs)))

    accs = lax.fori_loop(0, n_dr, dr_body,
                         tuple(jnp.zeros(shape, jnp.float32) for _ in range(2 * C_CASES)))
    for case in range(C_CASES):
        for qr_l in range(2):
            n = 2 * case + qr_l
            tab_ref[0, case, qr_l * GRID_W:(qr_l + 1) * GRID_W, :] = jnp.where(
                valid_maps[n], accs[n], MASK_VALUE)


def _na_tables(rpb, rows):
    flat = rpb.reshape(-1)
    return pl.pallas_call(
        functools.partial(_na_table_kernel, rows=rows),
        out_shape=jax.ShapeDtypeStruct((C_HEADS, C_CASES, C_PAIR, C_WIN), jnp.float32),
        grid=(C_HEADS,),
        in_specs=[pl.BlockSpec(memory_space=pltpu.SMEM)],
        out_specs=pl.BlockSpec((1, C_CASES, C_PAIR, C_WIN), lambda h: (h, 0, 0, 0)),
        name="na_tables",
    )(flat)


def _na_kernel(q_ref, k_ref, v_ref, tab_ref, o_ref, *, rows):
    rb = pl.program_id(1)
    qmask = _head_lane_masks((C_PAIR, HEAD_BLOCK))
    pairs_per_step = C_RB // 2
    for t in range(pairs_per_step):
        ip = rb * pairs_per_step + t
        ws = jnp.clip(2 * ip - NA_KH // 2, 0, rows - C_KROWS)
        case = ip - ws // 2
        k0 = pl.multiple_of(ws * GRID_W, C_PAIR)
        q = q_ref[0, t * C_PAIR:(t + 1) * C_PAIR, :]
        kw = k_ref[0, pl.ds(k0, C_WIN), :]
        vw = v_ref[0, pl.ds(k0, C_WIN), :]
        o_acc = jnp.zeros((C_PAIR, HEAD_BLOCK), jnp.float32)
        for h in range(C_HEADS):
            qm = jnp.where(qmask[h], q, jnp.zeros_like(q)) * (HEAD_DIM ** -0.5)
            s = lax.dot_general(qm, kw, _NT, preferred_element_type=jnp.float32)
            s = s + tab_ref[h, case]
            m = jnp.max(s, axis=-1, keepdims=True)
            e = jnp.exp(s - m)
            den = jnp.sum(e, axis=-1, keepdims=True)
            o = jnp.dot(e.astype(jnp.bfloat16), vw, preferred_element_type=jnp.float32)
            o_acc = jnp.where(qmask[h], o * (1.0 / den), o_acc)
        o_ref[0, t * C_PAIR:(t + 1) * C_PAIR, :] = o_acc.astype(o_ref.dtype)


def _neighborhood_attention(proj, tables):
    bn, s_len, _ = proj.shape
    rows = s_len // GRID_W
    tq = C_RB * GRID_W
    qblk = COL_CQ // HEAD_BLOCK
    kblk = COL_CK // HEAD_BLOCK
    vblk = COL_CV // HEAD_BLOCK
    return pl.pallas_call(
        functools.partial(_na_kernel, rows=rows),
        out_shape=jax.ShapeDtypeStruct((bn, s_len, C_WIDTH), jnp.bfloat16),
        grid=(bn, rows // C_RB),
        in_specs=[pl.BlockSpec((1, tq, HEAD_BLOCK), lambda b, i: (b, i, qblk)),
                  pl.BlockSpec((1, s_len, HEAD_BLOCK), lambda b, i: (b, 0, kblk)),
                  pl.BlockSpec((1, s_len, HEAD_BLOCK), lambda b, i: (b, 0, vblk)),
                  pl.BlockSpec((C_HEADS, C_CASES, C_PAIR, C_WIN), lambda b, i: (0, 0, 0, 0))],
        out_specs=pl.BlockSpec((1, tq, C_WIDTH), lambda b, i: (b, i, 0)),
        compiler_params=pltpu.CompilerParams(
            dimension_semantics=("parallel", "arbitrary"),
            vmem_limit_bytes=VMEM_LIMIT),
        name="na_attn",
    )(proj, proj, proj, tables)


def _merge_kernel(x_ref, gate_ref, ya_ref, ob0_ref, ob1_ref, ob2_ref, lb0_ref, lb1_ref, lb2_ref,
                  yc_ref, z_ref, g0_ref, g1_ref, g2_ref, wb_ref, wo_ref, fg_ref, o_ref, *, final_norm):
    l0, l1, l2 = lb0_ref[0], lb1_ref[0], lb2_ref[0]
    lm = jnp.maximum(jnp.maximum(l0, l1), l2)
    e0, e1, e2 = jnp.exp(l0 - lm), jnp.exp(l1 - lm), jnp.exp(l2 - lm)
    yb = (e0 * ob0_ref[0] + e1 * ob1_ref[0] + e2 * ob2_ref[0]) * (1.0 / (e0 + e1 + e2))

    z = z_ref[0].astype(jnp.float32)
    sz = z * _sigmoid(z)
    ya = (ya_ref[0].astype(jnp.float32) * sz[:, :A_WIDTH]).astype(jnp.bfloat16)
    ybz = (yb * sz[:, A_WIDTH:A_WIDTH + B_WIDTH]).astype(jnp.bfloat16)
    ycz = (yc_ref[0].astype(jnp.float32) * sz[:, A_WIDTH + B_WIDTH:]).astype(jnp.bfloat16)

    pa = jnp.dot(ya, wb_ref[:A_WIDTH, :], preferred_element_type=jnp.float32)
    pb = jnp.dot(ybz, wb_ref[A_WIDTH:A_WIDTH + B_WIDTH, :], preferred_element_type=jnp.float32)
    pc = jnp.dot(ycz, wb_ref[A_WIDTH + B_WIDTH:, :], preferred_element_type=jnp.float32)
    merged = _sigmoid(g0_ref[0].astype(jnp.float32)) * pa
    merged = merged + _sigmoid(g1_ref[0].astype(jnp.float32)) * pb
    merged = merged + _sigmoid(g2_ref[0].astype(jnp.float32)) * pc
    out = jnp.dot(merged.astype(jnp.bfloat16), wo_ref[...], preferred_element_type=jnp.float32)
    xn = x_ref[0] + gate_ref[0] * out
    if final_norm:
        ms = jnp.mean(xn * xn, axis=-1, keepdims=True)
        xn = xn * lax.rsqrt(ms + RMS_EPS) * fg_ref[...]
    o_ref[0] = xn


def _merge(x, gate, ya, ob, lb, yc, proj, w_branch, w_out, final_gain, final_norm):
    bn, s_len, _ = x.shape
    tm = MERGE_TM
    zblk = COL_Z // D_MODEL
    gblk = COL_G // D_MODEL

    def tok(width, blk=0):
        return pl.BlockSpec((1, tm, width), lambda b, i: (b, i, blk))

    return pl.pallas_call(
        functools.partial(_merge_kernel, final_norm=final_norm),
        out_shape=jax.ShapeDtypeStruct((bn, s_len, D_MODEL), jnp.float32),
        grid=(bn, s_len // tm),
        in_specs=[tok(D_MODEL),
                  pl.BlockSpec((1, 1, D_MODEL), lambda b, i: (b, 0, 0)),
                  tok(A_WIDTH),
                  tok(B_WIDTH), tok(B_WIDTH), tok(B_WIDTH),
                  tok(B_WIDTH), tok(B_WIDTH), tok(B_WIDTH),
                  tok(C_WIDTH),
                  tok(D_MODEL, zblk), tok(D_MODEL, gblk), tok(D_MODEL, gblk + 1), tok(D_MODEL, gblk + 2),
                  pl.BlockSpec((BR_WIDTH, D_MODEL), lambda b, i: (0, 0)),
                  pl.BlockSpec((D_MODEL, D_MODEL), lambda b, i: (0, 0)),
                  pl.BlockSpec((1, D_MODEL), lambda b, i: (0, 0))],
        out_specs=tok(D_MODEL),
        compiler_params=pltpu.CompilerParams(
            dimension_semantics=("parallel", "parallel"),
            vmem_limit_bytes=VMEM_LIMIT),
        name="merge",
    )(x, gate, ya, ob[0], ob[1], ob[2], lb[0], lb[1], lb[2], yc, proj, proj, proj, proj,
      w_branch, w_out, final_gain)


def _split_w_in(w_in_l):
    sizes = (A_QK_COLS, A_QK_COLS, A_WIDTH, B_COLS, B_COLS, B_COLS, C_WIDTH, C_WIDTH, C_WIDTH,
             BR_WIDTH, N_BRANCHES * D_MODEL)
    offs = np.concatenate([[0], np.cumsum(sizes)])
    parts = [w_in_l[:, offs[n]:offs[n + 1]] for n in range(len(sizes))]
    a_v = parts.pop(2)
    w_main = jnp.concatenate(parts, axis=1).astype(jnp.bfloat16)
    w_vt = a_v.T.astype(jnp.bfloat16)
    return w_main, w_vt


def kernel(x, c, positions, norm_gain, w_ada, b_ada, w_in, diff_lambda, diff_subln_gain, na_rpb,
           w_branch, w_out, final_gain):
    depth = w_in.shape[0]
    bn, s_len, _ = x.shape
    rows = s_len // GRID_W
    tabs = _rope_tables(positions)
    ada = _ada_all_layers(c, w_ada, b_ada)
    fg = final_gain.reshape(1, D_MODEL)
    for layer in range(depth):
        shift = ada[layer, :, None, :D_MODEL]
        scale = ada[layer, :, None, D_MODEL:2 * D_MODEL]
        gate = ada[layer, :, None, 2 * D_MODEL:]
        w_main, w_vt = _split_w_in(w_in[layer])
        proj, vt = _in_proj(x, shift, scale, norm_gain[layer].reshape(1, D_MODEL), tabs, w_main, w_vt)
        lambda_init = 0.8 - 0.6 * math.exp(-0.3 * layer)
        ya = _diff_attention(proj, vt, diff_lambda[layer], diff_subln_gain[layer], lambda_init)
        ob, lb = [], []
        for g, (window, dilation) in enumerate(B_PATTERNS):
            assert (window // 2) // dilation == B_REACH
            o_g, l_g = _band_attention(proj, g, dilation)
            ob.append(o_g)
            lb.append(l_g)
        yc = _neighborhood_attention(proj, _na_tables(na_rpb[layer], rows))
        x = _merge(x, gate, ya, ob, lb, yc, proj, w_branch[layer].astype(jnp.bfloat16),
                   w_out[layer].astype(jnp.bfloat16), fg, final_norm=(layer == depth - 1))
    return x
```

```python
import functools
import math

import numpy as np
import jax
import jax.numpy as jnp
from jax import lax
from jax.experimental import pallas as pl
from jax.experimental.pallas import tpu as pltpu

D_MODEL = 1024
HEAD_DIM = 64
ROT_DIM = HEAD_DIM // 4
ROT_HALF = ROT_DIM // 2
ROPE_THETA = 500000.0
RMS_EPS = 1e-6
MASK_VALUE = -1e30
LOG2_E = math.log2(math.e)

A_HEADS = 4
A_VDIM = 2 * HEAD_DIM
A_WIDTH = A_HEADS * A_VDIM
A_QK_COLS = A_HEADS * 2 * HEAD_DIM

B_PATTERNS = ((128, 1), (512, 4), (2048, 16))
B_GROUPS = 3
B_HEADS = 4
B_WIDTH = B_HEADS * HEAD_DIM
B_COLS = B_GROUPS * B_WIDTH

GRID_W = 64
C_HEADS = 4
NA_KH = 8
NA_KW = 16
C_WIDTH = C_HEADS * HEAD_DIM

N_BRANCHES = 3
BR_WIDTH = A_WIDTH + B_WIDTH + C_WIDTH

LANES = 128
HEAD_BLOCK = 4 * HEAD_DIM
VMEM_LIMIT = 56 * 1024 * 1024

COL_AQ = 0
COL_AK = COL_AQ + A_QK_COLS
COL_BQ = COL_AK + A_QK_COLS
COL_BK = COL_BQ + B_COLS
COL_BV = COL_BK + B_COLS
COL_CQ = COL_BV + B_COLS
COL_CK = COL_CQ + C_WIDTH
COL_CV = COL_CK + C_WIDTH
COL_Z = COL_CV + C_WIDTH
COL_G = COL_Z + BR_WIDTH
MAIN_COLS = COL_G + N_BRANCHES * D_MODEL
ROPE_COLS = COL_BV

PROJ_TM = 1024
PROJ_TN = 1024
A_TQ = 512
A_KC = 512
B_TQ = 128
B_UNROLL = 2
B_REACH = 64
B_WIN = B_TQ + 2 * B_REACH
C_PAIR = 2 * GRID_W
C_KROWS = 10
C_WIN = C_KROWS * GRID_W
C_CASES = 5
C_RB = 8
MERGE_TM = 512

_NT = (((1,), (1,)), ((), ()))


def _sigmoid(x):
    return 1.0 / (1.0 + jnp.exp(-x))


def _rope_table_kernel(pos_ref, inv_ref, c_ref, s1_ref, s2_ref):
    ang = pos_ref[0] * inv_ref[...]
    cs = jnp.cos(ang)
    sn = jnp.sin(ang)
    lane = lax.broadcasted_iota(jnp.int32, ang.shape, 1) & (HEAD_DIM - 1)
    c_ref[0] = jnp.where(lane < ROT_DIM, cs, 1.0)
    s1_ref[0] = jnp.where((lane >= ROT_HALF) & (lane < ROT_DIM), sn, 0.0)
    s2_ref[0] = jnp.where(lane < ROT_HALF, -sn, 0.0)


def _rope_tables(positions):
    bn, s_len = positions.shape
    inv = np.float32(ROPE_THETA) ** (-np.arange(0, ROT_DIM, 2, dtype=np.float32) / np.float32(ROT_DIM))
    inv_lane = np.zeros((1, LANES), np.float32)
    for l in range(LANES):
        if (l % HEAD_DIM) < ROT_DIM:
            inv_lane[0, l] = inv[(l % HEAD_DIM) % ROT_HALF]
    pos = positions.astype(jnp.float32)[..., None]
    ts = 1024
    tab = jax.ShapeDtypeStruct((bn, s_len, LANES), jnp.float32)
    spec = pl.BlockSpec((1, ts, LANES), lambda b, i: (b, i, 0))
    return pl.pallas_call(
        _rope_table_kernel,
        out_shape=(tab, tab, tab),
        grid=(bn, s_len // ts),
        in_specs=[pl.BlockSpec((1, ts, 1), lambda b, i: (b, i, 0)),
                  pl.BlockSpec((1, LANES), lambda b, i: (0, 0))],
        out_specs=(spec, spec, spec),
        name="rope_tables",
    )(pos, jnp.asarray(inv_lane))


def _ada_kernel(c_ref, w_ref, b_ref, o_ref):
    c = c_ref[...]
    c_act = c * _sigmoid(c)
    o_ref[0] = jnp.dot(c_act, w_ref[0], precision=lax.Precision.HIGHEST,
                       preferred_element_type=jnp.float32) + b_ref[0]


def _ada_all_layers(c, w_ada, b_ada):
    depth = w_ada.shape[0]
    bn = c.shape[0]
    rows = 8
    c_pad = jnp.pad(c, ((0, rows - bn), (0, 0)))
    tn = 1024
    out = pl.pallas_call(
        _ada_kernel,
        out_shape=jax.ShapeDtypeStruct((depth, rows, 3 * D_MODEL), jnp.float32),
        grid=(depth, 3 * D_MODEL // tn),
        in_specs=[pl.BlockSpec((rows, D_MODEL), lambda l, j: (0, 0)),
                  pl.BlockSpec((1, D_MODEL, tn), lambda l, j: (l, 0, j)),
                  pl.BlockSpec((1, 1, tn), lambda l, j: (l, 0, j))],
        out_specs=pl.BlockSpec((1, rows, tn), lambda l, j: (l, 0, j)),
        name="adaln",
    )(c_pad, w_ada, b_ada.reshape(depth, 1, 3 * D_MODEL))
    return out[:, :bn]


def _rope_store(acc, c_ref, s1_ref, s2_ref, out_ref, col0, ncols):
    cc = c_ref[0]
    s1 = s1_ref[0]
    s2 = s2_ref[0]
    for c in range(ncols // LANES):
        lo = col0 + c * LANES
        t = acc[:, lo:lo + LANES]
        r = t * cc + pltpu.roll(t, ROT_HALF, 1) * s1 + pltpu.roll(t, LANES - ROT_HALF, 1) * s2
        out_ref[0, :, lo:lo + LANES] = r.astype(out_ref.dtype)


def _in_proj_kernel(x_ref, shift_ref, scale_ref, gain_ref, c_ref, s1_ref, s2_ref, w_ref, wvt_ref,
                    proj_ref, vt_ref, h_ref):
    j = pl.program_id(2)

    @pl.when(j == 0)
    def _():
        xf = x_ref[0]
        ms = jnp.mean(xf * xf, axis=-1, keepdims=True)
        y = xf * lax.rsqrt(ms + RMS_EPS) * gain_ref[...]
        h = (y * (1.0 + scale_ref[0]) + shift_ref[0]).astype(jnp.bfloat16)
        h_ref[...] = h
        vt = lax.dot_general(wvt_ref[...], h, _NT, preferred_element_type=jnp.float32)
        vt_ref[0] = vt.astype(vt_ref.dtype)

    acc = jnp.dot(h_ref[...], w_ref[...], preferred_element_type=jnp.float32)
    full_rope_tiles = ROPE_COLS // PROJ_TN
    part_cols = ROPE_COLS - full_rope_tiles * PROJ_TN

    @pl.when(j < full_rope_tiles)
    def _():
        _rope_store(acc, c_ref, s1_ref, s2_ref, proj_ref, 0, PROJ_TN)

    @pl.when(j == full_rope_tiles)
    def _():
        _rope_store(acc, c_ref, s1_ref, s2_ref, proj_ref, 0, part_cols)
        proj_ref[0, :, part_cols:] = acc[:, part_cols:].astype(proj_ref.dtype)

    @pl.when(j > full_rope_tiles)
    def _():
        proj_ref[0] = acc.astype(proj_ref.dtype)


def _in_proj(x, shift, scale, gain, tabs, w_main, w_vt):
    bn, s_len, _ = x.shape
    tm, tn = PROJ_TM, PROJ_TN
    tab_spec = pl.BlockSpec((1, tm, LANES), lambda b, i, j: (b, i, 0))
    mod_spec = pl.BlockSpec((1, 1, D_MODEL), lambda b, i, j: (b, 0, 0))
    return pl.pallas_call(
        _in_proj_kernel,
        out_shape=(jax.ShapeDtypeStruct((bn, s_len, MAIN_COLS), jnp.bfloat16),
                   jax.ShapeDtypeStruct((bn, A_WIDTH, s_len), jnp.bfloat16)),
        grid=(bn, s_len // tm, MAIN_COLS // tn),
        in_specs=[pl.BlockSpec((1, tm, D_MODEL), lambda b, i, j: (b, i, 0)),
                  mod_spec, mod_spec,
                  pl.BlockSpec((1, D_MODEL), lambda b, i, j: (0, 0)),
                  tab_spec, tab_spec, tab_spec,
                  pl.BlockSpec((D_MODEL, tn), lambda b, i, j: (0, j)),
                  pl.BlockSpec((A_WIDTH, D_MODEL), lambda b, i, j: (0, 0))],
        out_specs=(pl.BlockSpec((1, tm, tn), lambda b, i, j: (b, i, j)),
                   pl.BlockSpec((1, A_WIDTH, tm), lambda b, i, j: (b, 0, i))),
        scratch_shapes=[pltpu.VMEM((tm, D_MODEL), jnp.bfloat16)],
        compiler_params=pltpu.CompilerParams(
            dimension_semantics=("parallel", "parallel", "arbitrary"),
            vmem_limit_bytes=VMEM_LIMIT),
        name="in_proj",
    )(x, shift, scale, gain, *tabs, w_main, w_vt)


def _diff_attn_kernel(q_ref, k_ref, vt_ref, dl_ref, gain_ref, o_ref,
                      qm_ref, sa_ref, sb_ref, m_ref, l_ref, acc_ref, *, lambda_init, s_len):
    h = pl.program_id(1)
    tq = q_ref.shape[1]
    n_chunks = s_len // A_KC
    q = q_ref[0]
    pair = lax.broadcasted_iota(jnp.int32, q.shape, 1) // HEAD_DIM
    base = 2 * (h % 2)
    for c in range(2):
        qsel = jnp.where(pair == base + c, q, jnp.zeros_like(q)).astype(jnp.float32)
        qm_ref[c] = (qsel * (HEAD_DIM ** -0.5 * LOG2_E)).astype(qm_ref.dtype)
        m_ref[c] = jnp.full((1, tq), MASK_VALUE, jnp.float32)
        l_ref[c] = jnp.zeros((1, tq), jnp.float32)
        acc_ref[c] = jnp.zeros((A_VDIM, tq), jnp.float32)

    def scores(i, s_ref):
        kc = k_ref[0, pl.ds(pl.multiple_of(i * A_KC, A_KC), A_KC), :]
        for c in range(2):
            s_ref[c] = lax.dot_general(kc, qm_ref[c], _NT,
                                       preferred_element_type=jnp.float32)

    def update(i, s_ref):
        vt_c = vt_ref[0, :, pl.ds(pl.multiple_of(i * A_KC, A_KC), A_KC)]
        for c in range(2):
            st = s_ref[c]
            m_old = m_ref[c]
            m_new = jnp.maximum(m_old, jnp.max(st, axis=0, keepdims=True))
            alpha = jnp.exp2(m_old - m_new)
            e = jnp.exp2(st - m_new)
            l_ref[c] = alpha * l_ref[c] + jnp.sum(e, axis=0, keepdims=True)
            pv = jnp.dot(vt_c, e.astype(jnp.bfloat16), preferred_element_type=jnp.float32)
            acc_ref[c] = alpha * acc_ref[c] + pv
            m_ref[c] = m_new

    scores(0, sa_ref)

    def body(j, carry):
        scores(2 * j + 1, sb_ref)
        update(2 * j, sa_ref)
        scores(2 * j + 2, sa_ref)
        update(2 * j + 1, sb_ref)
        return carry

    lax.fori_loop(0, n_chunks // 2 - 1, body, 0)
    scores(n_chunks - 1, sb_ref)
    update(n_chunks - 2, sa_ref)
    update(n_chunks - 1, sb_ref)

    dl = dl_ref[...]
    lam = (jnp.exp(jnp.sum(dl[0:1] * dl[1:2], axis=1, keepdims=True))
           - jnp.exp(jnp.sum(dl[2:3] * dl[3:4], axis=1, keepdims=True)) + lambda_init)
    ot = acc_ref[0] * (1.0 / l_ref[0]) - lam * (acc_ref[1] * (1.0 / l_ref[1]))
    ms = jnp.mean(ot * ot, axis=0, keepdims=True)
    y = ot * lax.rsqrt(ms + RMS_EPS) * gain_ref[...] * (1.0 - lambda_init)
    o_ref[0] = y.T.astype(o_ref.dtype)


def _diff_attention(proj, vt, diff_lambda, subln_gain, lambda_init):
    bn, s_len, _ = proj.shape
    tq = A_TQ
    qblk = COL_AQ // HEAD_BLOCK
    kblk = COL_AK // HEAD_BLOCK
    assert s_len % (2 * A_KC) == 0
    return pl.pallas_call(
        functools.partial(_diff_attn_kernel, lambda_init=lambda_init, s_len=s_len),
        out_shape=jax.ShapeDtypeStruct((bn, s_len, A_WIDTH), jnp.bfloat16),
        grid=(bn, A_HEADS, s_len // tq),
        in_specs=[pl.BlockSpec((1, tq, HEAD_BLOCK), lambda b, h, i: (b, i, qblk + h // 2)),
                  pl.BlockSpec((1, s_len, HEAD_BLOCK), lambda b, h, i: (b, 0, kblk + h // 2)),
                  pl.BlockSpec((1, A_VDIM, s_len), lambda b, h, i: (b, h, 0)),
                  pl.BlockSpec((4, HEAD_DIM), lambda b, h, i: (0, 0)),
                  pl.BlockSpec((A_VDIM, 1), lambda b, h, i: (0, 0))],
        out_specs=pl.BlockSpec((1, tq, A_VDIM), lambda b, h, i: (b, i, h)),
        scratch_shapes=[pltpu.VMEM((2, tq, HEAD_BLOCK), jnp.bfloat16),
                        pltpu.VMEM((2, A_KC, tq), jnp.float32),
                        pltpu.VMEM((2, A_KC, tq), jnp.float32),
                        pltpu.VMEM((2, 1, tq), jnp.float32),
                        pltpu.VMEM((2, 1, tq), jnp.float32),
                        pltpu.VMEM((2, A_VDIM, tq), jnp.float32)],
        compiler_params=pltpu.CompilerParams(
            dimension_semantics=("parallel", "parallel", "arbitrary"),
            vmem_limit_bytes=VMEM_LIMIT),
        name="diff_attn",
    )(proj, proj, vt, diff_lambda, subln_gain.reshape(A_VDIM, 1))


def _head_lane_masks(shape):
    head = lax.broadcasted_iota(jnp.int32, shape, 1) // HEAD_DIM
    return [head == h for h in range(HEAD_BLOCK // HEAD_DIM)]


def _stack_heads(q, qmask):
    zero = jnp.zeros_like(q)
    return jnp.concatenate([jnp.where(mk, q, zero) for mk in qmask], axis=0) * (HEAD_DIM ** -0.5)


def _unstack_heads(x, qmask, tq):
    out = x[:tq]
    for h in range(1, len(qmask)):
        out = jnp.where(qmask[h], x[h * tq:(h + 1) * tq], out)
    return out


def _band_attn_kernel(q_ref, k_ref, v_ref, o_ref, lse_ref, *scratch, dilation, length):
    m = pl.program_id(1)
    nblk = length // B_TQ
    qmask = _head_lane_masks((B_TQ, HEAD_BLOCK))
    if dilation > 1:
        stage_ref, tmp_ref, qd_ref, kd_ref, vd_ref = scratch
        s_len = length * dilation

        @pl.when(m == 0)
        def _():
            for src, dst in ((q_ref, qd_ref), (k_ref, kd_ref), (v_ref, vd_ref)):
                for slab in range(HEAD_BLOCK // LANES):
                    lanes = slice(slab * LANES, (slab + 1) * LANES)
                    stage_ref[...] = src[0, :, lanes].astype(jnp.float32)
                    for m4 in range(4):
                        first = stage_ref[pl.ds(m4, s_len // 4, stride=4), :]
                        if dilation == 4:
                            dst[m4, :, lanes] = first.astype(dst.dtype)
                        else:
                            tmp_ref[...] = first
                            for mm in range(4):
                                dst[m4 + 4 * mm, :, lanes] = tmp_ref[
                                    pl.ds(mm, length, stride=4), :].astype(dst.dtype)

        q_src, k_src, v_src = qd_ref.at[m], kd_ref.at[m], vd_ref.at[m]
    else:
        q_src, k_src, v_src = q_ref.at[0], k_ref.at[0], v_ref.at[0]

    def out_rows(q0):
        if dilation > 1:
            return pl.ds(q0 * dilation + m, B_TQ, stride=dilation)
        return pl.ds(q0, B_TQ)

    def one_block(q0):
        ks = pl.multiple_of(jnp.clip(q0 - B_REACH, 0, length - B_WIN), B_REACH)
        q = q_src[pl.ds(q0, B_TQ), :]
        kw = k_src[pl.ds(ks, B_WIN), :]
        vw = v_src[pl.ds(ks, B_WIN), :]
        qs = _stack_heads(q, qmask)
        s = lax.dot_general(qs, kw, _NT, preferred_element_type=jnp.float32)
        row = lax.broadcasted_iota(jnp.int32, s.shape, 0) & (B_TQ - 1)
        col = lax.broadcasted_iota(jnp.int32, s.shape, 1)
        valid = jnp.abs(ks + col - q0 - row) <= B_REACH
        s = jnp.where(valid, s, MASK_VALUE)
        mx = jnp.max(s, axis=-1, keepdims=True)
        e = jnp.exp(s - mx)
        den = jnp.sum(e, axis=-1, keepdims=True)
        o = jnp.dot(e.astype(jnp.bfloat16), vw, preferred_element_type=jnp.float32)
        o = o * (1.0 / den)
        lse = jnp.broadcast_to(mx + jnp.log(den), o.shape)
        o_acc = _unstack_heads(o, qmask, B_TQ)
        l_acc = _unstack_heads(lse, qmask, B_TQ)
        rows = out_rows(q0)
        for slab in range(HEAD_BLOCK // LANES):
            lanes = slice(slab * LANES, (slab + 1) * LANES)
            o_ref[0, slab, rows, :] = o_acc[:, lanes]
            lse_ref[0, slab, rows, :] = l_acc[:, lanes]

    def body(i, carry):
        for u in range(B_UNROLL):
            one_block(pl.multiple_of((i * B_UNROLL + u) * B_TQ, B_TQ))
        return carry

    lax.fori_loop(0, nblk // B_UNROLL, body, 0)


def _band_attention(proj, group, dilation):
    bn, s_len, _ = proj.shape
    length = s_len // dilation
    assert length >= B_WIN and length % (B_TQ * B_UNROLL) == 0 and dilation in (1, 4, 16)
    qblk = COL_BQ // HEAD_BLOCK + group
    kblk = COL_BK // HEAD_BLOCK + group
    vblk = COL_BV // HEAD_BLOCK + group

    def in_spec(blk):
        return pl.BlockSpec((1, s_len, HEAD_BLOCK), lambda b, m: (b, 0, blk))

    slabs = HEAD_BLOCK // LANES
    out_sds = jax.ShapeDtypeStruct((bn, slabs, s_len, LANES), jnp.float32)
    out_spec = pl.BlockSpec((1, slabs, s_len, LANES), lambda b, m: (b, 0, 0, 0))
    scratch = []
    if dilation > 1:
        regrouped = pltpu.VMEM((dilation, length, HEAD_BLOCK), jnp.bfloat16)
        scratch = [pltpu.VMEM((s_len, LANES), jnp.float32), pltpu.VMEM((s_len // 4, LANES), jnp.float32),
                   regrouped, regrouped, regrouped]
    return pl.pallas_call(
        functools.partial(_band_attn_kernel, dilation=dilation, length=length),
        out_shape=(out_sds, out_sds),
        grid=(bn, dilation),
        in_specs=[in_spec(qblk), in_spec(kblk), in_spec(vblk)],
        out_specs=(out_spec, out_spec),
        scratch_shapes=scratch,
        compiler_params=pltpu.CompilerParams(
            dimension_semantics=("parallel", "arbitrary"),
            vmem_limit_bytes=VMEM_LIMIT),
        name=f"band_attn_d{dilation}",
    )(proj, proj, proj)


def _na_table_kernel(rpb_ref, tab_ref, *, rows):
    h = pl.program_id(0)
    n_dr = 2 * NA_KH - 1
    n_dc = 2 * NA_KW - 1
    shape = (GRID_W, C_WIN)
    qc = lax.broadcasted_iota(jnp.int32, shape, 0)
    kk = lax.broadcasted_iota(jnp.int32, shape, 1)
    kc = kk & (GRID_W - 1)
    kr_l = kk // GRID_W
    dc_i = jnp.clip(kc - qc + (NA_KW - 1), 0, n_dc - 1)
    cs = jnp.clip(qc - NA_KW // 2, 0, GRID_W - NA_KW)
    col_valid = (kc >= cs) & (kc < cs + NA_KW)

    pair_rep = (0, 1, 2, rows // 2 - 2, rows // 2 - 1)
    dr_maps = []
    valid_maps = []
    for case in range(C_CASES):
        ip = pair_rep[case]
        ws = min(max(2 * ip - NA_KH // 2, 0), rows - C_KROWS)
        for qr_l in range(2):
            r = 2 * ip + qr_l
            rs = min(max(r - NA_KH // 2, 0), rows - NA_KH)
            kr = ws + kr_l
            valid_maps.append(col_valid & (kr >= rs) & (kr < rs + NA_KH))
            dr_maps.append(kr - r + (NA_KH - 1))

    def dr_body(dr, accs):
        base = (h * n_dr + dr) * n_dc
        row = jnp.zeros(shape, jnp.float32)
        for dc in range(n_dc):
            row = jnp.where(dc_i == dc, rpb_ref[base + dc], row)
        return tuple(jnp.where(dr_maps[n] == dr, row, accs[n]) for n in range(len(accs)))

    accs = lax.fori_loop(0, n_dr, dr_body,
                         tuple(jnp.zeros(shape, jnp.float32) for _ in range(2 * C_CASES)))
    for case in range(C_CASES):
        for qr_l in range(2):
            n = 2 * case + qr_l
            tab_ref[case, 0, qr_l * GRID_W:(qr_l + 1) * GRID_W, :] = jnp.where(
                valid_maps[n], accs[n], MASK_VALUE)


def _na_tables(rpb, rows):
    flat = rpb.reshape(-1)
    return pl.pallas_call(
        functools.partial(_na_table_kernel, rows=rows),
        out_shape=jax.ShapeDtypeStruct((C_CASES, C_HEADS, C_PAIR, C_WIN), jnp.float32),
        grid=(C_HEADS,),
        in_specs=[pl.BlockSpec(memory_space=pltpu.SMEM)],
        out_specs=pl.BlockSpec((C_CASES, 1, C_PAIR, C_WIN), lambda h: (0, h, 0, 0)),
        name="na_tables",
    )(flat)


def _na_kernel(q_ref, k_ref, v_ref, tab_ref, o_ref, *, rows):
    rb = pl.program_id(1)
    qmask = _head_lane_masks((C_PAIR, HEAD_BLOCK))
    pairs_per_step = C_RB // 2
    for t in range(pairs_per_step):
        ip = rb * pairs_per_step + t
        ws = jnp.clip(2 * ip - NA_KH // 2, 0, rows - C_KROWS)
        case = ip - ws // 2
        k0 = pl.multiple_of(ws * GRID_W, C_PAIR)
        q = q_ref[0, t * C_PAIR:(t + 1) * C_PAIR, :]
        kw = k_ref[0, pl.ds(k0, C_WIN), :]
        vw = v_ref[0, pl.ds(k0, C_WIN), :]
        qs = _stack_heads(q, qmask)
        s = lax.dot_general(qs, kw, _NT, preferred_element_type=jnp.float32)
        s = s + tab_ref[case]
        mx = jnp.max(s, axis=-1, keepdims=True)
        e = jnp.exp(s - mx)
        den = jnp.sum(e, axis=-1, keepdims=True)
        o = jnp.dot(e.astype(jnp.bfloat16), vw, preferred_element_type=jnp.float32)
        o_acc = _unstack_heads(o * (1.0 / den), qmask, C_PAIR)
        o_ref[0, t * C_PAIR:(t + 1) * C_PAIR, :] = o_acc.astype(o_ref.dtype)


def _neighborhood_attention(proj, tables):
    bn, s_len, _ = proj.shape
    rows = s_len // GRID_W
    tq = C_RB * GRID_W
    qblk = COL_CQ // HEAD_BLOCK
    kblk = COL_CK // HEAD_BLOCK
    vblk = COL_CV // HEAD_BLOCK
    return pl.pallas_call(
        functools.partial(_na_kernel, rows=rows),
        out_shape=jax.ShapeDtypeStruct((bn, s_len, C_WIDTH), jnp.bfloat16),
        grid=(bn, rows // C_RB),
        in_specs=[pl.BlockSpec((1, tq, HEAD_BLOCK), lambda b, i: (b, i, qblk)),
                  pl.BlockSpec((1, s_len, HEAD_BLOCK), lambda b, i: (b, 0, kblk)),
                  pl.BlockSpec((1, s_len, HEAD_BLOCK), lambda b, i: (b, 0, vblk)),
                  pl.BlockSpec((C_CASES, C_HEADS * C_PAIR, C_WIN), lambda b, i: (0, 0, 0))],
        out_specs=pl.BlockSpec((1, tq, C_WIDTH), lambda b, i: (b, i, 0)),
        compiler_params=pltpu.CompilerParams(
            dimension_semantics=("parallel", "arbitrary"),
            vmem_limit_bytes=VMEM_LIMIT),
        name="na_attn",
    )(proj, proj, proj, tables.reshape(C_CASES, C_HEADS * C_PAIR, C_WIN))


def _merge_kernel(x_ref, gate_ref, ya_ref, ob0_ref, ob1_ref, ob2_ref, lb0_ref, lb1_ref, lb2_ref,
                  yc_ref, z_ref, g0_ref, g1_ref, g2_ref, wb_ref, wo_ref, fg_ref, o_ref, *, final_norm):
    yb_slabs = []
    for slab in range(HEAD_BLOCK // LANES):
        l0, l1, l2 = lb0_ref[0, slab], lb1_ref[0, slab], lb2_ref[0, slab]
        lm = jnp.maximum(jnp.maximum(l0, l1), l2)
        e0, e1, e2 = jnp.exp(l0 - lm), jnp.exp(l1 - lm), jnp.exp(l2 - lm)
        yb_slabs.append((e0 * ob0_ref[0, slab] + e1 * ob1_ref[0, slab] + e2 * ob2_ref[0, slab])
                        * (1.0 / (e0 + e1 + e2)))
    yb = jnp.concatenate(yb_slabs, axis=-1)

    z = z_ref[0].astype(jnp.float32)
    sz = z * _sigmoid(z)
    ya = (ya_ref[0].astype(jnp.float32) * sz[:, :A_WIDTH]).astype(jnp.bfloat16)
    ybz = (yb * sz[:, A_WIDTH:A_WIDTH + B_WIDTH]).astype(jnp.bfloat16)
    ycz = (yc_ref[0].astype(jnp.float32) * sz[:, A_WIDTH + B_WIDTH:]).astype(jnp.bfloat16)

    pa = jnp.dot(ya, wb_ref[:A_WIDTH, :], preferred_element_type=jnp.float32)
    pb = jnp.dot(ybz, wb_ref[A_WIDTH:A_WIDTH + B_WIDTH, :], preferred_element_type=jnp.float32)
    pc = jnp.dot(ycz, wb_ref[A_WIDTH + B_WIDTH:, :], preferred_element_type=jnp.float32)
    merged = _sigmoid(g0_ref[0].astype(jnp.float32)) * pa
    merged = merged + _sigmoid(g1_ref[0].astype(jnp.float32)) * pb
    merged = merged + _sigmoid(g2_ref[0].astype(jnp.float32)) * pc
    out = jnp.dot(merged.astype(jnp.bfloat16), wo_ref[...], preferred_element_type=jnp.float32)
    xn = x_ref[0] + gate_ref[0] * out
    if final_norm:
        ms = jnp.mean(xn * xn, axis=-1, keepdims=True)
        xn = xn * lax.rsqrt(ms + RMS_EPS) * fg_ref[...]
    o_ref[0] = xn


def _merge(x, gate, ya, ob, lb, yc, proj, w_branch, w_out, final_gain, final_norm):
    bn, s_len, _ = x.shape
    tm = MERGE_TM
    zblk = COL_Z // D_MODEL
    gblk = COL_G // D_MODEL

    def tok(width, blk=0):
        return pl.BlockSpec((1, tm, width), lambda b, i: (b, i, blk))

    slab = pl.BlockSpec((1, HEAD_BLOCK // LANES, tm, LANES), lambda b, i: (b, 0, i, 0))

    return pl.pallas_call(
        functools.partial(_merge_kernel, final_norm=final_norm),
        out_shape=jax.ShapeDtypeStruct((bn, s_len, D_MODEL), jnp.float32),
        grid=(bn, s_len // tm),
        in_specs=[tok(D_MODEL),
                  pl.BlockSpec((1, 1, D_MODEL), lambda b, i: (b, 0, 0)),
                  tok(A_WIDTH),
                  slab, slab, slab, slab, slab, slab,
                  tok(C_WIDTH),
                  tok(D_MODEL, zblk), tok(D_MODEL, gblk), tok(D_MODEL, gblk + 1), tok(D_MODEL, gblk + 2),
                  pl.BlockSpec((BR_WIDTH, D_MODEL), lambda b, i: (0, 0)),
                  pl.BlockSpec((D_MODEL, D_MODEL), lambda b, i: (0, 0)),
                  pl.BlockSpec((1, D_MODEL), lambda b, i: (0, 0))],
        out_specs=tok(D_MODEL),
        compiler_params=pltpu.CompilerParams(
            dimension_semantics=("parallel", "parallel"),
            vmem_limit_bytes=VMEM_LIMIT),
        name="merge",
    )(x, gate, ya, ob[0], ob[1], ob[2], lb[0], lb[1], lb[2], yc, proj, proj, proj, proj,
      w_branch, w_out, final_gain)


def _split_w_in(w_in_l):
    sizes = (A_QK_COLS, A_QK_COLS, A_WIDTH, B_COLS, B_COLS, B_COLS, C_WIDTH, C_WIDTH, C_WIDTH,
             BR_WIDTH, N_BRANCHES * D_MODEL)
    offs = np.concatenate([[0], np.cumsum(sizes)])
    parts = [w_in_l[:, offs[n]:offs[n + 1]] for n in range(len(sizes))]
    a_v = parts.pop(2)
    w_main = jnp.concatenate(parts, axis=1).astype(jnp.bfloat16)
    w_vt = a_v.T.astype(jnp.bfloat16)
    return w_main, w_vt


def kernel(x, c, positions, norm_gain, w_ada, b_ada, w_in, diff_lambda, diff_subln_gain, na_rpb,
           w_branch, w_out, final_gain):
    depth = w_in.shape[0]
    bn, s_len, _ = x.shape
    rows = s_len // GRID_W
    tabs = _rope_tables(positions)
    ada = _ada_all_layers(c, w_ada, b_ada)
    fg = final_gain.reshape(1, D_MODEL)
    for layer in range(depth):
        shift = ada[layer, :, None, :D_MODEL]
        scale = ada[layer, :, None, D_MODEL:2 * D_MODEL]
        gate = ada[layer, :, None, 2 * D_MODEL:]
        w_main, w_vt = _split_w_in(w_in[layer])
        proj, vt = _in_proj(x, shift, scale, norm_gain[layer].reshape(1, D_MODEL), tabs, w_main, w_vt)
        lambda_init = 0.8 - 0.6 * math.exp(-0.3 * layer)
        ya = _diff_attention(proj, vt, diff_lambda[layer], diff_subln_gain[layer], lambda_init)
        ob, lb = [], []
        for g, (window, dilation) in enumerate(B_PATTERNS):
            assert (window // 2) // dilation == B_REACH
            o_g, l_g = _band_attention(proj, g, dilation)
            ob.append(o_g)
            lb.append(l_g)
        yc = _neighborhood_attention(proj, _na_tables(na_rpb[layer], rows))
        x = _merge(x, gate, ya, ob, lb, yc, proj, w_branch[layer].astype(jnp.bfloat16),
                   w_out[layer].astype(jnp.bfloat16), fg, final_norm=(layer == depth - 1))
    return x
```

```python
import functools
import math

import numpy as np
import jax
import jax.numpy as jnp
from jax import lax
from jax.experimental import pallas as pl
from jax.experimental.pallas import tpu as pltpu

D_MODEL = 1024
HEAD_DIM = 64
ROT_DIM = HEAD_DIM // 4
ROT_HALF = ROT_DIM // 2
ROPE_THETA = 500000.0
RMS_EPS = 1e-6
MASK_VALUE = -1e30
LOG2_E = math.log2(math.e)

A_HEADS = 4
A_VDIM = 2 * HEAD_DIM
A_WIDTH = A_HEADS * A_VDIM
A_QK_COLS = A_HEADS * 2 * HEAD_DIM

B_PATTERNS = ((128, 1), (512, 4), (2048, 16))
B_GROUPS = 3
B_HEADS = 4
B_WIDTH = B_HEADS * HEAD_DIM
B_COLS = B_GROUPS * B_WIDTH

GRID_W = 64
C_HEADS = 4
NA_KH = 8
NA_KW = 16
C_WIDTH = C_HEADS * HEAD_DIM

N_BRANCHES = 3
BR_WIDTH = A_WIDTH + B_WIDTH + C_WIDTH

LANES = 128
HEAD_BLOCK = 4 * HEAD_DIM
ROPE_LANES = (HEAD_BLOCK // HEAD_DIM) * ROT_HALF
VMEM_LIMIT = 56 * 1024 * 1024

COL_AQ = 0
COL_AK = COL_AQ + A_QK_COLS
COL_BQ = COL_AK + A_QK_COLS
COL_BK = COL_BQ + B_COLS
COL_BV = COL_BK + B_COLS
COL_CQ = COL_BV + B_COLS
COL_CK = COL_CQ + C_WIDTH
COL_CV = COL_CK + C_WIDTH
COL_Z = COL_CV + C_WIDTH
COL_G = COL_Z + BR_WIDTH
MAIN_COLS = COL_G + N_BRANCHES * D_MODEL
ROPE_COLS = COL_BV

PROJ_TM = 1024
PROJ_TN = 1024
A_TQ = 1024
A_KC = 512
A_VROWS = A_VDIM + 16
B_TQ_MAX = 128
B_UNROLL_MAX = 4
B_REACH = 64
B_CASES = 3
C_PAIR = 2 * GRID_W
C_KROWS = 10
C_WIN = C_KROWS * GRID_W
C_CASES = 5
C_RB = 8
MERGE_TM = 512

_NT = (((1,), (1,)), ((), ()))


def _sigmoid(x):
    return 1.0 / (1.0 + jnp.exp(-x))


def _rope_layout_perm():
    perm = np.zeros(HEAD_BLOCK, np.int32)
    rest = HEAD_DIM - ROT_DIM
    for p in range(HEAD_BLOCK):
        l, c = p % LANES, p // LANES
        if l < ROPE_LANES:
            perm[p] = HEAD_DIM * (l // ROT_HALF) + c * ROT_HALF + l % ROT_HALF
        else:
            r = l - ROPE_LANES
            perm[p] = HEAD_DIM * (2 * c + r // rest) + ROT_DIM + r % rest
    assert sorted(perm.tolist()) == list(range(HEAD_BLOCK))
    return perm


def _rope_layout_head(col):
    l = col % LANES
    c = col // LANES
    rest = HEAD_DIM - ROT_DIM
    return jnp.where(l < ROPE_LANES, l // ROT_HALF, 2 * c + jnp.where(l >= ROPE_LANES + rest, 1, 0))


def _rope_table_kernel(pos_ref, inv_ref, c_ref, s_ref):
    ang = pos_ref[0] * inv_ref[...]
    lane = lax.broadcasted_iota(jnp.int32, ang.shape, 1)
    c_ref[0] = jnp.where(lane < ROPE_LANES, jnp.cos(ang), 1.0)
    s_ref[0] = jnp.where(lane < ROPE_LANES, jnp.sin(ang), 0.0)


def _rope_tables(positions):
    bn, s_len = positions.shape
    inv = np.float32(ROPE_THETA) ** (-np.arange(0, ROT_DIM, 2, dtype=np.float32) / np.float32(ROT_DIM))
    inv_lane = np.zeros((1, LANES), np.float32)
    inv_lane[0, :ROPE_LANES] = np.tile(inv, ROPE_LANES // ROT_HALF)
    pos = positions.astype(jnp.float32)[..., None]
    ts = 1024
    tab = jax.ShapeDtypeStruct((bn, s_len, LANES), jnp.float32)
    spec = pl.BlockSpec((1, ts, LANES), lambda b, i: (b, i, 0))
    return pl.pallas_call(
        _rope_table_kernel,
        out_shape=(tab, tab),
        grid=(bn, s_len // ts),
        in_specs=[pl.BlockSpec((1, ts, 1), lambda b, i: (b, i, 0)),
                  pl.BlockSpec((1, LANES), lambda b, i: (0, 0))],
        out_specs=(spec, spec),
        name="rope_tables",
    )(pos, jnp.asarray(inv_lane))


def _ada_kernel(c_ref, w_ref, b_ref, o_ref):
    c = c_ref[...]
    c_act = c * _sigmoid(c)
    o_ref[0] = jnp.dot(c_act, w_ref[0], precision=lax.Precision.HIGHEST,
                       preferred_element_type=jnp.float32) + b_ref[0]


def _ada_all_layers(c, w_ada, b_ada):
    depth = w_ada.shape[0]
    bn = c.shape[0]
    rows = 8
    c_pad = jnp.pad(c, ((0, rows - bn), (0, 0)))
    tn = 1024
    out = pl.pallas_call(
        _ada_kernel,
        out_shape=jax.ShapeDtypeStruct((depth, rows, 3 * D_MODEL), jnp.float32),
        grid=(depth, 3 * D_MODEL // tn),
        in_specs=[pl.BlockSpec((rows, D_MODEL), lambda l, j: (0, 0)),
                  pl.BlockSpec((1, D_MODEL, tn), lambda l, j: (l, 0, j)),
                  pl.BlockSpec((1, 1, tn), lambda l, j: (l, 0, j))],
        out_specs=pl.BlockSpec((1, rows, tn), lambda l, j: (l, 0, j)),
        name="adaln",
    )(c_pad, w_ada, b_ada.reshape(depth, 1, 3 * D_MODEL))
    return out[:, :bn]


def _rope_store(acc, c_ref, s_ref, out_ref, ncols):
    cc = c_ref[0]
    ss = s_ref[0]
    for blk in range(ncols // HEAD_BLOCK):
        lo0 = blk * HEAD_BLOCK
        hi0 = lo0 + LANES
        lo = acc[:, lo0:hi0]
        hi = acc[:, hi0:hi0 + LANES]
        out_ref[0, :, lo0:hi0] = (lo * cc - hi * ss).astype(out_ref.dtype)
        out_ref[0, :, hi0:hi0 + LANES] = (hi * cc + lo * ss).astype(out_ref.dtype)


def _in_proj_kernel(x_ref, shift_ref, scale_ref, gain_ref, c_ref, s_ref, w_ref, wvt_ref,
                    proj_ref, vt_ref, h_ref):
    j = pl.program_id(2)

    @pl.when(j == 0)
    def _():
        xf = x_ref[0]
        ms = jnp.mean(xf * xf, axis=-1, keepdims=True)
        y = xf * lax.rsqrt(ms + RMS_EPS) * gain_ref[...]
        h = (y * (1.0 + scale_ref[0]) + shift_ref[0]).astype(jnp.bfloat16)
        h_ref[...] = h
        vt = lax.dot_general(wvt_ref[...], h, _NT, preferred_element_type=jnp.float32)
        vt_ref[0] = vt.astype(vt_ref.dtype)

    acc = jnp.dot(h_ref[...], w_ref[...], preferred_element_type=jnp.float32)
    full_rope_tiles = ROPE_COLS // PROJ_TN
    part_cols = ROPE_COLS - full_rope_tiles * PROJ_TN

    @pl.when(j < full_rope_tiles)
    def _():
        _rope_store(acc, c_ref, s_ref, proj_ref, PROJ_TN)

    @pl.when(j == full_rope_tiles)
    def _():
        _rope_store(acc, c_ref, s_ref, proj_ref, part_cols)
        proj_ref[0, :, part_cols:] = acc[:, part_cols:].astype(proj_ref.dtype)

    @pl.when(j > full_rope_tiles)
    def _():
        proj_ref[0] = acc.astype(proj_ref.dtype)


def _in_proj(x, shift, scale, gain, tabs, w_main, w_vt):
    bn, s_len, _ = x.shape
    tm, tn = PROJ_TM, PROJ_TN
    tab_spec = pl.BlockSpec((1, tm, LANES), lambda b, i, j: (b, i, 0))
    mod_spec = pl.BlockSpec((1, 1, D_MODEL), lambda b, i, j: (b, 0, 0))
    return pl.pallas_call(
        _in_proj_kernel,
        out_shape=(jax.ShapeDtypeStruct((bn, s_len, MAIN_COLS), jnp.bfloat16),
                   jax.ShapeDtypeStruct((bn, A_WIDTH, s_len), jnp.bfloat16)),
        grid=(bn, s_len // tm, MAIN_COLS // tn),
        in_specs=[pl.BlockSpec((1, tm, D_MODEL), lambda b, i, j: (b, i, 0)),
                  mod_spec, mod_spec,
                  pl.BlockSpec((1, D_MODEL), lambda b, i, j: (0, 0)),
                  tab_spec, tab_spec,
                  pl.BlockSpec((D_MODEL, tn), lambda b, i, j: (0, j)),
                  pl.BlockSpec((A_WIDTH, D_MODEL), lambda b, i, j: (0, 0))],
        out_specs=(pl.BlockSpec((1, tm, tn), lambda b, i, j: (b, i, j)),
                   pl.BlockSpec((1, A_WIDTH, tm), lambda b, i, j: (b, 0, i))),
        scratch_shapes=[pltpu.VMEM((tm, D_MODEL), jnp.bfloat16)],
        compiler_params=pltpu.CompilerParams(
            dimension_semantics=("parallel", "parallel", "arbitrary"),
            vmem_limit_bytes=VMEM_LIMIT),
        name="in_proj",
    )(x, shift, scale, gain, *tabs, w_main, w_vt)


def _diff_attn_kernel(q_ref, k_ref, vt_ref, dl_ref, gain_ref, o_ref,
                      qm_ref, vta_ref, sa_ref, sb_ref, m_ref, acc_ref, *, lambda_init, s_len):
    h = pl.program_id(1)
    tq = q_ref.shape[1]
    n_chunks = s_len // A_KC

    @pl.when(pl.program_id(2) == 0)
    def _():
        vta_ref[:A_VDIM, :] = vt_ref[0]
        row = lax.broadcasted_iota(jnp.int32, (A_VROWS - A_VDIM, s_len), 0)
        vta_ref[A_VDIM:, :] = jnp.where(row == 0, 1.0, 0.0).astype(vta_ref.dtype)

    q = q_ref[0]
    pair = _rope_layout_head(lax.broadcasted_iota(jnp.int32, q.shape, 1))
    base = 2 * (h % 2)
    for c in range(2):
        qsel = jnp.where(pair == base + c, q, jnp.zeros_like(q)).astype(jnp.float32)
        qm_ref[c] = (qsel * (HEAD_DIM ** -0.5 * LOG2_E)).astype(qm_ref.dtype)
        m_ref[c] = jnp.full((1, tq), MASK_VALUE, jnp.float32)
        acc_ref[c] = jnp.zeros((A_VROWS, tq), jnp.float32)

    def scores(i, s_ref):
        kc = k_ref[0, pl.ds(pl.multiple_of(i * A_KC, A_KC), A_KC), :]
        for c in range(2):
            s_ref[c] = lax.dot_general(kc, qm_ref[c], _NT,
                                       preferred_element_type=jnp.float32)

    def update(i, s_ref):
        vt_c = vta_ref[:, pl.ds(pl.multiple_of(i * A_KC, A_KC), A_KC)]
        for c in range(2):
            st = s_ref[c]
            m_old = m_ref[c]
            m_new = jnp.maximum(m_old, jnp.max(st, axis=0, keepdims=True))
            alpha = jnp.exp2(m_old - m_new)
            e = jnp.exp2(st - m_new).astype(jnp.bfloat16)
            pv = jnp.dot(vt_c, e, preferred_element_type=jnp.float32)
            acc_ref[c] = alpha * acc_ref[c] + pv
            m_ref[c] = m_new

    scores(0, sa_ref)

    def body(j, carry):
        scores(2 * j + 1, sb_ref)
        update(2 * j, sa_ref)
        scores(2 * j + 2, sa_ref)
        update(2 * j + 1, sb_ref)
        return carry

    lax.fori_loop(0, n_chunks // 2 - 1, body, 0)
    scores(n_chunks - 1, sb_ref)
    update(n_chunks - 2, sa_ref)
    update(n_chunks - 1, sb_ref)

    dl = dl_ref[...]
    lam = (jnp.exp(jnp.sum(dl[0:1] * dl[1:2], axis=1, keepdims=True))
           - jnp.exp(jnp.sum(dl[2:3] * dl[3:4], axis=1, keepdims=True)) + lambda_init)
    maps = [acc_ref[c, :A_VDIM, :] * (1.0 / acc_ref[c, A_VDIM:A_VDIM + 1, :]) for c in range(2)]
    ot = maps[0] - lam * maps[1]
    ms = jnp.mean(ot * ot, axis=0, keepdims=True)
    y = ot * lax.rsqrt(ms + RMS_EPS) * gain_ref[...] * (1.0 - lambda_init)
    o_ref[0] = y.T.astype(o_ref.dtype)


def _diff_attention(proj, vt, diff_lambda, subln_gain, lambda_init):
    bn, s_len, _ = proj.shape
    tq = A_TQ
    qblk = COL_AQ // HEAD_BLOCK
    kblk = COL_AK // HEAD_BLOCK
    assert s_len % (2 * A_KC) == 0
    return pl.pallas_call(
        functools.partial(_diff_attn_kernel, lambda_init=lambda_init, s_len=s_len),
        out_shape=jax.ShapeDtypeStruct((bn, s_len, A_WIDTH), jnp.bfloat16),
        grid=(bn, A_HEADS, s_len // tq),
        in_specs=[pl.BlockSpec((1, tq, HEAD_BLOCK), lambda b, h, i: (b, i, qblk + h // 2)),
                  pl.BlockSpec((1, s_len, HEAD_BLOCK), lambda b, h, i: (b, 0, kblk + h // 2)),
                  pl.BlockSpec((1, A_VDIM, s_len), lambda b, h, i: (b, h, 0)),
                  pl.BlockSpec((4, HEAD_DIM), lambda b, h, i: (0, 0)),
                  pl.BlockSpec((A_VDIM, 1), lambda b, h, i: (0, 0))],
        out_specs=pl.BlockSpec((1, tq, A_VDIM), lambda b, h, i: (b, i, h)),
        scratch_shapes=[pltpu.VMEM((2, tq, HEAD_BLOCK), jnp.bfloat16),
                        pltpu.VMEM((A_VROWS, s_len), jnp.bfloat16),
                        pltpu.VMEM((2, A_KC, tq), jnp.float32),
                        pltpu.VMEM((2, A_KC, tq), jnp.float32),
                        pltpu.VMEM((2, 1, tq), jnp.float32),
                        pltpu.VMEM((2, A_VROWS, tq), jnp.float32)],
        compiler_params=pltpu.CompilerParams(
            dimension_semantics=("parallel", "parallel", "arbitrary"),
            vmem_limit_bytes=VMEM_LIMIT),
        name="diff_attn",
    )(proj, proj, vt, diff_lambda, subln_gain.reshape(A_VDIM, 1))


def _head_lane_masks(shape, rotary_layout=False):
    col = lax.broadcasted_iota(jnp.int32, shape, 1)
    head = _rope_layout_head(col) if rotary_layout else col // HEAD_DIM
    return [head == h for h in range(HEAD_BLOCK // HEAD_DIM)]


def _stack_heads(q, qmask):
    zero = jnp.zeros_like(q)
    return jnp.concatenate([jnp.where(mk, q, zero) for mk in qmask], axis=0) * (HEAD_DIM ** -0.5)


def _unstack_heads(x, qmask, tq):
    out = x[:tq]
    for h in range(1, len(qmask)):
        out = jnp.where(qmask[h], x[h * tq:(h + 1) * tq], out)
    return out


def _band_attn_kernel(q_ref, k_ref, v_ref, o_ref, lse_ref, *scratch, dilation, length, tq, win,
                      unroll):
    nblk = length // tq
    qmask = _head_lane_masks((tq, HEAD_BLOCK), rotary_layout=True)
    vmask = _head_lane_masks((tq, HEAD_BLOCK))
    band_ref = scratch[0]

    row = lax.broadcasted_iota(jnp.int32, band_ref.shape[1:], 0) % tq
    col = lax.broadcasted_iota(jnp.int32, band_ref.shape[1:], 1)
    for case in range(B_CASES):
        valid = jnp.abs(col - case * B_REACH - row) <= B_REACH
        band_ref[case] = jnp.where(valid, 0.0, MASK_VALUE)

    if dilation > 1:
        stage_ref, tmp_ref, qd_ref, kd_ref, vd_ref = scratch[1:]
        s_len = length * dilation
        for src, dst in ((q_ref, qd_ref), (k_ref, kd_ref), (v_ref, vd_ref)):
            for slab in range(HEAD_BLOCK // LANES):
                lanes = slice(slab * LANES, (slab + 1) * LANES)
                stage_ref[...] = src[0, :, lanes].astype(jnp.float32)
                for m4 in range(4):
                    first = stage_ref[pl.ds(m4, s_len // 4, stride=4), :]
                    if dilation == 4:
                        dst[m4, :, lanes] = first.astype(dst.dtype)
                    else:
                        tmp_ref[...] = first
                        for mm in range(4):
                            dst[m4 + 4 * mm, :, lanes] = tmp_ref[
                                pl.ds(mm, length, stride=4), :].astype(dst.dtype)

    def one_block(t):
        m = t // nblk
        q0 = pl.multiple_of((t % nblk) * tq, tq)
        ks = pl.multiple_of(jnp.clip(q0 - B_REACH, 0, length - win), B_REACH)
        if dilation > 1:
            q_src, k_src, v_src = qd_ref.at[m], kd_ref.at[m], vd_ref.at[m]
            rows = pl.ds(q0 * dilation + m, tq, stride=dilation)
        else:
            q_src, k_src, v_src = q_ref.at[0], k_ref.at[0], v_ref.at[0]
            rows = pl.ds(q0, tq)
        q = q_src[pl.ds(q0, tq), :]
        kw = k_src[pl.ds(ks, win), :]
        vw = v_src[pl.ds(ks, win), :]
        qs = _stack_heads(q, qmask)
        s = lax.dot_general(qs, kw, _NT, preferred_element_type=jnp.float32)
        s = s + band_ref[(q0 - ks) // B_REACH]
        mx = jnp.max(s, axis=-1, keepdims=True)
        e = jnp.exp(s - mx)
        den = jnp.sum(e, axis=-1, keepdims=True)
        o = jnp.dot(e.astype(jnp.bfloat16), vw, preferred_element_type=jnp.float32)
        o = o * (1.0 / den)
        lse = jnp.broadcast_to(mx + jnp.log(den), o.shape)
        o_acc = _unstack_heads(o, vmask, tq)
        l_acc = _unstack_heads(lse, vmask, tq)
        for slab in range(HEAD_BLOCK // LANES):
            lanes = slice(slab * LANES, (slab + 1) * LANES)
            o_ref[0, slab, rows, :] = o_acc[:, lanes]
            lse_ref[0, slab, rows, :] = l_acc[:, lanes]

    def body(i, carry):
        for u in range(unroll):
            one_block(i * unroll + u)
        return carry

    lax.fori_loop(0, dilation * nblk // unroll, body, 0)


def _band_attention(proj, group, dilation):
    bn, s_len, _ = proj.shape
    length = s_len // dilation
    tq = min(B_TQ_MAX, length)
    win = min(tq + 2 * B_REACH, length)
    unroll = min(B_UNROLL_MAX, s_len // tq)
    assert length % tq == 0 and (s_len // tq) % unroll == 0 and dilation in (1, 4, 16)
    qblk = COL_BQ // HEAD_BLOCK + group
    kblk = COL_BK // HEAD_BLOCK + group
    vblk = COL_BV // HEAD_BLOCK + group

    def in_spec(blk):
        return pl.BlockSpec((1, s_len, HEAD_BLOCK), lambda b: (b, 0, blk))

    slabs = HEAD_BLOCK // LANES
    out_sds = jax.ShapeDtypeStruct((bn, slabs, s_len, LANES), jnp.float32)
    out_spec = pl.BlockSpec((1, slabs, s_len, LANES), lambda b: (b, 0, 0, 0))
    scratch = [pltpu.VMEM((B_CASES, B_HEADS * tq, win), jnp.float32)]
    if dilation > 1:
        regrouped = pltpu.VMEM((dilation, length, HEAD_BLOCK), jnp.bfloat16)
        scratch += [pltpu.VMEM((s_len, LANES), jnp.float32), pltpu.VMEM((s_len // 4, LANES), jnp.float32),
                    regrouped, regrouped, regrouped]
    return pl.pallas_call(
        functools.partial(_band_attn_kernel, dilation=dilation, length=length, tq=tq, win=win,
                          unroll=unroll),
        out_shape=(out_sds, out_sds),
        grid=(bn,),
        in_specs=[in_spec(qblk), in_spec(kblk), in_spec(vblk)],
        out_specs=(out_spec, out_spec),
        scratch_shapes=scratch,
        compiler_params=pltpu.CompilerParams(
            dimension_semantics=("parallel",),
            vmem_limit_bytes=VMEM_LIMIT),
        name=f"band_attn_d{dilation}",
    )(proj, proj, proj)


def _na_table_kernel(rpb_ref, tab_ref, *, rows):
    h = pl.program_id(0)
    n_dr = 2 * NA_KH - 1
    n_dc = 2 * NA_KW - 1
    shape = (GRID_W, LANES)
    qc = lax.broadcasted_iota(jnp.int32, shape, 0)
    lane = lax.broadcasted_iota(jnp.int32, shape, 1)
    kc = lane % GRID_W
    dc_i = jnp.clip(kc - qc + (NA_KW - 1), 0, n_dc - 1)
    cs = jnp.clip(qc - NA_KW // 2, 0, GRID_W - NA_KW)
    col_valid = (kc >= cs) & (kc < cs + NA_KW)
    masked = jnp.full(shape, MASK_VALUE, jnp.float32)

    row_bias = []
    for dr in range(n_dr):
        base = (h * n_dr + dr) * n_dc
        t = masked
        for dc in range(n_dc):
            t = jnp.where(dc_i == dc, rpb_ref[base + dc], t)
        row_bias.append(jnp.where(col_valid, t, MASK_VALUE))

    pair_rep = (0, 1, 2, rows // 2 - 2, rows // 2 - 1)
    for case in range(C_CASES):
        ip = pair_rep[case]
        ws = min(max(2 * ip - NA_KH // 2, 0), rows - C_KROWS)
        for qr_l in range(2):
            r = 2 * ip + qr_l
            rs = min(max(r - NA_KH // 2, 0), rows - NA_KH)
            for tile in range(C_WIN // LANES):
                halves = []
                for kr in (ws + 2 * tile, ws + 2 * tile + 1):
                    halves.append(row_bias[kr - r + NA_KH - 1] if rs <= kr < rs + NA_KH else masked)
                tab_ref[case, 0, qr_l * GRID_W:(qr_l + 1) * GRID_W, tile * LANES:(tile + 1) * LANES] = (
                    jnp.where(lane < GRID_W, halves[0], halves[1]))


def _na_tables(rpb, rows):
    flat = rpb.reshape(-1)
    return pl.pallas_call(
        functools.partial(_na_table_kernel, rows=rows),
        out_shape=jax.ShapeDtypeStruct((C_CASES, C_HEADS, C_PAIR, C_WIN), jnp.float32),
        grid=(C_HEADS,),
        in_specs=[pl.BlockSpec(memory_space=pltpu.SMEM)],
        out_specs=pl.BlockSpec((C_CASES, 1, C_PAIR, C_WIN), lambda h: (0, h, 0, 0)),
        name="na_tables",
    )(flat)


def _na_kernel(q_ref, k_ref, v_ref, tab_ref, o_ref, *, rows):
    rb = pl.program_id(1)
    qmask = _head_lane_masks((C_PAIR, HEAD_BLOCK))
    pairs_per_step = C_RB // 2
    for t in range(pairs_per_step):
        ip = rb * pairs_per_step + t
        ws = jnp.clip(2 * ip - NA_KH // 2, 0, rows - C_KROWS)
        case = ip - ws // 2
        k0 = pl.multiple_of(ws * GRID_W, C_PAIR)
        q = q_ref[0, t * C_PAIR:(t + 1) * C_PAIR, :]
        kw = k_ref[0, pl.ds(k0, C_WIN), :]
        vw = v_ref[0, pl.ds(k0, C_WIN), :]
        qs = _stack_heads(q, qmask)
        s = lax.dot_general(qs, kw, _NT, preferred_element_type=jnp.float32)
        s = s + tab_ref[case]
        mx = jnp.max(s, axis=-1, keepdims=True)
        e = jnp.exp(s - mx)
        den = jnp.sum(e, axis=-1, keepdims=True)
        o = jnp.dot(e.astype(jnp.bfloat16), vw, preferred_element_type=jnp.float32)
        o_acc = _unstack_heads(o * (1.0 / den), qmask, C_PAIR)
        o_ref[0, t * C_PAIR:(t + 1) * C_PAIR, :] = o_acc.astype(o_ref.dtype)


def _neighborhood_attention(proj, tables):
    bn, s_len, _ = proj.shape
    rows = s_len // GRID_W
    tq = C_RB * GRID_W
    qblk = COL_CQ // HEAD_BLOCK
    kblk = COL_CK // HEAD_BLOCK
    vblk = COL_CV // HEAD_BLOCK
    return pl.pallas_call(
        functools.partial(_na_kernel, rows=rows),
        out_shape=jax.ShapeDtypeStruct((bn, s_len, C_WIDTH), jnp.bfloat16),
        grid=(bn, rows // C_RB),
        in_specs=[pl.BlockSpec((1, tq, HEAD_BLOCK), lambda b, i: (b, i, qblk)),
                  pl.BlockSpec((1, s_len, HEAD_BLOCK), lambda b, i: (b, 0, kblk)),
                  pl.BlockSpec((1, s_len, HEAD_BLOCK), lambda b, i: (b, 0, vblk)),
                  pl.BlockSpec((C_CASES, C_HEADS * C_PAIR, C_WIN), lambda b, i: (0, 0, 0))],
        out_specs=pl.BlockSpec((1, tq, C_WIDTH), lambda b, i: (b, i, 0)),
        compiler_params=pltpu.CompilerParams(
            dimension_semantics=("parallel", "arbitrary"),
            vmem_limit_bytes=VMEM_LIMIT),
        name="na_attn",
    )(proj, proj, proj, tables.reshape(C_CASES, C_HEADS * C_PAIR, C_WIN))


def _merge_kernel(x_ref, gate_ref, ya_ref, ob0_ref, ob1_ref, ob2_ref, lb0_ref, lb1_ref, lb2_ref,
                  yc_ref, z_ref, g0_ref, g1_ref, g2_ref, wb_ref, wo_ref, fg_ref, o_ref, *, final_norm):
    yb_slabs = []
    for slab in range(HEAD_BLOCK // LANES):
        l0, l1, l2 = lb0_ref[0, slab], lb1_ref[0, slab], lb2_ref[0, slab]
        lm = jnp.maximum(jnp.maximum(l0, l1), l2)
        e0, e1, e2 = jnp.exp(l0 - lm), jnp.exp(l1 - lm), jnp.exp(l2 - lm)
        yb_slabs.append((e0 * ob0_ref[0, slab] + e1 * ob1_ref[0, slab] + e2 * ob2_ref[0, slab])
                        * (1.0 / (e0 + e1 + e2)))
    yb = jnp.concatenate(yb_slabs, axis=-1)

    z = z_ref[0].astype(jnp.float32)
    sz = z * _sigmoid(z)
    ya = (ya_ref[0].astype(jnp.float32) * sz[:, :A_WIDTH]).astype(jnp.bfloat16)
    ybz = (yb * sz[:, A_WIDTH:A_WIDTH + B_WIDTH]).astype(jnp.bfloat16)
    ycz = (yc_ref[0].astype(jnp.float32) * sz[:, A_WIDTH + B_WIDTH:]).astype(jnp.bfloat16)

    pa = jnp.dot(ya, wb_ref[:A_WIDTH, :], preferred_element_type=jnp.float32)
    pb = jnp.dot(ybz, wb_ref[A_WIDTH:A_WIDTH + B_WIDTH, :], preferred_element_type=jnp.float32)
    pc = jnp.dot(ycz, wb_ref[A_WIDTH + B_WIDTH:, :], preferred_element_type=jnp.float32)
    merged = _sigmoid(g0_ref[0].astype(jnp.float32)) * pa
    merged = merged + _sigmoid(g1_ref[0].astype(jnp.float32)) * pb
    merged = merged + _sigmoid(g2_ref[0].astype(jnp.float32)) * pc
    out = jnp.dot(merged.astype(jnp.bfloat16), wo_ref[...], preferred_element_type=jnp.float32)
    xn = x_ref[0] + gate_ref[0] * out
    if final_norm:
        ms = jnp.mean(xn * xn, axis=-1, keepdims=True)
        xn = xn * lax.rsqrt(ms + RMS_EPS) * fg_ref[...]
    o_ref[0] = xn


def _merge(x, gate, ya, ob, lb, yc, proj, w_branch, w_out, final_gain, final_norm):
    bn, s_len, _ = x.shape
    tm = MERGE_TM
    zblk = COL_Z // D_MODEL
    gblk = COL_G // D_MODEL

    def tok(width, blk=0):
        return pl.BlockSpec((1, tm, width), lambda b, i: (b, i, blk))

    slab = pl.BlockSpec((1, HEAD_BLOCK // LANES, tm, LANES), lambda b, i: (b, 0, i, 0))

    return pl.pallas_call(
        functools.partial(_merge_kernel, final_norm=final_norm),
        out_shape=jax.ShapeDtypeStruct((bn, s_len, D_MODEL), jnp.float32),
        grid=(bn, s_len // tm),
        in_specs=[tok(D_MODEL),
                  pl.BlockSpec((1, 1, D_MODEL), lambda b, i: (b, 0, 0)),
                  tok(A_WIDTH),
                  slab, slab, slab, slab, slab, slab,
                  tok(C_WIDTH),
                  tok(D_MODEL, zblk), tok(D_MODEL, gblk), tok(D_MODEL, gblk + 1), tok(D_MODEL, gblk + 2),
                  pl.BlockSpec((BR_WIDTH, D_MODEL), lambda b, i: (0, 0)),
                  pl.BlockSpec((D_MODEL, D_MODEL), lambda b, i: (0, 0)),
                  pl.BlockSpec((1, D_MODEL), lambda b, i: (0, 0))],
        out_specs=tok(D_MODEL),
        compiler_params=pltpu.CompilerParams(
            dimension_semantics=("parallel", "parallel"),
            vmem_limit_bytes=VMEM_LIMIT),
        name="merge",
    )(x, gate, ya, ob[0], ob[1], ob[2], lb[0], lb[1], lb[2], yc, proj, proj, proj, proj,
      w_branch, w_out, final_gain)


def _split_w_in(w_in_l):
    sizes = (A_QK_COLS, A_QK_COLS, A_WIDTH, B_COLS, B_COLS, B_COLS, C_WIDTH, C_WIDTH, C_WIDTH,
             BR_WIDTH, N_BRANCHES * D_MODEL)
    offs = np.concatenate([[0], np.cumsum(sizes)])
    parts = [w_in_l[:, offs[n]:offs[n + 1]] for n in range(len(sizes))]
    a_v = parts.pop(2)
    perm = _rope_layout_perm()
    for n in range(4):
        cols = np.concatenate([blk * HEAD_BLOCK + perm for blk in range(parts[n].shape[1] // HEAD_BLOCK)])
        parts[n] = parts[n][:, cols]
    w_main = jnp.concatenate(parts, axis=1).astype(jnp.bfloat16)
    w_vt = a_v.T.astype(jnp.bfloat16)
    return w_main, w_vt


def kernel(x, c, positions, norm_gain, w_ada, b_ada, w_in, diff_lambda, diff_subln_gain, na_rpb,
           w_branch, w_out, final_gain):
    depth = w_in.shape[0]
    bn, s_len, _ = x.shape
    rows = s_len // GRID_W
    tabs = _rope_tables(positions)
    ada = _ada_all_layers(c, w_ada, b_ada)
    fg = final_gain.reshape(1, D_MODEL)
    for layer in range(depth):
        shift = ada[layer, :, None, :D_MODEL]
        scale = ada[layer, :, None, D_MODEL:2 * D_MODEL]
        gate = ada[layer, :, None, 2 * D_MODEL:]
        w_main, w_vt = _split_w_in(w_in[layer])
        proj, vt = _in_proj(x, shift, scale, norm_gain[layer].reshape(1, D_MODEL), tabs, w_main, w_vt)
        lambda_init = 0.8 - 0.6 * math.exp(-0.3 * layer)
        ya = _diff_attention(proj, vt, diff_lambda[layer], diff_subln_gain[layer], lambda_init)
        ob, lb = [], []
        for g, (window, dilation) in enumerate(B_PATTERNS):
            assert (window // 2) // dilation == B_REACH
            o_g, l_g = _band_attention(proj, g, dilation)
            ob.append(o_g)
            lb.append(l_g)
        yc = _neighborhood_attention(proj, _na_tables(na_rpb[layer], rows))
        x = _merge(x, gate, ya, ob, lb, yc, proj, w_branch[layer].astype(jnp.bfloat16),
                   w_out[layer].astype(jnp.bfloat16), fg, final_norm=(layer == depth - 1))
    return x
```

```python
import functools
import math

import numpy as np
import jax
import jax.numpy as jnp
from jax import lax
from jax.experimental import pallas as pl
from jax.experimental.pallas import tpu as pltpu

D_MODEL = 1024
HEAD_DIM = 64
ROT_DIM = HEAD_DIM // 4
ROT_HALF = ROT_DIM // 2
ROPE_THETA = 500000.0
RMS_EPS = 1e-6
MASK_VALUE = -1e30
LOG2_E = math.log2(math.e)

A_HEADS = 4
A_VDIM = 2 * HEAD_DIM
A_WIDTH = A_HEADS * A_VDIM
A_QK_COLS = A_HEADS * 2 * HEAD_DIM

B_PATTERNS = ((128, 1), (512, 4), (2048, 16))
B_GROUPS = 3
B_HEADS = 4
B_WIDTH = B_HEADS * HEAD_DIM
B_COLS = B_GROUPS * B_WIDTH

GRID_W = 64
C_HEADS = 4
NA_KH = 8
NA_KW = 16
C_WIDTH = C_HEADS * HEAD_DIM

N_BRANCHES = 3
BR_WIDTH = A_WIDTH + B_WIDTH + C_WIDTH

LANES = 128
HEAD_BLOCK = 4 * HEAD_DIM
ROPE_LANES = (HEAD_BLOCK // HEAD_DIM) * ROT_HALF
VMEM_LIMIT = 56 * 1024 * 1024

COL_AQ = 0
COL_AK = COL_AQ + A_QK_COLS
COL_BQ = COL_AK + A_QK_COLS
COL_BK = COL_BQ + B_COLS
COL_BV = COL_BK + B_COLS
COL_CQ = COL_BV + B_COLS
COL_CK = COL_CQ + C_WIDTH
COL_CV = COL_CK + C_WIDTH
COL_Z = COL_CV + C_WIDTH
COL_G = COL_Z + BR_WIDTH
MAIN_COLS = COL_G + N_BRANCHES * D_MODEL
ROPE_COLS = COL_BV

PROJ_TM = 2048
PROJ_TN = 1024
A_TQ = 1024
A_KC = 512
A_VROWS = A_VDIM + 16
B_TQ_MAX = 128
B_UNROLL_MAX = 4
B_REACH = 64
B_CASES = 3
C_PAIR = 2 * GRID_W
C_KROWS = 10
C_WIN = C_KROWS * GRID_W
C_CASES = 5
C_RB = 8
MERGE_TM = 512

_NT = (((1,), (1,)), ((), ()))


def _sigmoid(x):
    return 1.0 / (1.0 + jnp.exp2(x * (-LOG2_E)))


def _to_rope_layout(w):
    lead = w.shape[:-1]
    nb = w.shape[-1] // HEAD_BLOCK
    heads = HEAD_BLOCK // HEAD_DIM
    w4 = w.reshape(*lead, nb, heads, HEAD_DIM)
    first = w4[..., :ROT_HALF].reshape(*lead, nb, ROPE_LANES)
    second = w4[..., ROT_HALF:ROT_DIM].reshape(*lead, nb, ROPE_LANES)
    rest = w4[..., ROT_DIM:]
    rest_lo = rest[..., :heads // 2, :].reshape(*lead, nb, LANES - ROPE_LANES)
    rest_hi = rest[..., heads // 2:, :].reshape(*lead, nb, LANES - ROPE_LANES)
    out = jnp.concatenate([first, rest_lo, second, rest_hi], axis=-1)
    return out.reshape(*lead, nb * HEAD_BLOCK)


def _rope_layout_head(col):
    l = col % LANES
    c = col // LANES
    rest = HEAD_DIM - ROT_DIM
    return jnp.where(l < ROPE_LANES, l // ROT_HALF, 2 * c + jnp.where(l >= ROPE_LANES + rest, 1, 0))


def _rope_table_kernel(pos_ref, inv_ref, c_ref, s_ref):
    ang = pos_ref[0] * inv_ref[...]
    lane = lax.broadcasted_iota(jnp.int32, ang.shape, 1)
    c_ref[0] = jnp.where(lane < ROPE_LANES, jnp.cos(ang), 1.0)
    s_ref[0] = jnp.where(lane < ROPE_LANES, jnp.sin(ang), 0.0)


def _rope_tables(positions):
    bn, s_len = positions.shape
    inv = np.float32(ROPE_THETA) ** (-np.arange(0, ROT_DIM, 2, dtype=np.float32) / np.float32(ROT_DIM))
    inv_lane = np.zeros((1, LANES), np.float32)
    inv_lane[0, :ROPE_LANES] = np.tile(inv, ROPE_LANES // ROT_HALF)
    pos = positions.astype(jnp.float32)[..., None]
    ts = 1024
    tab = jax.ShapeDtypeStruct((bn, s_len, LANES), jnp.float32)
    spec = pl.BlockSpec((1, ts, LANES), lambda b, i: (b, i, 0))
    return pl.pallas_call(
        _rope_table_kernel,
        out_shape=(tab, tab),
        grid=(bn, s_len // ts),
        in_specs=[pl.BlockSpec((1, ts, 1), lambda b, i: (b, i, 0)),
                  pl.BlockSpec((1, LANES), lambda b, i: (0, 0))],
        out_specs=(spec, spec),
        name="rope_tables",
    )(pos, jnp.asarray(inv_lane))


def _ada_kernel(c_ref, w_ref, b_ref, o_ref):
    c = c_ref[...]
    c_act = c * _sigmoid(c)
    o_ref[0] = jnp.dot(c_act, w_ref[0], precision=lax.Precision.HIGHEST,
                       preferred_element_type=jnp.float32) + b_ref[0]


def _ada_all_layers(c, w_ada, b_ada):
    depth = w_ada.shape[0]
    bn = c.shape[0]
    rows = 8
    c_pad = jnp.pad(c, ((0, rows - bn), (0, 0)))
    tn = 1024
    out = pl.pallas_call(
        _ada_kernel,
        out_shape=jax.ShapeDtypeStruct((depth, rows, 3 * D_MODEL), jnp.float32),
        grid=(depth, 3 * D_MODEL // tn),
        in_specs=[pl.BlockSpec((rows, D_MODEL), lambda l, j: (0, 0)),
                  pl.BlockSpec((1, D_MODEL, tn), lambda l, j: (l, 0, j)),
                  pl.BlockSpec((1, 1, tn), lambda l, j: (l, 0, j))],
        out_specs=pl.BlockSpec((1, rows, tn), lambda l, j: (l, 0, j)),
        name="adaln",
    )(c_pad, w_ada, b_ada.reshape(depth, 1, 3 * D_MODEL))
    return out[:, :bn]


def _in_proj_kernel(x_ref, shift_ref, scale_ref, gain_ref, c_ref, s_ref, w_ref, wvt_ref,
                    proj_ref, vt_ref, h_ref):
    j = pl.program_id(2)

    @pl.when(j == 0)
    def _():
        xf = x_ref[0]
        ms = jnp.mean(xf * xf, axis=-1, keepdims=True)
        y = xf * lax.rsqrt(ms + RMS_EPS) * gain_ref[...]
        h = (y * (1.0 + scale_ref[0]) + shift_ref[0]).astype(jnp.bfloat16)
        h_ref[...] = h
        vt = lax.dot_general(wvt_ref[...], h, _NT, preferred_element_type=jnp.float32)
        vt_ref[0] = vt.astype(vt_ref.dtype)

    acc = jnp.dot(h_ref[...], w_ref[...], preferred_element_type=jnp.float32)
    blocks_per_tile = PROJ_TN // HEAD_BLOCK
    cc = c_ref[0]
    ss = s_ref[0]
    for blk in range(blocks_per_tile):
        rotary = (j * blocks_per_tile + blk) < (ROPE_COLS // HEAD_BLOCK)
        cb = jnp.where(rotary, cc, 1.0)
        sb = jnp.where(rotary, ss, 0.0)
        lo0 = blk * HEAD_BLOCK
        hi0 = lo0 + LANES
        lo = acc[:, lo0:hi0]
        hi = acc[:, hi0:hi0 + LANES]
        proj_ref[0, :, lo0:hi0] = (lo * cb - hi * sb).astype(proj_ref.dtype)
        proj_ref[0, :, hi0:hi0 + LANES] = (hi * cb + lo * sb).astype(proj_ref.dtype)


def _in_proj(x, shift, scale, gain, tabs, w_main, w_vt, layer):
    bn, s_len, _ = x.shape
    tm, tn = PROJ_TM, PROJ_TN
    tab_spec = pl.BlockSpec((1, tm, LANES), lambda b, i, j: (b, i, 0))
    mod_spec = pl.BlockSpec((1, 1, D_MODEL), lambda b, i, j: (b, 0, 0))
    return pl.pallas_call(
        _in_proj_kernel,
        out_shape=(jax.ShapeDtypeStruct((bn, s_len, MAIN_COLS), jnp.bfloat16),
                   jax.ShapeDtypeStruct((bn, A_WIDTH, s_len), jnp.bfloat16)),
        grid=(bn, s_len // tm, MAIN_COLS // tn),
        in_specs=[pl.BlockSpec((1, tm, D_MODEL), lambda b, i, j: (b, i, 0)),
                  mod_spec, mod_spec,
                  pl.BlockSpec((1, D_MODEL), lambda b, i, j: (0, 0)),
                  tab_spec, tab_spec,
                  pl.BlockSpec((None, D_MODEL, tn), lambda b, i, j: (layer, 0, j)),
                  pl.BlockSpec((None, A_WIDTH, D_MODEL), lambda b, i, j: (layer, 0, 0))],
        out_specs=(pl.BlockSpec((1, tm, tn), lambda b, i, j: (b, i, j)),
                   pl.BlockSpec((1, A_WIDTH, tm), lambda b, i, j: (b, 0, i))),
        scratch_shapes=[pltpu.VMEM((tm, D_MODEL), jnp.bfloat16)],
        compiler_params=pltpu.CompilerParams(
            dimension_semantics=("parallel", "parallel", "arbitrary"),
            vmem_limit_bytes=VMEM_LIMIT),
        name="in_proj",
    )(x, shift, scale, gain, *tabs, w_main, w_vt)


def _diff_attn_kernel(q_ref, k_ref, vt_ref, dl_ref, gain_ref, o_ref,
                      qm_ref, vta_ref, sa_ref, sb_ref, m_ref, acc_ref, *, lambda_init, s_len):
    h = pl.program_id(1)
    tq = q_ref.shape[1]
    n_chunks = s_len // A_KC

    @pl.when(pl.program_id(2) == 0)
    def _():
        vta_ref[:A_VDIM, :] = vt_ref[0]
        row = lax.broadcasted_iota(jnp.int32, (A_VROWS - A_VDIM, s_len), 0)
        vta_ref[A_VDIM:, :] = jnp.where(row == 0, 1.0, 0.0).astype(vta_ref.dtype)

    q = q_ref[0]
    pair = _rope_layout_head(lax.broadcasted_iota(jnp.int32, q.shape, 1))
    base = 2 * (h % 2)
    for c in range(2):
        qsel = jnp.where(pair == base + c, q, jnp.zeros_like(q)).astype(jnp.float32)
        qm_ref[c] = (qsel * (HEAD_DIM ** -0.5 * LOG2_E)).astype(qm_ref.dtype)
        m_ref[c] = jnp.full((1, tq), MASK_VALUE, jnp.float32)
        acc_ref[c] = jnp.zeros((A_VROWS, tq), jnp.float32)

    def scores(i, s_ref):
        kc = k_ref[0, pl.ds(pl.multiple_of(i * A_KC, A_KC), A_KC), :]
        for c in range(2):
            s_ref[c] = lax.dot_general(kc, qm_ref[c], _NT,
                                       preferred_element_type=jnp.float32)

    def update(i, s_ref):
        vt_c = vta_ref[:, pl.ds(pl.multiple_of(i * A_KC, A_KC), A_KC)]
        for c in range(2):
            st = s_ref[c]
            m_old = m_ref[c]
            m_new = jnp.maximum(m_old, jnp.max(st, axis=0, keepdims=True))
            alpha = jnp.exp2(m_old - m_new)
            e = jnp.exp2(st - m_new).astype(jnp.bfloat16)
            pv = jnp.dot(vt_c, e, preferred_element_type=jnp.float32)
            acc_ref[c] = alpha * acc_ref[c] + pv
            m_ref[c] = m_new

    scores(0, sa_ref)

    def body(j, carry):
        scores(2 * j + 1, sb_ref)
        update(2 * j, sa_ref)
        scores(2 * j + 2, sa_ref)
        update(2 * j + 1, sb_ref)
        return carry

    lax.fori_loop(0, n_chunks // 2 - 1, body, 0)
    scores(n_chunks - 1, sb_ref)
    update(n_chunks - 2, sa_ref)
    update(n_chunks - 1, sb_ref)

    dl = dl_ref[...]
    lam = (jnp.exp(jnp.sum(dl[0:1] * dl[1:2], axis=1, keepdims=True))
           - jnp.exp(jnp.sum(dl[2:3] * dl[3:4], axis=1, keepdims=True)) + lambda_init)
    maps = [acc_ref[c, :A_VDIM, :] * (1.0 / acc_ref[c, A_VDIM:A_VDIM + 1, :]) for c in range(2)]
    ot = maps[0] - lam * maps[1]
    ms = jnp.mean(ot * ot, axis=0, keepdims=True)
    y = ot * lax.rsqrt(ms + RMS_EPS) * gain_ref[...] * (1.0 - lambda_init)
    o_ref[0] = y.T.astype(o_ref.dtype)


def _diff_attention(proj, vt, diff_lambda, subln_gain, lambda_init):
    bn, s_len, _ = proj.shape
    tq = A_TQ
    qblk = COL_AQ // HEAD_BLOCK
    kblk = COL_AK // HEAD_BLOCK
    assert s_len % (2 * A_KC) == 0
    return pl.pallas_call(
        functools.partial(_diff_attn_kernel, lambda_init=lambda_init, s_len=s_len),
        out_shape=jax.ShapeDtypeStruct((bn, s_len, A_WIDTH), jnp.bfloat16),
        grid=(bn, A_HEADS, s_len // tq),
        in_specs=[pl.BlockSpec((1, tq, HEAD_BLOCK), lambda b, h, i: (b, i, qblk + h // 2)),
                  pl.BlockSpec((1, s_len, HEAD_BLOCK), lambda b, h, i: (b, 0, kblk + h // 2)),
                  pl.BlockSpec((1, A_VDIM, s_len), lambda b, h, i: (b, h, 0)),
                  pl.BlockSpec((4, HEAD_DIM), lambda b, h, i: (0, 0)),
                  pl.BlockSpec((A_VDIM, 1), lambda b, h, i: (0, 0))],
        out_specs=pl.BlockSpec((1, tq, A_VDIM), lambda b, h, i: (b, i, h)),
        scratch_shapes=[pltpu.VMEM((2, tq, HEAD_BLOCK), jnp.bfloat16),
                        pltpu.VMEM((A_VROWS, s_len), jnp.bfloat16),
                        pltpu.VMEM((2, A_KC, tq), jnp.float32),
                        pltpu.VMEM((2, A_KC, tq), jnp.float32),
                        pltpu.VMEM((2, 1, tq), jnp.float32),
                        pltpu.VMEM((2, A_VROWS, tq), jnp.float32)],
        compiler_params=pltpu.CompilerParams(
            dimension_semantics=("parallel", "parallel", "arbitrary"),
            vmem_limit_bytes=VMEM_LIMIT),
        name="diff_attn",
    )(proj, proj, vt, diff_lambda, subln_gain.reshape(A_VDIM, 1))


def _head_lane_masks(shape, rotary_layout=False):
    col = lax.broadcasted_iota(jnp.int32, shape, 1)
    head = _rope_layout_head(col) if rotary_layout else col // HEAD_DIM
    return [head == h for h in range(HEAD_BLOCK // HEAD_DIM)]


def _stack_heads(q, qmask):
    zero = jnp.zeros_like(q)
    return jnp.concatenate([jnp.where(mk, q, zero) for mk in qmask], axis=0) * (HEAD_DIM ** -0.5)


def _unstack_heads(x, qmask, tq):
    out = x[:tq]
    for h in range(1, len(qmask)):
        out = jnp.where(qmask[h], x[h * tq:(h + 1) * tq], out)
    return out


def _band_attn_kernel(q_ref, k_ref, v_ref, o_ref, lse_ref, *scratch, dilation, length, tq, win,
                      unroll):
    nblk = length // tq
    qmask = _head_lane_masks((tq, HEAD_BLOCK), rotary_layout=True)
    vmask = _head_lane_masks((tq, HEAD_BLOCK))
    band_ref = scratch[0]

    row = lax.broadcasted_iota(jnp.int32, band_ref.shape[1:], 0) % tq
    col = lax.broadcasted_iota(jnp.int32, band_ref.shape[1:], 1)
    for case in range(B_CASES):
        valid = jnp.abs(col - case * B_REACH - row) <= B_REACH
        band_ref[case] = jnp.where(valid, 0.0, MASK_VALUE)

    if dilation > 1:
        stage_ref, tmp_ref, qd_ref, kd_ref, vd_ref = scratch[1:]
        s_len = length * dilation
        for src, dst in ((q_ref, qd_ref), (k_ref, kd_ref), (v_ref, vd_ref)):
            for slab in range(HEAD_BLOCK // LANES):
                lanes = slice(slab * LANES, (slab + 1) * LANES)
                stage_ref[...] = src[0, :, lanes].astype(jnp.float32)
                for m4 in range(4):
                    first = stage_ref[pl.ds(m4, s_len // 4, stride=4), :]
                    if dilation == 4:
                        dst[m4, :, lanes] = first.astype(dst.dtype)
                    else:
                        tmp_ref[...] = first
                        for mm in range(4):
                            dst[m4 + 4 * mm, :, lanes] = tmp_ref[
                                pl.ds(mm, length, stride=4), :].astype(dst.dtype)

    def one_block(t):
        m = t // nblk
        q0 = pl.multiple_of((t % nblk) * tq, tq)
        ks = pl.multiple_of(jnp.clip(q0 - B_REACH, 0, length - win), B_REACH)
        if dilation > 1:
            q_src, k_src, v_src = qd_ref.at[m], kd_ref.at[m], vd_ref.at[m]
            rows = pl.ds(q0 * dilation + m, tq, stride=dilation)
        else:
            q_src, k_src, v_src = q_ref.at[0], k_ref.at[0], v_ref.at[0]
            rows = pl.ds(q0, tq)
        q = q_src[pl.ds(q0, tq), :]
        kw = k_src[pl.ds(ks, win), :]
        vw = v_src[pl.ds(ks, win), :]
        qs = _stack_heads(q, qmask)
        s = lax.dot_general(qs, kw, _NT, preferred_element_type=jnp.float32)
        s = s + band_ref[(q0 - ks) // B_REACH]
        mx = jnp.max(s, axis=-1, keepdims=True)
        e = jnp.exp(s - mx)
        den = jnp.sum(e, axis=-1, keepdims=True)
        o = jnp.dot(e.astype(jnp.bfloat16), vw, preferred_element_type=jnp.float32)
        o = o * (1.0 / den)
        lse = jnp.broadcast_to(mx + jnp.log(den), o.shape)
        o_acc = _unstack_heads(o, vmask, tq)
        l_acc = _unstack_heads(lse, vmask, tq)
        for slab in range(HEAD_BLOCK // LANES):
            lanes = slice(slab * LANES, (slab + 1) * LANES)
            o_ref[0, slab, rows, :] = o_acc[:, lanes]
            lse_ref[0, slab, rows, :] = l_acc[:, lanes]

    def body(i, carry):
        for u in range(unroll):
            one_block(i * unroll + u)
        return carry

    lax.fori_loop(0, dilation * nblk // unroll, body, 0)


def _band_attention(proj, group, dilation):
    bn, s_len, _ = proj.shape
    length = s_len // dilation
    tq = min(B_TQ_MAX, length)
    win = min(tq + 2 * B_REACH, length)
    unroll = min(B_UNROLL_MAX, s_len // tq)
    assert length % tq == 0 and (s_len // tq) % unroll == 0 and dilation in (1, 4, 16)
    qblk = COL_BQ // HEAD_BLOCK + group
    kblk = COL_BK // HEAD_BLOCK + group
    vblk = COL_BV // HEAD_BLOCK + group

    def in_spec(blk):
        return pl.BlockSpec((1, s_len, HEAD_BLOCK), lambda b: (b, 0, blk))

    slabs = HEAD_BLOCK // LANES
    out_sds = jax.ShapeDtypeStruct((bn, slabs, s_len, LANES), jnp.float32)
    out_spec = pl.BlockSpec((1, slabs, s_len, LANES), lambda b: (b, 0, 0, 0))
    scratch = [pltpu.VMEM((B_CASES, B_HEADS * tq, win), jnp.float32)]
    if dilation > 1:
        regrouped = pltpu.VMEM((dilation, length, HEAD_BLOCK), jnp.bfloat16)
        scratch += [pltpu.VMEM((s_len, LANES), jnp.float32), pltpu.VMEM((s_len // 4, LANES), jnp.float32),
                    regrouped, regrouped, regrouped]
    return pl.pallas_call(
        functools.partial(_band_attn_kernel, dilation=dilation, length=length, tq=tq, win=win,
                          unroll=unroll),
        out_shape=(out_sds, out_sds),
        grid=(bn,),
        in_specs=[in_spec(qblk), in_spec(kblk), in_spec(vblk)],
        out_specs=(out_spec, out_spec),
        scratch_shapes=scratch,
        compiler_params=pltpu.CompilerParams(
            dimension_semantics=("parallel",),
            vmem_limit_bytes=VMEM_LIMIT),
        name=f"band_attn_d{dilation}",
    )(proj, proj, proj)


def _na_table_kernel(rpb_ref, tab_ref, *, rows):
    h = pl.program_id(0)
    n_dr = 2 * NA_KH - 1
    n_dc = 2 * NA_KW - 1
    shape = (GRID_W, LANES)
    qc = lax.broadcasted_iota(jnp.int32, shape, 0)
    lane = lax.broadcasted_iota(jnp.int32, shape, 1)
    kc = lane % GRID_W
    dc_i = jnp.clip(kc - qc + (NA_KW - 1), 0, n_dc - 1)
    cs = jnp.clip(qc - NA_KW // 2, 0, GRID_W - NA_KW)
    col_valid = (kc >= cs) & (kc < cs + NA_KW)
    masked = jnp.full(shape, MASK_VALUE, jnp.float32)

    row_bias = []
    for dr in range(n_dr):
        base = (h * n_dr + dr) * n_dc
        t = masked
        for dc in range(n_dc):
            t = jnp.where(dc_i == dc, rpb_ref[base + dc], t)
        row_bias.append(jnp.where(col_valid, t, MASK_VALUE))

    pair_rep = (0, 1, 2, rows // 2 - 2, rows // 2 - 1)
    for case in range(C_CASES):
        ip = pair_rep[case]
        ws = min(max(2 * ip - NA_KH // 2, 0), rows - C_KROWS)
        for qr_l in range(2):
            r = 2 * ip + qr_l
            rs = min(max(r - NA_KH // 2, 0), rows - NA_KH)
            for tile in range(C_WIN // LANES):
                halves = []
                for kr in (ws + 2 * tile, ws + 2 * tile + 1):
                    halves.append(row_bias[kr - r + NA_KH - 1] if rs <= kr < rs + NA_KH else masked)
                tab_ref[case, 0, qr_l * GRID_W:(qr_l + 1) * GRID_W, tile * LANES:(tile + 1) * LANES] = (
                    jnp.where(lane < GRID_W, halves[0], halves[1]))


def _na_tables(rpb, rows):
    flat = rpb.reshape(-1)
    return pl.pallas_call(
        functools.partial(_na_table_kernel, rows=rows),
        out_shape=jax.ShapeDtypeStruct((C_CASES, C_HEADS, C_PAIR, C_WIN), jnp.float32),
        grid=(C_HEADS,),
        in_specs=[pl.BlockSpec(memory_space=pltpu.SMEM)],
        out_specs=pl.BlockSpec((C_CASES, 1, C_PAIR, C_WIN), lambda h: (0, h, 0, 0)),
        name="na_tables",
    )(flat)


def _na_kernel(q_ref, k_ref, v_ref, tab_ref, o_ref, *, rows):
    rb = pl.program_id(1)
    qmask = _head_lane_masks((C_PAIR, HEAD_BLOCK))
    pairs_per_step = C_RB // 2
    for t in range(pairs_per_step):
        ip = rb * pairs_per_step + t
        ws = jnp.clip(2 * ip - NA_KH // 2, 0, rows - C_KROWS)
        case = ip - ws // 2
        k0 = pl.multiple_of(ws * GRID_W, C_PAIR)
        q = q_ref[0, t * C_PAIR:(t + 1) * C_PAIR, :]
        kw = k_ref[0, pl.ds(k0, C_WIN), :]
        vw = v_ref[0, pl.ds(k0, C_WIN), :]
        qs = _stack_heads(q, qmask)
        s = lax.dot_general(qs, kw, _NT, preferred_element_type=jnp.float32)
        s = s + tab_ref[case]
        mx = jnp.max(s, axis=-1, keepdims=True)
        e = jnp.exp(s - mx)
        den = jnp.sum(e, axis=-1, keepdims=True)
        o = jnp.dot(e.astype(jnp.bfloat16), vw, preferred_element_type=jnp.float32)
        o_acc = _unstack_heads(o * (1.0 / den), qmask, C_PAIR)
        o_ref[0, t * C_PAIR:(t + 1) * C_PAIR, :] = o_acc.astype(o_ref.dtype)


def _neighborhood_attention(proj, tables):
    bn, s_len, _ = proj.shape
    rows = s_len // GRID_W
    tq = C_RB * GRID_W
    qblk = COL_CQ // HEAD_BLOCK
    kblk = COL_CK // HEAD_BLOCK
    vblk = COL_CV // HEAD_BLOCK
    return pl.pallas_call(
        functools.partial(_na_kernel, rows=rows),
        out_shape=jax.ShapeDtypeStruct((bn, s_len, C_WIDTH), jnp.bfloat16),
        grid=(bn, rows // C_RB),
        in_specs=[pl.BlockSpec((1, tq, HEAD_BLOCK), lambda b, i: (b, i, qblk)),
                  pl.BlockSpec((1, s_len, HEAD_BLOCK), lambda b, i: (b, 0, kblk)),
                  pl.BlockSpec((1, s_len, HEAD_BLOCK), lambda b, i: (b, 0, vblk)),
                  pl.BlockSpec((C_CASES, C_HEADS * C_PAIR, C_WIN), lambda b, i: (0, 0, 0))],
        out_specs=pl.BlockSpec((1, tq, C_WIDTH), lambda b, i: (b, i, 0)),
        compiler_params=pltpu.CompilerParams(
            dimension_semantics=("parallel", "arbitrary"),
            vmem_limit_bytes=VMEM_LIMIT),
        name="na_attn",
    )(proj, proj, proj, tables.reshape(C_CASES, C_HEADS * C_PAIR, C_WIN))


def _merge_kernel(x_ref, gate_ref, ya_ref, ob0_ref, ob1_ref, ob2_ref, lb0_ref, lb1_ref, lb2_ref,
                  yc_ref, z_ref, g0_ref, g1_ref, g2_ref, wb_ref, wo_ref, fg_ref, o_ref, *, final_norm):
    yb_slabs = []
    for slab in range(HEAD_BLOCK // LANES):
        l0, l1, l2 = lb0_ref[0, slab], lb1_ref[0, slab], lb2_ref[0, slab]
        lm = jnp.maximum(jnp.maximum(l0, l1), l2)
        e0, e1, e2 = jnp.exp(l0 - lm), jnp.exp(l1 - lm), jnp.exp(l2 - lm)
        yb_slabs.append((e0 * ob0_ref[0, slab] + e1 * ob1_ref[0, slab] + e2 * ob2_ref[0, slab])
                        * (1.0 / (e0 + e1 + e2)))
    yb = jnp.concatenate(yb_slabs, axis=-1)

    z = z_ref[0].astype(jnp.float32)
    sz = z * _sigmoid(z)
    ya = (ya_ref[0].astype(jnp.float32) * sz[:, :A_WIDTH]).astype(jnp.bfloat16)
    ybz = (yb * sz[:, A_WIDTH:A_WIDTH + B_WIDTH]).astype(jnp.bfloat16)
    ycz = (yc_ref[0].astype(jnp.float32) * sz[:, A_WIDTH + B_WIDTH:]).astype(jnp.bfloat16)

    pa = jnp.dot(ya, wb_ref[:A_WIDTH, :], preferred_element_type=jnp.float32)
    pb = jnp.dot(ybz, wb_ref[A_WIDTH:A_WIDTH + B_WIDTH, :], preferred_element_type=jnp.float32)
    pc = jnp.dot(ycz, wb_ref[A_WIDTH + B_WIDTH:, :], preferred_element_type=jnp.float32)
    merged = _sigmoid(g0_ref[0].astype(jnp.float32)) * pa
    merged = merged + _sigmoid(g1_ref[0].astype(jnp.float32)) * pb
    merged = merged + _sigmoid(g2_ref[0].astype(jnp.float32)) * pc
    out = jnp.dot(merged.astype(jnp.bfloat16), wo_ref[...], preferred_element_type=jnp.float32)
    xn = x_ref[0] + gate_ref[0] * out
    if final_norm:
        ms = jnp.mean(xn * xn, axis=-1, keepdims=True)
        xn = xn * lax.rsqrt(ms + RMS_EPS) * fg_ref[...]
    o_ref[0] = xn


def _merge(x, gate, ya, ob, lb, yc, proj, w_branch, w_out, final_gain, layer, final_norm):
    bn, s_len, _ = x.shape
    tm = MERGE_TM
    zblk = COL_Z // D_MODEL
    gblk = COL_G // D_MODEL

    def tok(width, blk=0):
        return pl.BlockSpec((1, tm, width), lambda b, i: (b, i, blk))

    slab = pl.BlockSpec((1, HEAD_BLOCK // LANES, tm, LANES), lambda b, i: (b, 0, i, 0))

    return pl.pallas_call(
        functools.partial(_merge_kernel, final_norm=final_norm),
        out_shape=jax.ShapeDtypeStruct((bn, s_len, D_MODEL), jnp.float32),
        grid=(bn, s_len // tm),
        in_specs=[tok(D_MODEL),
                  pl.BlockSpec((1, 1, D_MODEL), lambda b, i: (b, 0, 0)),
                  tok(A_WIDTH),
                  slab, slab, slab, slab, slab, slab,
                  tok(C_WIDTH),
                  tok(D_MODEL, zblk), tok(D_MODEL, gblk), tok(D_MODEL, gblk + 1), tok(D_MODEL, gblk + 2),
                  pl.BlockSpec((None, BR_WIDTH, D_MODEL), lambda b, i: (layer, 0, 0)),
                  pl.BlockSpec((None, D_MODEL, D_MODEL), lambda b, i: (layer, 0, 0)),
                  pl.BlockSpec((1, D_MODEL), lambda b, i: (0, 0))],
        out_specs=tok(D_MODEL),
        compiler_params=pltpu.CompilerParams(
            dimension_semantics=("parallel", "parallel"),
            vmem_limit_bytes=VMEM_LIMIT),
        name="merge",
    )(x, gate, ya, ob[0], ob[1], ob[2], lb[0], lb[1], lb[2], yc, proj, proj, proj, proj,
      w_branch, w_out, final_gain)


def _split_w_in(w_in):
    sizes = (A_QK_COLS, A_QK_COLS, A_WIDTH, B_COLS, B_COLS, B_COLS, C_WIDTH, C_WIDTH, C_WIDTH,
             BR_WIDTH, N_BRANCHES * D_MODEL)
    offs = np.concatenate([[0], np.cumsum(sizes)])
    parts = [w_in[..., offs[n]:offs[n + 1]] for n in range(len(sizes))]
    a_v = parts.pop(2)
    parts[:4] = [_to_rope_layout(p) for p in parts[:4]]
    w_main = jnp.concatenate(parts, axis=-1).astype(jnp.bfloat16)
    w_vt = jnp.swapaxes(a_v, 1, 2).astype(jnp.bfloat16)
    return w_main, w_vt


def kernel(x, c, positions, norm_gain, w_ada, b_ada, w_in, diff_lambda, diff_subln_gain, na_rpb,
           w_branch, w_out, final_gain):
    depth = w_in.shape[0]
    bn, s_len, _ = x.shape
    rows = s_len // GRID_W
    tabs = _rope_tables(positions)
    ada = _ada_all_layers(c, w_ada, b_ada)
    fg = final_gain.reshape(1, D_MODEL)
    w_main, w_vt = _split_w_in(w_in)
    w_branch = w_branch.astype(jnp.bfloat16)
    w_out = w_out.astype(jnp.bfloat16)
    for layer in range(depth):
        shift = ada[layer, :, None, :D_MODEL]
        scale = ada[layer, :, None, D_MODEL:2 * D_MODEL]
        gate = ada[layer, :, None, 2 * D_MODEL:]
        proj, vt = _in_proj(x, shift, scale, norm_gain[layer].reshape(1, D_MODEL), tabs, w_main, w_vt,
                            layer)
        lambda_init = 0.8 - 0.6 * math.exp(-0.3 * layer)
        ya = _diff_attention(proj, vt, diff_lambda[layer], diff_subln_gain[layer], lambda_init)
        ob, lb = [], []
        for g, (window, dilation) in enumerate(B_PATTERNS):
            assert (window // 2) // dilation == B_REACH
            o_g, l_g = _band_attention(proj, g, dilation)
            ob.append(o_g)
            lb.append(l_g)
        yc = _neighborhood_attention(proj, _na_tables(na_rpb[layer], rows))
        x = _merge(x, gate, ya, ob, lb, yc, proj, w_branch, w_out, fg, layer,
                   final_norm=(layer == depth - 1))
    return x
```

```python
import functools
import math

import numpy as np
import jax
import jax.numpy as jnp
from jax import lax
from jax.experimental import pallas as pl
from jax.experimental.pallas import tpu as pltpu

D_MODEL = 1024
HEAD_DIM = 64
ROT_DIM = HEAD_DIM // 4
ROT_HALF = ROT_DIM // 2
ROPE_THETA = 500000.0
RMS_EPS = 1e-6
MASK_VALUE = -1e30
LOG2_E = math.log2(math.e)

A_HEADS = 4
A_VDIM = 2 * HEAD_DIM
A_WIDTH = A_HEADS * A_VDIM
A_QK_COLS = A_HEADS * 2 * HEAD_DIM

B_PATTERNS = ((128, 1), (512, 4), (2048, 16))
B_GROUPS = 3
B_HEADS = 4
B_WIDTH = B_HEADS * HEAD_DIM
B_COLS = B_GROUPS * B_WIDTH

GRID_W = 64
C_HEADS = 4
NA_KH = 8
NA_KW = 16
C_WIDTH = C_HEADS * HEAD_DIM

N_BRANCHES = 3
BR_WIDTH = A_WIDTH + B_WIDTH + C_WIDTH

LANES = 128
HEAD_BLOCK = 4 * HEAD_DIM
ROPE_LANES = (HEAD_BLOCK // HEAD_DIM) * ROT_HALF
VMEM_LIMIT = 56 * 1024 * 1024

COL_AQ = 0
COL_AK = COL_AQ + A_QK_COLS
COL_BQ = COL_AK + A_QK_COLS
COL_BK = COL_BQ + B_COLS
COL_BV = COL_BK + B_COLS
COL_CQ = COL_BV + B_COLS
COL_CK = COL_CQ + C_WIDTH
COL_CV = COL_CK + C_WIDTH
COL_Z = COL_CV + C_WIDTH
COL_G = COL_Z + BR_WIDTH
MAIN_COLS = COL_G + N_BRANCHES * D_MODEL
ROPE_COLS = COL_BV

PROJ_TM = 2048
PROJ_TN = 1024
A_TQ = 1024
A_KC = 512
A_VROWS = A_VDIM + 16
B_TQ_MAX = 128
B_UNROLL_MAX = 4
B_REACH = 64
B_CASES = 3
C_PAIR = 2 * GRID_W
C_KROWS = 10
C_WIN = C_KROWS * GRID_W
C_CASES = 5
C_RB = 8
MERGE_TM = 512

_NT = (((1,), (1,)), ((), ()))


def _sigmoid(x):
    return 1.0 / (1.0 + jnp.exp2(x * (-LOG2_E)))


def _to_rope_layout(w):
    lead = w.shape[:-1]
    nb = w.shape[-1] // HEAD_BLOCK
    heads = HEAD_BLOCK // HEAD_DIM
    w4 = w.reshape(*lead, nb, heads, HEAD_DIM)
    first = w4[..., :ROT_HALF].reshape(*lead, nb, ROPE_LANES)
    second = w4[..., ROT_HALF:ROT_DIM].reshape(*lead, nb, ROPE_LANES)
    rest = w4[..., ROT_DIM:]
    rest_lo = rest[..., :heads // 2, :].reshape(*lead, nb, LANES - ROPE_LANES)
    rest_hi = rest[..., heads // 2:, :].reshape(*lead, nb, LANES - ROPE_LANES)
    out = jnp.concatenate([first, rest_lo, second, rest_hi], axis=-1)
    return out.reshape(*lead, nb * HEAD_BLOCK)


def _rope_layout_head(col):
    l = col % LANES
    c = col // LANES
    rest = HEAD_DIM - ROT_DIM
    return jnp.where(l < ROPE_LANES, l // ROT_HALF, 2 * c + jnp.where(l >= ROPE_LANES + rest, 1, 0))


def _rope_table_kernel(pos_ref, inv_ref, c_ref, s_ref):
    ang = pos_ref[0] * inv_ref[...]
    lane = lax.broadcasted_iota(jnp.int32, ang.shape, 1)
    c_ref[0] = jnp.where(lane < ROPE_LANES, jnp.cos(ang), 1.0)
    s_ref[0] = jnp.where(lane < ROPE_LANES, jnp.sin(ang), 0.0)


def _rope_tables(positions):
    bn, s_len = positions.shape
    inv = np.float32(ROPE_THETA) ** (-np.arange(0, ROT_DIM, 2, dtype=np.float32) / np.float32(ROT_DIM))
    inv_lane = np.zeros((1, LANES), np.float32)
    inv_lane[0, :ROPE_LANES] = np.tile(inv, ROPE_LANES // ROT_HALF)
    pos = positions.astype(jnp.float32)[..., None]
    ts = 1024
    tab = jax.ShapeDtypeStruct((bn, s_len, LANES), jnp.float32)
    spec = pl.BlockSpec((1, ts, LANES), lambda b, i: (b, i, 0))
    return pl.pallas_call(
        _rope_table_kernel,
        out_shape=(tab, tab),
        grid=(bn, s_len // ts),
        in_specs=[pl.BlockSpec((1, ts, 1), lambda b, i: (b, i, 0)),
                  pl.BlockSpec((1, LANES), lambda b, i: (0, 0))],
        out_specs=(spec, spec),
        name="rope_tables",
    )(pos, jnp.asarray(inv_lane))


def _ada_kernel(c_ref, w_ref, b_ref, o_ref):
    c = c_ref[...]
    c_act = c * _sigmoid(c)
    o_ref[0] = jnp.dot(c_act, w_ref[0], precision=lax.Precision.HIGHEST,
                       preferred_element_type=jnp.float32) + b_ref[0]


def _ada_all_layers(c, w_ada, b_ada):
    depth = w_ada.shape[0]
    bn = c.shape[0]
    rows = 8
    c_pad = jnp.pad(c, ((0, rows - bn), (0, 0)))
    tn = 1024
    out = pl.pallas_call(
        _ada_kernel,
        out_shape=jax.ShapeDtypeStruct((depth, rows, 3 * D_MODEL), jnp.float32),
        grid=(depth, 3 * D_MODEL // tn),
        in_specs=[pl.BlockSpec((rows, D_MODEL), lambda l, j: (0, 0)),
                  pl.BlockSpec((1, D_MODEL, tn), lambda l, j: (l, 0, j)),
                  pl.BlockSpec((1, 1, tn), lambda l, j: (l, 0, j))],
        out_specs=pl.BlockSpec((1, rows, tn), lambda l, j: (l, 0, j)),
        name="adaln",
    )(c_pad, w_ada, b_ada.reshape(depth, 1, 3 * D_MODEL))
    return out[:, :bn]


def _in_proj_kernel(x_ref, shift_ref, scale_ref, gain_ref, c_ref, s_ref, w_ref, wvt_ref,
                    proj_ref, vt_ref, h_ref):
    j = pl.program_id(2)

    @pl.when(j == 0)
    def _():
        xf = x_ref[0]
        ms = jnp.mean(xf * xf, axis=-1, keepdims=True)
        y = xf * lax.rsqrt(ms + RMS_EPS) * gain_ref[...]
        h = (y * (1.0 + scale_ref[0]) + shift_ref[0]).astype(jnp.bfloat16)
        h_ref[...] = h
        vt = lax.dot_general(wvt_ref[...], h, _NT, preferred_element_type=jnp.float32)
        vt_ref[0] = vt.astype(vt_ref.dtype)

    acc = jnp.dot(h_ref[...], w_ref[...], preferred_element_type=jnp.float32)
    blocks_per_tile = PROJ_TN // HEAD_BLOCK
    cc = c_ref[0]
    ss = s_ref[0]
    for blk in range(blocks_per_tile):
        rotary = (j * blocks_per_tile + blk) < (ROPE_COLS // HEAD_BLOCK)
        cb = jnp.where(rotary, cc, 1.0)
        sb = jnp.where(rotary, ss, 0.0)
        lo0 = blk * HEAD_BLOCK
        hi0 = lo0 + LANES
        lo = acc[:, lo0:hi0]
        hi = acc[:, hi0:hi0 + LANES]
        proj_ref[0, :, lo0:hi0] = (lo * cb - hi * sb).astype(proj_ref.dtype)
        proj_ref[0, :, hi0:hi0 + LANES] = (hi * cb + lo * sb).astype(proj_ref.dtype)


def _in_proj(x, shift, scale, gain, tabs, w_main, w_vt, layer):
    bn, s_len, _ = x.shape
    tm, tn = PROJ_TM, PROJ_TN
    tab_spec = pl.BlockSpec((1, tm, LANES), lambda b, i, j: (b, i, 0))
    mod_spec = pl.BlockSpec((1, 1, D_MODEL), lambda b, i, j: (b, 0, 0))
    return pl.pallas_call(
        _in_proj_kernel,
        out_shape=(jax.ShapeDtypeStruct((bn, s_len, MAIN_COLS), jnp.bfloat16),
                   jax.ShapeDtypeStruct((bn, A_WIDTH, s_len), jnp.bfloat16)),
        grid=(bn, s_len // tm, MAIN_COLS // tn),
        in_specs=[pl.BlockSpec((1, tm, D_MODEL), lambda b, i, j: (b, i, 0)),
                  mod_spec, mod_spec,
                  pl.BlockSpec((1, D_MODEL), lambda b, i, j: (0, 0)),
                  tab_spec, tab_spec,
                  pl.BlockSpec((None, D_MODEL, tn), lambda b, i, j: (layer, 0, j)),
                  pl.BlockSpec((None, A_WIDTH, D_MODEL), lambda b, i, j: (layer, 0, 0))],
        out_specs=(pl.BlockSpec((1, tm, tn), lambda b, i, j: (b, i, j)),
                   pl.BlockSpec((1, A_WIDTH, tm), lambda b, i, j: (b, 0, i))),
        scratch_shapes=[pltpu.VMEM((tm, D_MODEL), jnp.bfloat16)],
        compiler_params=pltpu.CompilerParams(
            dimension_semantics=("parallel", "parallel", "arbitrary"),
            vmem_limit_bytes=VMEM_LIMIT),
        name="in_proj",
    )(x, shift, scale, gain, *tabs, w_main, w_vt)


def _diff_attn_kernel(q_ref, k_ref, vt_ref, dl_ref, gain_ref, o_ref,
                      qm_ref, vta_ref, sa_ref, sb_ref, m_ref, acc_ref, *, lambda_init, s_len):
    h = pl.program_id(1)
    tq = q_ref.shape[1]
    n_chunks = s_len // A_KC

    @pl.when(pl.program_id(2) == 0)
    def _():
        vta_ref[:A_VDIM, :] = vt_ref[0]
        row = lax.broadcasted_iota(jnp.int32, (A_VROWS - A_VDIM, s_len), 0)
        vta_ref[A_VDIM:, :] = jnp.where(row == 0, 1.0, 0.0).astype(vta_ref.dtype)

    q = q_ref[0]
    pair = _rope_layout_head(lax.broadcasted_iota(jnp.int32, q.shape, 1))
    base = 2 * (h % 2)
    for c in range(2):
        qsel = jnp.where(pair == base + c, q, jnp.zeros_like(q)).astype(jnp.float32)
        qm_ref[c] = (qsel * (HEAD_DIM ** -0.5 * LOG2_E)).astype(qm_ref.dtype)
        m_ref[c] = jnp.full((1, tq), MASK_VALUE, jnp.float32)
        acc_ref[c] = jnp.zeros((A_VROWS, tq), jnp.float32)

    def scores(i, s_ref):
        kc = k_ref[0, pl.ds(pl.multiple_of(i * A_KC, A_KC), A_KC), :]
        for c in range(2):
            s_ref[c] = lax.dot_general(kc, qm_ref[c], _NT,
                                       preferred_element_type=jnp.float32)

    def update(i, s_ref):
        vt_c = vta_ref[:, pl.ds(pl.multiple_of(i * A_KC, A_KC), A_KC)]
        for c in range(2):
            st = s_ref[c]
            m_old = m_ref[c]
            m_new = jnp.maximum(m_old, jnp.max(st, axis=0, keepdims=True))
            alpha = jnp.exp2(m_old - m_new)
            e = jnp.exp2(st - m_new).astype(jnp.bfloat16)
            pv = jnp.dot(vt_c, e, preferred_element_type=jnp.float32)
            acc_ref[c] = alpha * acc_ref[c] + pv
            m_ref[c] = m_new

    scores(0, sa_ref)

    def body(j, carry):
        scores(2 * j + 1, sb_ref)
        update(2 * j, sa_ref)
        scores(2 * j + 2, sa_ref)
        update(2 * j + 1, sb_ref)
        return carry

    lax.fori_loop(0, n_chunks // 2 - 1, body, 0)
    scores(n_chunks - 1, sb_ref)
    update(n_chunks - 2, sa_ref)
    update(n_chunks - 1, sb_ref)

    dl = dl_ref[...]
    lam = (jnp.exp(jnp.sum(dl[0:1] * dl[1:2], axis=1, keepdims=True))
           - jnp.exp(jnp.sum(dl[2:3] * dl[3:4], axis=1, keepdims=True)) + lambda_init)
    maps = [acc_ref[c, :A_VDIM, :] * (1.0 / acc_ref[c, A_VDIM:A_VDIM + 1, :]) for c in range(2)]
    ot = maps[0] - lam * maps[1]
    ms = jnp.mean(ot * ot, axis=0, keepdims=True)
    y = ot * lax.rsqrt(ms + RMS_EPS) * gain_ref[...] * (1.0 - lambda_init)
    o_ref[0] = y.T.astype(o_ref.dtype)


def _diff_attention(proj, vt, diff_lambda, subln_gain, lambda_init):
    bn, s_len, _ = proj.shape
    tq = A_TQ
    qblk = COL_AQ // HEAD_BLOCK
    kblk = COL_AK // HEAD_BLOCK
    assert s_len % (2 * A_KC) == 0
    return pl.pallas_call(
        functools.partial(_diff_attn_kernel, lambda_init=lambda_init, s_len=s_len),
        out_shape=jax.ShapeDtypeStruct((bn, s_len, A_WIDTH), jnp.bfloat16),
        grid=(bn, A_HEADS, s_len // tq),
        in_specs=[pl.BlockSpec((1, tq, HEAD_BLOCK), lambda b, h, i: (b, i, qblk + h // 2)),
                  pl.BlockSpec((1, s_len, HEAD_BLOCK), lambda b, h, i: (b, 0, kblk + h // 2)),
                  pl.BlockSpec((1, A_VDIM, s_len), lambda b, h, i: (b, h, 0)),
                  pl.BlockSpec((4, HEAD_DIM), lambda b, h, i: (0, 0)),
                  pl.BlockSpec((A_VDIM, 1), lambda b, h, i: (0, 0))],
        out_specs=pl.BlockSpec((1, tq, A_VDIM), lambda b, h, i: (b, i, h)),
        scratch_shapes=[pltpu.VMEM((2, tq, HEAD_BLOCK), jnp.bfloat16),
                        pltpu.VMEM((A_VROWS, s_len), jnp.bfloat16),
                        pltpu.VMEM((2, A_KC, tq), jnp.float32),
                        pltpu.VMEM((2, A_KC, tq), jnp.float32),
                        pltpu.VMEM((2, 1, tq), jnp.float32),
                        pltpu.VMEM((2, A_VROWS, tq), jnp.float32)],
        compiler_params=pltpu.CompilerParams(
            dimension_semantics=("parallel", "parallel", "arbitrary"),
            vmem_limit_bytes=VMEM_LIMIT),
        name="diff_attn",
    )(proj, proj, vt, diff_lambda, subln_gain.reshape(A_VDIM, 1))


def _head_lane_masks(shape, rotary_layout=False):
    col = lax.broadcasted_iota(jnp.int32, shape, 1)
    head = _rope_layout_head(col) if rotary_layout else col // HEAD_DIM
    return [head == h for h in range(HEAD_BLOCK // HEAD_DIM)]


def _stack_heads(q, qmask):
    zero = jnp.zeros_like(q)
    return jnp.concatenate([jnp.where(mk, q, zero) for mk in qmask], axis=0) * (HEAD_DIM ** -0.5)


def _unstack_heads(x, qmask, tq):
    out = x[:tq]
    for h in range(1, len(qmask)):
        out = jnp.where(qmask[h], x[h * tq:(h + 1) * tq], out)
    return out


def _band_group(q_ref, k_ref, v_ref, y_ref, band_ref, stage_ref, tmp_ref, qd_ref, kd_ref, vd_ref,
                run_o_ref, run_l_ref, *, dilation, s_len, first, last):
    length = s_len // dilation
    tq = min(B_TQ_MAX, length)
    win = min(tq + 2 * B_REACH, length)
    unroll = min(B_UNROLL_MAX, s_len // tq)
    nblk = length // tq
    assert length % tq == 0 and (s_len // tq) % unroll == 0 and dilation in (1, 4, 16)
    assert not (last and dilation > 1)
    qmask = _head_lane_masks((tq, HEAD_BLOCK), rotary_layout=True)
    vmask = _head_lane_masks((tq, HEAD_BLOCK))

    if dilation > 1:
        for src, dst in ((q_ref, qd_ref), (k_ref, kd_ref), (v_ref, vd_ref)):
            for slab in range(HEAD_BLOCK // LANES):
                lanes = slice(slab * LANES, (slab + 1) * LANES)
                stage_ref[...] = src[0, :, lanes].astype(jnp.float32)
                for m4 in range(4):
                    quarter = stage_ref[pl.ds(m4, s_len // 4, stride=4), :]
                    if dilation == 4:
                        dst[m4 * length:(m4 + 1) * length, lanes] = quarter.astype(dst.dtype)
                    else:
                        tmp_ref[...] = quarter
                        for mm in range(4):
                            m = m4 + 4 * mm
                            dst[m * length:(m + 1) * length, lanes] = tmp_ref[
                                pl.ds(mm, length, stride=4), :].astype(dst.dtype)
        q_src, k_src, v_src = qd_ref, kd_ref, vd_ref
    else:
        q_src, k_src, v_src = q_ref.at[0], k_ref.at[0], v_ref.at[0]

    def one_block(t):
        m = t // nblk
        q0 = pl.multiple_of((t % nblk) * tq, tq)
        ks = pl.multiple_of(jnp.clip(q0 - B_REACH, 0, length - win), B_REACH)
        base = pl.multiple_of(m * length, tq)
        if dilation > 1:
            rows = pl.ds(q0 * dilation + m, tq, stride=dilation)
        else:
            rows = pl.ds(q0, tq)
        q = q_src[pl.ds(base + q0, tq), :]
        kw = k_src[pl.ds(base + ks, win), :]
        vw = v_src[pl.ds(base + ks, win), :]
        qs = _stack_heads(q, qmask)
        s = lax.dot_general(qs, kw, _NT, preferred_element_type=jnp.float32)
        s = s + band_ref[(q0 - ks) // B_REACH]
        mx = jnp.max(s, axis=-1, keepdims=True)
        e = jnp.exp(s - mx)
        den = jnp.sum(e, axis=-1, keepdims=True)
        o = jnp.dot(e.astype(jnp.bfloat16), vw, preferred_element_type=jnp.float32)
        o = o * (1.0 / den)
        lse = jnp.broadcast_to(mx + jnp.log(den), o.shape)
        o_new = _unstack_heads(o, vmask, tq)
        l_new = _unstack_heads(lse, vmask, tq)
        for slab in range(HEAD_BLOCK // LANES):
            lanes = slice(slab * LANES, (slab + 1) * LANES)
            o_s, l_s = o_new[:, lanes], l_new[:, lanes]
            if not first:
                o_run, l_run = run_o_ref[slab, rows, :], run_l_ref[slab, rows, :]
                l_max = jnp.maximum(l_run, l_s)
                w_run, w_new = jnp.exp(l_run - l_max), jnp.exp(l_s - l_max)
                total = w_run + w_new
                o_s = (o_run * w_run + o_s * w_new) * (1.0 / total)
                l_s = l_max + jnp.log(total)
            if last:
                y_ref[0, rows, lanes] = o_s.astype(y_ref.dtype)
            else:
                run_o_ref[slab, rows, :] = o_s
                run_l_ref[slab, rows, :] = l_s

    def body(i, carry):
        for u in range(unroll):
            one_block(i * unroll + u)
        return carry

    lax.fori_loop(0, dilation * nblk // unroll, body, 0)


def _band_mix_kernel(q_ref, k_ref, v_ref, y_ref, band_ref, *scratch, s_len, dilations):
    g = pl.program_id(1)

    tq = band_ref.shape[1] // B_HEADS
    row = lax.broadcasted_iota(jnp.int32, band_ref.shape[1:], 0) % tq
    col = lax.broadcasted_iota(jnp.int32, band_ref.shape[1:], 1)
    for case in range(B_CASES):
        valid = jnp.abs(col - case * B_REACH - row) <= B_REACH
        band_ref[case] = jnp.where(valid, 0.0, MASK_VALUE)

    for step, dilation in enumerate(dilations):
        @pl.when(g == step)
        def _(step=step, dilation=dilation):
            _band_group(q_ref, k_ref, v_ref, y_ref, band_ref, *scratch, dilation=dilation,
                        s_len=s_len, first=(step == 0), last=(step == len(dilations) - 1))


def _band_mixture(proj):
    bn, s_len, _ = proj.shape
    n_groups = len(B_PATTERNS)
    order = sorted(range(n_groups), key=lambda g: -B_PATTERNS[g][1])
    dilations = tuple(B_PATTERNS[g][1] for g in order)
    for window, dilation in B_PATTERNS:
        assert (window // 2) // dilation == B_REACH
    assert order == list(range(n_groups - 1, -1, -1))

    def in_spec(col0):
        blk = col0 // HEAD_BLOCK
        return pl.BlockSpec((1, s_len, HEAD_BLOCK), lambda b, g: (b, 0, blk + n_groups - 1 - g))

    slabs = HEAD_BLOCK // LANES
    regrouped = pltpu.VMEM((s_len, HEAD_BLOCK), jnp.bfloat16)
    state = pltpu.VMEM((slabs, s_len, LANES), jnp.float32)
    return pl.pallas_call(
        functools.partial(_band_mix_kernel, s_len=s_len, dilations=dilations),
        out_shape=jax.ShapeDtypeStruct((bn, s_len, B_WIDTH), jnp.bfloat16),
        grid=(bn, n_groups),
        in_specs=[in_spec(COL_BQ), in_spec(COL_BK), in_spec(COL_BV)],
        out_specs=pl.BlockSpec((1, s_len, B_WIDTH), lambda b, g: (b, 0, 0)),
        scratch_shapes=[pltpu.VMEM((B_CASES, B_HEADS * B_TQ_MAX, B_TQ_MAX + 2 * B_REACH), jnp.float32),
                        pltpu.VMEM((s_len, LANES), jnp.float32),
                        pltpu.VMEM((s_len // 4, LANES), jnp.float32),
                        regrouped, regrouped, regrouped,
                        state, state],
        compiler_params=pltpu.CompilerParams(
            dimension_semantics=("parallel", "arbitrary"),
            vmem_limit_bytes=VMEM_LIMIT),
        name="band_mix",
    )(proj, proj, proj)


def _na_table_kernel(rpb_ref, tab_ref, *, rows):
    h = pl.program_id(0)
    n_dr = 2 * NA_KH - 1
    n_dc = 2 * NA_KW - 1
    shape = (GRID_W, LANES)
    qc = lax.broadcasted_iota(jnp.int32, shape, 0)
    lane = lax.broadcasted_iota(jnp.int32, shape, 1)
    kc = lane % GRID_W
    dc_i = jnp.clip(kc - qc + (NA_KW - 1), 0, n_dc - 1)
    cs = jnp.clip(qc - NA_KW // 2, 0, GRID_W - NA_KW)
    col_valid = (kc >= cs) & (kc < cs + NA_KW)
    masked = jnp.full(shape, MASK_VALUE, jnp.float32)

    row_bias = []
    for dr in range(n_dr):
        base = (h * n_dr + dr) * n_dc
        t = masked
        for dc in range(n_dc):
            t = jnp.where(dc_i == dc, rpb_ref[base + dc], t)
        row_bias.append(jnp.where(col_valid, t, MASK_VALUE))

    pair_rep = (0, 1, 2, rows // 2 - 2, rows // 2 - 1)
    for case in range(C_CASES):
        ip = pair_rep[case]
        ws = min(max(2 * ip - NA_KH // 2, 0), rows - C_KROWS)
        for qr_l in range(2):
            r = 2 * ip + qr_l
            rs = min(max(r - NA_KH // 2, 0), rows - NA_KH)
            for tile in range(C_WIN // LANES):
                halves = []
                for kr in (ws + 2 * tile, ws + 2 * tile + 1):
                    halves.append(row_bias[kr - r + NA_KH - 1] if rs <= kr < rs + NA_KH else masked)
                tab_ref[case, 0, qr_l * GRID_W:(qr_l + 1) * GRID_W, tile * LANES:(tile + 1) * LANES] = (
                    jnp.where(lane < GRID_W, halves[0], halves[1]))


def _na_tables(rpb, rows):
    flat = rpb.reshape(-1)
    return pl.pallas_call(
        functools.partial(_na_table_kernel, rows=rows),
        out_shape=jax.ShapeDtypeStruct((C_CASES, C_HEADS, C_PAIR, C_WIN), jnp.float32),
        grid=(C_HEADS,),
        in_specs=[pl.BlockSpec(memory_space=pltpu.SMEM)],
        out_specs=pl.BlockSpec((C_CASES, 1, C_PAIR, C_WIN), lambda h: (0, h, 0, 0)),
        name="na_tables",
    )(flat)


def _na_kernel(q_ref, k_ref, v_ref, tab_ref, o_ref, *, rows):
    rb = pl.program_id(1)
    qmask = _head_lane_masks((C_PAIR, HEAD_BLOCK))
    pairs_per_step = C_RB // 2
    for t in range(pairs_per_step):
        ip = rb * pairs_per_step + t
        ws = jnp.clip(2 * ip - NA_KH // 2, 0, rows - C_KROWS)
        case = ip - ws // 2
        k0 = pl.multiple_of(ws * GRID_W, C_PAIR)
        q = q_ref[0, t * C_PAIR:(t + 1) * C_PAIR, :]
        kw = k_ref[0, pl.ds(k0, C_WIN), :]
        vw = v_ref[0, pl.ds(k0, C_WIN), :]
        qs = _stack_heads(q, qmask)
        s = lax.dot_general(qs, kw, _NT, preferred_element_type=jnp.float32)
        s = s + tab_ref[case]
        mx = jnp.max(s, axis=-1, keepdims=True)
        e = jnp.exp(s - mx)
        den = jnp.sum(e, axis=-1, keepdims=True)
        o = jnp.dot(e.astype(jnp.bfloat16), vw, preferred_element_type=jnp.float32)
        o_acc = _unstack_heads(o * (1.0 / den), qmask, C_PAIR)
        o_ref[0, t * C_PAIR:(t + 1) * C_PAIR, :] = o_acc.astype(o_ref.dtype)


def _neighborhood_attention(proj, tables):
    bn, s_len, _ = proj.shape
    rows = s_len // GRID_W
    tq = C_RB * GRID_W
    qblk = COL_CQ // HEAD_BLOCK
    kblk = COL_CK // HEAD_BLOCK
    vblk = COL_CV // HEAD_BLOCK
    return pl.pallas_call(
        functools.partial(_na_kernel, rows=rows),
        out_shape=jax.ShapeDtypeStruct((bn, s_len, C_WIDTH), jnp.bfloat16),
        grid=(bn, rows // C_RB),
        in_specs=[pl.BlockSpec((1, tq, HEAD_BLOCK), lambda b, i: (b, i, qblk)),
                  pl.BlockSpec((1, s_len, HEAD_BLOCK), lambda b, i: (b, 0, kblk)),
                  pl.BlockSpec((1, s_len, HEAD_BLOCK), lambda b, i: (b, 0, vblk)),
                  pl.BlockSpec((C_CASES, C_HEADS * C_PAIR, C_WIN), lambda b, i: (0, 0, 0))],
        out_specs=pl.BlockSpec((1, tq, C_WIDTH), lambda b, i: (b, i, 0)),
        compiler_params=pltpu.CompilerParams(
            dimension_semantics=("parallel", "arbitrary"),
            vmem_limit_bytes=VMEM_LIMIT),
        name="na_attn",
    )(proj, proj, proj, tables.reshape(C_CASES, C_HEADS * C_PAIR, C_WIN))


def _merge_kernel(x_ref, gate_ref, ya_ref, yb_ref, yc_ref, z_ref, g0_ref, g1_ref, g2_ref,
                  wb_ref, wo_ref, fg_ref, o_ref, *, final_norm):
    z = z_ref[0].astype(jnp.float32)
    sz = z * _sigmoid(z)
    ya = (ya_ref[0].astype(jnp.float32) * sz[:, :A_WIDTH]).astype(jnp.bfloat16)
    ybz = (yb_ref[0].astype(jnp.float32) * sz[:, A_WIDTH:A_WIDTH + B_WIDTH]).astype(jnp.bfloat16)
    ycz = (yc_ref[0].astype(jnp.float32) * sz[:, A_WIDTH + B_WIDTH:]).astype(jnp.bfloat16)

    pa = jnp.dot(ya, wb_ref[:A_WIDTH, :], preferred_element_type=jnp.float32)
    pb = jnp.dot(ybz, wb_ref[A_WIDTH:A_WIDTH + B_WIDTH, :], preferred_element_type=jnp.float32)
    pc = jnp.dot(ycz, wb_ref[A_WIDTH + B_WIDTH:, :], preferred_element_type=jnp.float32)
    merged = _sigmoid(g0_ref[0].astype(jnp.float32)) * pa
    merged = merged + _sigmoid(g1_ref[0].astype(jnp.float32)) * pb
    merged = merged + _sigmoid(g2_ref[0].astype(jnp.float32)) * pc
    out = jnp.dot(merged.astype(jnp.bfloat16), wo_ref[...], preferred_element_type=jnp.float32)
    xn = x_ref[0] + gate_ref[0] * out
    if final_norm:
        ms = jnp.mean(xn * xn, axis=-1, keepdims=True)
        xn = xn * lax.rsqrt(ms + RMS_EPS) * fg_ref[...]
    o_ref[0] = xn


def _merge(x, gate, ya, yb, yc, proj, w_branch, w_out, final_gain, layer, final_norm):
    bn, s_len, _ = x.shape
    tm = MERGE_TM
    zblk = COL_Z // D_MODEL
    gblk = COL_G // D_MODEL

    def tok(width, blk=0):
        return pl.BlockSpec((1, tm, width), lambda b, i: (b, i, blk))

    return pl.pallas_call(
        functools.partial(_merge_kernel, final_norm=final_norm),
        out_shape=jax.ShapeDtypeStruct((bn, s_len, D_MODEL), jnp.float32),
        grid=(bn, s_len // tm),
        in_specs=[tok(D_MODEL),
                  pl.BlockSpec((1, 1, D_MODEL), lambda b, i: (b, 0, 0)),
                  tok(A_WIDTH), tok(B_WIDTH), tok(C_WIDTH),
                  tok(D_MODEL, zblk), tok(D_MODEL, gblk), tok(D_MODEL, gblk + 1), tok(D_MODEL, gblk + 2),
                  pl.BlockSpec((None, BR_WIDTH, D_MODEL), lambda b, i: (layer, 0, 0)),
                  pl.BlockSpec((None, D_MODEL, D_MODEL), lambda b, i: (layer, 0, 0)),
                  pl.BlockSpec((1, D_MODEL), lambda b, i: (0, 0))],
        out_specs=tok(D_MODEL),
        compiler_params=pltpu.CompilerParams(
            dimension_semantics=("parallel", "parallel"),
            vmem_limit_bytes=VMEM_LIMIT),
        name="merge",
    )(x, gate, ya, yb, yc, proj, proj, proj, proj,
      w_branch, w_out, final_gain)


def _split_w_in(w_in):
    sizes = (A_QK_COLS, A_QK_COLS, A_WIDTH, B_COLS, B_COLS, B_COLS, C_WIDTH, C_WIDTH, C_WIDTH,
             BR_WIDTH, N_BRANCHES * D_MODEL)
    offs = np.concatenate([[0], np.cumsum(sizes)])
    parts = [w_in[..., offs[n]:offs[n + 1]] for n in range(len(sizes))]
    a_v = parts.pop(2)
    parts[:4] = [_to_rope_layout(p) for p in parts[:4]]
    w_main = jnp.concatenate(parts, axis=-1).astype(jnp.bfloat16)
    w_vt = jnp.swapaxes(a_v, 1, 2).astype(jnp.bfloat16)
    return w_main, w_vt


def kernel(x, c, positions, norm_gain, w_ada, b_ada, w_in, diff_lambda, diff_subln_gain, na_rpb,
           w_branch, w_out, final_gain):
    depth = w_in.shape[0]
    bn, s_len, _ = x.shape
    rows = s_len // GRID_W
    tabs = _rope_tables(positions)
    ada = _ada_all_layers(c, w_ada, b_ada)
    fg = final_gain.reshape(1, D_MODEL)
    w_main, w_vt = _split_w_in(w_in)
    w_branch = w_branch.astype(jnp.bfloat16)
    w_out = w_out.astype(jnp.bfloat16)
    for layer in range(depth):
        shift = ada[layer, :, None, :D_MODEL]
        scale = ada[layer, :, None, D_MODEL:2 * D_MODEL]
        gate = ada[layer, :, None, 2 * D_MODEL:]
        proj, vt = _in_proj(x, shift, scale, norm_gain[layer].reshape(1, D_MODEL), tabs, w_main, w_vt,
                            layer)
        lambda_init = 0.8 - 0.6 * math.exp(-0.3 * layer)
        ya = _diff_attention(proj, vt, diff_lambda[layer], diff_subln_gain[layer], lambda_init)
        yb = _band_mixture(proj)
        yc = _neighborhood_attention(proj, _na_tables(na_rpb[layer], rows))
        x = _merge(x, gate, ya, yb, yc, proj, w_branch, w_out, fg, layer,
                   final_norm=(layer == depth - 1))
    return x
```

```python
import functools
import math

import numpy as np
import jax
import jax.numpy as jnp
from jax import lax
from jax.experimental import pallas as pl
from jax.experimental.pallas import tpu as pltpu

D_MODEL = 1024
HEAD_DIM = 64
ROT_DIM = HEAD_DIM // 4
ROT_HALF = ROT_DIM // 2
ROPE_THETA = 500000.0
RMS_EPS = 1e-6
MASK_VALUE = -1e30
LOG2_E = math.log2(math.e)

A_HEADS = 4
A_VDIM = 2 * HEAD_DIM
A_WIDTH = A_HEADS * A_VDIM
A_QK_COLS = A_HEADS * 2 * HEAD_DIM

B_PATTERNS = ((128, 1), (512, 4), (2048, 16))
B_GROUPS = 3
B_HEADS = 4
B_WIDTH = B_HEADS * HEAD_DIM
B_COLS = B_GROUPS * B_WIDTH

GRID_W = 64
C_HEADS = 4
NA_KH = 8
NA_KW = 16
C_WIDTH = C_HEADS * HEAD_DIM

N_BRANCHES = 3
BR_WIDTH = A_WIDTH + B_WIDTH + C_WIDTH

LANES = 128
HEAD_BLOCK = 4 * HEAD_DIM
ROPE_LANES = (HEAD_BLOCK // HEAD_DIM) * ROT_HALF
VMEM_LIMIT = 56 * 1024 * 1024

COL_AQ = 0
COL_AK = COL_AQ + A_QK_COLS
COL_BQ = COL_AK + A_QK_COLS
COL_BK = COL_BQ + B_COLS
COL_BV = COL_BK + B_COLS
COL_CQ = COL_BV + B_COLS
COL_CK = COL_CQ + C_WIDTH
COL_CV = COL_CK + C_WIDTH
COL_Z = COL_CV + C_WIDTH
COL_G = COL_Z + BR_WIDTH
MAIN_COLS = COL_G + N_BRANCHES * D_MODEL
ROPE_COLS = COL_BV

W_PREP_COLS = 512
PROJ_TM = 2048
PROJ_TN = 1024
A_TQ = 1024
A_KC = 512
A_VROWS = A_VDIM + 16
B_TQ_MAX = 128
B_UNROLL_MAX = 8
B_REACH = 64
B_CASES = 3
C_PAIR = 2 * GRID_W
C_KROWS = 10
C_WIN = C_KROWS * GRID_W
C_CASES = 5
C_RB = 8
MERGE_TM = 512

_NT = (((1,), (1,)), ((), ()))


def _sigmoid(x):
    return 1.0 / (1.0 + jnp.exp2(x * (-LOG2_E)))


def _rope_layout_head(col):
    l = col % LANES
    c = col // LANES
    rest = HEAD_DIM - ROT_DIM
    return jnp.where(l < ROPE_LANES, l // ROT_HALF, 2 * c + jnp.where(l >= ROPE_LANES + rest, 1, 0))


def _rope_layout_source(col):
    l = col % LANES
    c = col // LANES
    rest = HEAD_DIM - ROT_DIM
    r = l - ROPE_LANES
    upper = jnp.where(r >= rest, 1, 0)
    rotary_src = HEAD_DIM * (l // ROT_HALF) + ROT_HALF * c + l % ROT_HALF
    other_src = HEAD_DIM * (2 * c + upper) + ROT_DIM + r - rest * upper
    return jnp.where(l < ROPE_LANES, rotary_src, other_src)


def _rope_table_kernel(pos_ref, inv_ref, c_ref, s_ref):
    ang = pos_ref[0] * inv_ref[...]
    lane = lax.broadcasted_iota(jnp.int32, ang.shape, 1)
    c_ref[0] = jnp.where(lane < ROPE_LANES, jnp.cos(ang), 1.0)
    s_ref[0] = jnp.where(lane < ROPE_LANES, jnp.sin(ang), 0.0)


def _rope_tables(positions):
    bn, s_len = positions.shape
    inv = np.float32(ROPE_THETA) ** (-np.arange(0, ROT_DIM, 2, dtype=np.float32) / np.float32(ROT_DIM))
    inv_lane = np.zeros((1, LANES), np.float32)
    inv_lane[0, :ROPE_LANES] = np.tile(inv, ROPE_LANES // ROT_HALF)
    pos = positions.astype(jnp.float32)[..., None]
    ts = 1024
    tab = jax.ShapeDtypeStruct((bn, s_len, LANES), jnp.float32)
    spec = pl.BlockSpec((1, ts, LANES), lambda b, i: (b, i, 0))
    return pl.pallas_call(
        _rope_table_kernel,
        out_shape=(tab, tab),
        grid=(bn, s_len // ts),
        in_specs=[pl.BlockSpec((1, ts, 1), lambda b, i: (b, i, 0)),
                  pl.BlockSpec((1, LANES), lambda b, i: (0, 0))],
        out_specs=(spec, spec),
        name="rope_tables",
    )(pos, jnp.asarray(inv_lane))


def _ada_kernel(c_ref, w_ref, b_ref, o_ref):
    c = c_ref[...]
    c_act = c * _sigmoid(c)
    o_ref[0] = jnp.dot(c_act, w_ref[0], precision=lax.Precision.HIGHEST,
                       preferred_element_type=jnp.float32) + b_ref[0]


def _ada_all_layers(c, w_ada, b_ada):
    depth = w_ada.shape[0]
    bn = c.shape[0]
    rows = 8
    c_pad = jnp.pad(c, ((0, rows - bn), (0, 0)))
    tn = 1024
    out = pl.pallas_call(
        _ada_kernel,
        out_shape=jax.ShapeDtypeStruct((depth, rows, 3 * D_MODEL), jnp.float32),
        grid=(depth, 3 * D_MODEL // tn),
        in_specs=[pl.BlockSpec((rows, D_MODEL), lambda l, j: (0, 0)),
                  pl.BlockSpec((1, D_MODEL, tn), lambda l, j: (l, 0, j)),
                  pl.BlockSpec((1, 1, tn), lambda l, j: (l, 0, j))],
        out_specs=pl.BlockSpec((1, rows, tn), lambda l, j: (l, 0, j)),
        name="adaln",
    )(c_pad, w_ada, b_ada.reshape(depth, 1, 3 * D_MODEL))
    return out[:, :bn]


def _in_proj_kernel(x_ref, shift_ref, scale_ref, gain_ref, c_ref, s_ref, w_ref, wvt_ref,
                    proj_ref, vt_ref, h_ref):
    j = pl.program_id(2)

    @pl.when(j == 0)
    def _():
        xf = x_ref[0]
        ms = jnp.mean(xf * xf, axis=-1, keepdims=True)
        y = xf * lax.rsqrt(ms + RMS_EPS) * gain_ref[...]
        h = (y * (1.0 + scale_ref[0]) + shift_ref[0]).astype(jnp.bfloat16)
        h_ref[...] = h
        vt = lax.dot_general(wvt_ref[...], h, _NT, preferred_element_type=jnp.float32)
        vt_ref[0] = vt.astype(vt_ref.dtype)

    acc = jnp.dot(h_ref[...], w_ref[...], preferred_element_type=jnp.float32)
    blocks_per_tile = PROJ_TN // HEAD_BLOCK
    cc = c_ref[0]
    ss = s_ref[0]
    for blk in range(blocks_per_tile):
        rotary = (j * blocks_per_tile + blk) < (ROPE_COLS // HEAD_BLOCK)
        cb = jnp.where(rotary, cc, 1.0)
        sb = jnp.where(rotary, ss, 0.0)
        lo0 = blk * HEAD_BLOCK
        hi0 = lo0 + LANES
        lo = acc[:, lo0:hi0]
        hi = acc[:, hi0:hi0 + LANES]
        proj_ref[0, :, lo0:hi0] = (lo * cb - hi * sb).astype(proj_ref.dtype)
        proj_ref[0, :, hi0:hi0 + LANES] = (hi * cb + lo * sb).astype(proj_ref.dtype)


def _in_proj(x, shift, scale, gain, tabs, w_main, w_vt, layer):
    bn, s_len, _ = x.shape
    tm, tn = PROJ_TM, PROJ_TN
    tab_spec = pl.BlockSpec((1, tm, LANES), lambda b, i, j: (b, i, 0))
    mod_spec = pl.BlockSpec((1, 1, D_MODEL), lambda b, i, j: (b, 0, 0))
    return pl.pallas_call(
        _in_proj_kernel,
        out_shape=(jax.ShapeDtypeStruct((bn, s_len, MAIN_COLS), jnp.bfloat16),
                   jax.ShapeDtypeStruct((bn, A_WIDTH, s_len), jnp.bfloat16)),
        grid=(bn, s_len // tm, MAIN_COLS // tn),
        in_specs=[pl.BlockSpec((1, tm, D_MODEL), lambda b, i, j: (b, i, 0)),
                  mod_spec, mod_spec,
                  pl.BlockSpec((1, D_MODEL), lambda b, i, j: (0, 0)),
                  tab_spec, tab_spec,
                  pl.BlockSpec((None, D_MODEL, tn), lambda b, i, j: (layer, 0, j)),
                  pl.BlockSpec((None, A_WIDTH, D_MODEL), lambda b, i, j: (layer, 0, 0))],
        out_specs=(pl.BlockSpec((1, tm, tn), lambda b, i, j: (b, i, j)),
                   pl.BlockSpec((1, A_WIDTH, tm), lambda b, i, j: (b, 0, i))),
        scratch_shapes=[pltpu.VMEM((tm, D_MODEL), jnp.bfloat16)],
        compiler_params=pltpu.CompilerParams(
            dimension_semantics=("parallel", "parallel", "arbitrary"),
            vmem_limit_bytes=VMEM_LIMIT),
        name="in_proj",
    )(x, shift, scale, gain, *tabs, w_main, w_vt)


def _diff_attn_kernel(q_ref, k_ref, vt_ref, dl_ref, gain_ref, o_ref,
                      qm_ref, vta_ref, sa_ref, sb_ref, m_ref, acc_ref, *, lambda_init, s_len):
    h = pl.program_id(1)
    tq = q_ref.shape[1]
    n_chunks = s_len // A_KC

    @pl.when(pl.program_id(2) == 0)
    def _():
        vta_ref[:A_VDIM, :] = vt_ref[0]
        row = lax.broadcasted_iota(jnp.int32, (A_VROWS - A_VDIM, s_len), 0)
        vta_ref[A_VDIM:, :] = jnp.where(row == 0, 1.0, 0.0).astype(vta_ref.dtype)

    q = q_ref[0]
    pair = _rope_layout_head(lax.broadcasted_iota(jnp.int32, q.shape, 1))
    base = 2 * (h % 2)
    for c in range(2):
        qsel = jnp.where(pair == base + c, q, jnp.zeros_like(q)).astype(jnp.float32)
        qm_ref[c] = (qsel * (HEAD_DIM ** -0.5 * LOG2_E)).astype(qm_ref.dtype)
        m_ref[c] = jnp.full((1, tq), MASK_VALUE, jnp.float32)
        acc_ref[c] = jnp.zeros((A_VROWS, tq), jnp.float32)

    def scores(i, s_ref):
        kc = k_ref[0, pl.ds(pl.multiple_of(i * A_KC, A_KC), A_KC), :]
        for c in range(2):
            s_ref[c] = lax.dot_general(kc, qm_ref[c], _NT,
                                       preferred_element_type=jnp.float32)

    def update(i, s_ref):
        vt_c = vta_ref[:, pl.ds(pl.multiple_of(i * A_KC, A_KC), A_KC)]
        for c in range(2):
            st = s_ref[c]
            m_old = m_ref[c]
            m_new = jnp.maximum(m_old, jnp.max(st, axis=0, keepdims=True))
            alpha = jnp.exp2(m_old - m_new)
            e = jnp.exp2(st - m_new).astype(jnp.bfloat16)
            pv = jnp.dot(vt_c, e, preferred_element_type=jnp.float32)
            acc_ref[c] = alpha * acc_ref[c] + pv
            m_ref[c] = m_new

    scores(0, sa_ref)

    def body(j, carry):
        scores(2 * j + 1, sb_ref)
        update(2 * j, sa_ref)
        scores(2 * j + 2, sa_ref)
        update(2 * j + 1, sb_ref)
        return carry

    lax.fori_loop(0, n_chunks // 2 - 1, body, 0)
    scores(n_chunks - 1, sb_ref)
    update(n_chunks - 2, sa_ref)
    update(n_chunks - 1, sb_ref)

    dl = dl_ref[...]
    lam = (jnp.exp(jnp.sum(dl[0:1] * dl[1:2], axis=1, keepdims=True))
           - jnp.exp(jnp.sum(dl[2:3] * dl[3:4], axis=1, keepdims=True)) + lambda_init)
    maps = [acc_ref[c, :A_VDIM, :] * (1.0 / acc_ref[c, A_VDIM:A_VDIM + 1, :]) for c in range(2)]
    ot = maps[0] - lam * maps[1]
    ms = jnp.mean(ot * ot, axis=0, keepdims=True)
    y = ot * lax.rsqrt(ms + RMS_EPS) * gain_ref[...] * (1.0 - lambda_init)
    o_ref[0] = y.T.astype(o_ref.dtype)


def _diff_attention(proj, vt, diff_lambda, subln_gain, lambda_init):
    bn, s_len, _ = proj.shape
    tq = A_TQ
    qblk = COL_AQ // HEAD_BLOCK
    kblk = COL_AK // HEAD_BLOCK
    assert s_len % (2 * A_KC) == 0
    return pl.pallas_call(
        functools.partial(_diff_attn_kernel, lambda_init=lambda_init, s_len=s_len),
        out_shape=jax.ShapeDtypeStruct((bn, s_len, A_WIDTH), jnp.bfloat16),
        grid=(bn, A_HEADS, s_len // tq),
        in_specs=[pl.BlockSpec((1, tq, HEAD_BLOCK), lambda b, h, i: (b, i, qblk + h // 2)),
                  pl.BlockSpec((1, s_len, HEAD_BLOCK), lambda b, h, i: (b, 0, kblk + h // 2)),
                  pl.BlockSpec((1, A_VDIM, s_len), lambda b, h, i: (b, h, 0)),
                  pl.BlockSpec((4, HEAD_DIM), lambda b, h, i: (0, 0)),
                  pl.BlockSpec((A_VDIM, 1), lambda b, h, i: (0, 0))],
        out_specs=pl.BlockSpec((1, tq, A_VDIM), lambda b, h, i: (b, i, h)),
        scratch_shapes=[pltpu.VMEM((2, tq, HEAD_BLOCK), jnp.bfloat16),
                        pltpu.VMEM((A_VROWS, s_len), jnp.bfloat16),
                        pltpu.VMEM((2, A_KC, tq), jnp.float32),
                        pltpu.VMEM((2, A_KC, tq), jnp.float32),
                        pltpu.VMEM((2, 1, tq), jnp.float32),
                        pltpu.VMEM((2, A_VROWS, tq), jnp.float32)],
        compiler_params=pltpu.CompilerParams(
            dimension_semantics=("parallel", "parallel", "arbitrary"),
            vmem_limit_bytes=VMEM_LIMIT),
        name="diff_attn",
    )(proj, proj, vt, diff_lambda, subln_gain.reshape(A_VDIM, 1))


def _head_lane_masks(shape, rotary_layout=False):
    col = lax.broadcasted_iota(jnp.int32, shape, 1)
    head = _rope_layout_head(col) if rotary_layout else col // HEAD_DIM
    return [head == h for h in range(HEAD_BLOCK // HEAD_DIM)]


def _stack_heads(q, qmask):
    zero = jnp.zeros_like(q)
    return jnp.concatenate([jnp.where(mk, q, zero) for mk in qmask], axis=0) * (HEAD_DIM ** -0.5)


def _unstack_heads(x, qmask, tq):
    out = x[:tq]
    for h in range(1, len(qmask)):
        out = jnp.where(qmask[h], x[h * tq:(h + 1) * tq], out)
    return out


def _band_group(q_ref, k_ref, v_ref, y_ref, band_ref, stage_ref, tmp_ref, qd_ref, kd_ref, vd_ref,
                run_o_ref, run_l_ref, *, dilation, s_len, first, last):
    length = s_len // dilation
    tq = min(B_TQ_MAX, length)
    win = min(tq + 2 * B_REACH, length)
    unroll = min(B_UNROLL_MAX, s_len // tq)
    nblk = length // tq
    assert length % tq == 0 and (s_len // tq) % unroll == 0 and dilation in (1, 4, 16)
    assert not (last and dilation > 1)
    qmask = _head_lane_masks((tq, HEAD_BLOCK), rotary_layout=True)
    vmask = _head_lane_masks((tq, HEAD_BLOCK))

    if dilation > 1:
        for src, dst in ((q_ref, qd_ref), (k_ref, kd_ref), (v_ref, vd_ref)):
            for slab in range(HEAD_BLOCK // LANES):
                lanes = slice(slab * LANES, (slab + 1) * LANES)
                stage_ref[...] = src[0, :, lanes].astype(jnp.float32)
                for m4 in range(4):
                    quarter = stage_ref[pl.ds(m4, s_len // 4, stride=4), :]
                    if dilation == 4:
                        dst[m4 * length:(m4 + 1) * length, lanes] = quarter.astype(dst.dtype)
                    else:
                        tmp_ref[...] = quarter
                        for mm in range(4):
                            m = m4 + 4 * mm
                            dst[m * length:(m + 1) * length, lanes] = tmp_ref[
                                pl.ds(mm, length, stride=4), :].astype(dst.dtype)
        q_src, k_src, v_src = qd_ref, kd_ref, vd_ref
    else:
        q_src, k_src, v_src = q_ref.at[0], k_ref.at[0], v_ref.at[0]

    def one_block(t):
        m = t // nblk
        q0 = pl.multiple_of((t % nblk) * tq, tq)
        ks = pl.multiple_of(jnp.clip(q0 - B_REACH, 0, length - win), B_REACH)
        base = pl.multiple_of(m * length, tq)
        if dilation > 1:
            rows = pl.ds(q0 * dilation + m, tq, stride=dilation)
        else:
            rows = pl.ds(q0, tq)
        q = q_src[pl.ds(base + q0, tq), :]
        kw = k_src[pl.ds(base + ks, win), :]
        vw = v_src[pl.ds(base + ks, win), :]
        qs = _stack_heads(q, qmask)
        s = lax.dot_general(qs, kw, _NT, preferred_element_type=jnp.float32)
        s = s + band_ref[(q0 - ks) // B_REACH]
        mx = jnp.max(s, axis=-1, keepdims=True)
        e = jnp.exp(s - mx)
        den = jnp.sum(e, axis=-1, keepdims=True)
        o = jnp.dot(e.astype(jnp.bfloat16), vw, preferred_element_type=jnp.float32)
        o = o * (1.0 / den)
        lse = jnp.broadcast_to(mx + jnp.log(den), o.shape)
        o_new = _unstack_heads(o, vmask, tq)
        l_new = _unstack_heads(lse, vmask, tq)
        for slab in range(HEAD_BLOCK // LANES):
            lanes = slice(slab * LANES, (slab + 1) * LANES)
            o_s, l_s = o_new[:, lanes], l_new[:, lanes]
            if not first:
                o_run, l_run = run_o_ref[slab, rows, :], run_l_ref[slab, rows, :]
                l_max = jnp.maximum(l_run, l_s)
                w_run, w_new = jnp.exp(l_run - l_max), jnp.exp(l_s - l_max)
                total = w_run + w_new
                o_s = (o_run * w_run + o_s * w_new) * (1.0 / total)
                l_s = l_max + jnp.log(total)
            if last:
                y_ref[0, rows, lanes] = o_s.astype(y_ref.dtype)
            else:
                run_o_ref[slab, rows, :] = o_s
                run_l_ref[slab, rows, :] = l_s

    def body(i, carry):
        for u in range(unroll):
            one_block(i * unroll + u)
        return carry

    lax.fori_loop(0, dilation * nblk // unroll, body, 0)


def _band_mix_kernel(q_ref, k_ref, v_ref, y_ref, band_ref, *scratch, s_len, dilations):
    g = pl.program_id(1)

    tq = band_ref.shape[1] // B_HEADS
    row = lax.broadcasted_iota(jnp.int32, band_ref.shape[1:], 0) % tq
    col = lax.broadcasted_iota(jnp.int32, band_ref.shape[1:], 1)
    for case in range(B_CASES):
        valid = jnp.abs(col - case * B_REACH - row) <= B_REACH
        band_ref[case] = jnp.where(valid, 0.0, MASK_VALUE)

    for step, dilation in enumerate(dilations):
        @pl.when(g == step)
        def _(step=step, dilation=dilation):
            _band_group(q_ref, k_ref, v_ref, y_ref, band_ref, *scratch, dilation=dilation,
                        s_len=s_len, first=(step == 0), last=(step == len(dilations) - 1))


def _band_mixture(proj):
    bn, s_len, _ = proj.shape
    n_groups = len(B_PATTERNS)
    order = sorted(range(n_groups), key=lambda g: -B_PATTERNS[g][1])
    dilations = tuple(B_PATTERNS[g][1] for g in order)
    for window, dilation in B_PATTERNS:
        assert (window // 2) // dilation == B_REACH
    assert order == list(range(n_groups - 1, -1, -1))

    def in_spec(col0):
        blk = col0 // HEAD_BLOCK
        return pl.BlockSpec((1, s_len, HEAD_BLOCK), lambda b, g: (b, 0, blk + n_groups - 1 - g))

    slabs = HEAD_BLOCK // LANES
    regrouped = pltpu.VMEM((s_len, HEAD_BLOCK), jnp.bfloat16)
    state = pltpu.VMEM((slabs, s_len, LANES), jnp.float32)
    return pl.pallas_call(
        functools.partial(_band_mix_kernel, s_len=s_len, dilations=dilations),
        out_shape=jax.ShapeDtypeStruct((bn, s_len, B_WIDTH), jnp.bfloat16),
        grid=(bn, n_groups),
        in_specs=[in_spec(COL_BQ), in_spec(COL_BK), in_spec(COL_BV)],
        out_specs=pl.BlockSpec((1, s_len, B_WIDTH), lambda b, g: (b, 0, 0)),
        scratch_shapes=[pltpu.VMEM((B_CASES, B_HEADS * B_TQ_MAX, B_TQ_MAX + 2 * B_REACH), jnp.float32),
                        pltpu.VMEM((s_len, LANES), jnp.float32),
                        pltpu.VMEM((s_len // 4, LANES), jnp.float32),
                        regrouped, regrouped, regrouped,
                        state, state],
        compiler_params=pltpu.CompilerParams(
            dimension_semantics=("parallel", "arbitrary"),
            vmem_limit_bytes=VMEM_LIMIT),
        name="band_mix",
    )(proj, proj, proj)


def _na_table_kernel(rpb_ref, tab_ref, *, rows):
    h = pl.program_id(0)
    n_dr = 2 * NA_KH - 1
    n_dc = 2 * NA_KW - 1
    shape = (GRID_W, LANES)
    qc = lax.broadcasted_iota(jnp.int32, shape, 0)
    lane = lax.broadcasted_iota(jnp.int32, shape, 1)
    kc = lane % GRID_W
    dc_i = jnp.clip(kc - qc + (NA_KW - 1), 0, n_dc - 1)
    cs = jnp.clip(qc - NA_KW // 2, 0, GRID_W - NA_KW)
    col_valid = (kc >= cs) & (kc < cs + NA_KW)
    masked = jnp.full(shape, MASK_VALUE, jnp.float32)

    row_bias = []
    for dr in range(n_dr):
        base = (h * n_dr + dr) * n_dc
        t = masked
        for dc in range(n_dc):
            t = jnp.where(dc_i == dc, rpb_ref[base + dc], t)
        row_bias.append(jnp.where(col_valid, t, MASK_VALUE))

    pair_rep = (0, 1, 2, rows // 2 - 2, rows // 2 - 1)
    for case in range(C_CASES):
        ip = pair_rep[case]
        ws = min(max(2 * ip - NA_KH // 2, 0), rows - C_KROWS)
        for qr_l in range(2):
            r = 2 * ip + qr_l
            rs = min(max(r - NA_KH // 2, 0), rows - NA_KH)
            for tile in range(C_WIN // LANES):
                halves = []
                for kr in (ws + 2 * tile, ws + 2 * tile + 1):
                    halves.append(row_bias[kr - r + NA_KH - 1] if rs <= kr < rs + NA_KH else masked)
                tab_ref[case, 0, qr_l * GRID_W:(qr_l + 1) * GRID_W, tile * LANES:(tile + 1) * LANES] = (
                    jnp.where(lane < GRID_W, halves[0], halves[1]))


def _na_tables(rpb, rows):
    flat = rpb.reshape(-1)
    return pl.pallas_call(
        functools.partial(_na_table_kernel, rows=rows),
        out_shape=jax.ShapeDtypeStruct((C_CASES, C_HEADS, C_PAIR, C_WIN), jnp.float32),
        grid=(C_HEADS,),
        in_specs=[pl.BlockSpec(memory_space=pltpu.SMEM)],
        out_specs=pl.BlockSpec((C_CASES, 1, C_PAIR, C_WIN), lambda h: (0, h, 0, 0)),
        name="na_tables",
    )(flat)


def _na_kernel(q_ref, k_ref, v_ref, tab_ref, o_ref, *, rows):
    rb = pl.program_id(1)
    qmask = _head_lane_masks((C_PAIR, HEAD_BLOCK))
    pairs_per_step = C_RB // 2
    for t in range(pairs_per_step):
        ip = rb * pairs_per_step + t
        ws = jnp.clip(2 * ip - NA_KH // 2, 0, rows - C_KROWS)
        case = ip - ws // 2
        k0 = pl.multiple_of(ws * GRID_W, C_PAIR)
        q = q_ref[0, t * C_PAIR:(t + 1) * C_PAIR, :]
        kw = k_ref[0, pl.ds(k0, C_WIN), :]
        vw = v_ref[0, pl.ds(k0, C_WIN), :]
        qs = _stack_heads(q, qmask)
        s = lax.dot_general(qs, kw, _NT, preferred_element_type=jnp.float32)
        s = s + tab_ref[case]
        mx = jnp.max(s, axis=-1, keepdims=True)
        e = jnp.exp(s - mx)
        den = jnp.sum(e, axis=-1, keepdims=True)
        o = jnp.dot(e.astype(jnp.bfloat16), vw, preferred_element_type=jnp.float32)
        o_acc = _unstack_heads(o * (1.0 / den), qmask, C_PAIR)
        o_ref[0, t * C_PAIR:(t + 1) * C_PAIR, :] = o_acc.astype(o_ref.dtype)


def _neighborhood_attention(proj, tables):
    bn, s_len, _ = proj.shape
    rows = s_len // GRID_W
    tq = C_RB * GRID_W
    qblk = COL_CQ // HEAD_BLOCK
    kblk = COL_CK // HEAD_BLOCK
    vblk = COL_CV // HEAD_BLOCK
    return pl.pallas_call(
        functools.partial(_na_kernel, rows=rows),
        out_shape=jax.ShapeDtypeStruct((bn, s_len, C_WIDTH), jnp.bfloat16),
        grid=(bn, rows // C_RB),
        in_specs=[pl.BlockSpec((1, tq, HEAD_BLOCK), lambda b, i: (b, i, qblk)),
                  pl.BlockSpec((1, s_len, HEAD_BLOCK), lambda b, i: (b, 0, kblk)),
                  pl.BlockSpec((1, s_len, HEAD_BLOCK), lambda b, i: (b, 0, vblk)),
                  pl.BlockSpec((C_CASES, C_HEADS * C_PAIR, C_WIN), lambda b, i: (0, 0, 0))],
        out_specs=pl.BlockSpec((1, tq, C_WIDTH), lambda b, i: (b, i, 0)),
        compiler_params=pltpu.CompilerParams(
            dimension_semantics=("parallel", "arbitrary"),
            vmem_limit_bytes=VMEM_LIMIT),
        name="na_attn",
    )(proj, proj, proj, tables.reshape(C_CASES, C_HEADS * C_PAIR, C_WIN))


def _merge_kernel(x_ref, gate_ref, ya_ref, yb_ref, yc_ref, z_ref, g0_ref, g1_ref, g2_ref,
                  wb_ref, wo_ref, fg_ref, o_ref, *, final_norm):
    z = z_ref[0].astype(jnp.float32)
    sz = z * _sigmoid(z)
    ya = (ya_ref[0].astype(jnp.float32) * sz[:, :A_WIDTH]).astype(jnp.bfloat16)
    ybz = (yb_ref[0].astype(jnp.float32) * sz[:, A_WIDTH:A_WIDTH + B_WIDTH]).astype(jnp.bfloat16)
    ycz = (yc_ref[0].astype(jnp.float32) * sz[:, A_WIDTH + B_WIDTH:]).astype(jnp.bfloat16)

    pa = jnp.dot(ya, wb_ref[:A_WIDTH, :], preferred_element_type=jnp.float32)
    pb = jnp.dot(ybz, wb_ref[A_WIDTH:A_WIDTH + B_WIDTH, :], preferred_element_type=jnp.float32)
    pc = jnp.dot(ycz, wb_ref[A_WIDTH + B_WIDTH:, :], preferred_element_type=jnp.float32)
    merged = _sigmoid(g0_ref[0].astype(jnp.float32)) * pa
    merged = merged + _sigmoid(g1_ref[0].astype(jnp.float32)) * pb
    merged = merged + _sigmoid(g2_ref[0].astype(jnp.float32)) * pc
    out = jnp.dot(merged.astype(jnp.bfloat16), wo_ref[...], preferred_element_type=jnp.float32)
    xn = x_ref[0] + gate_ref[0] * out
    if final_norm:
        ms = jnp.mean(xn * xn, axis=-1, keepdims=True)
        xn = xn * lax.rsqrt(ms + RMS_EPS) * fg_ref[...]
    o_ref[0] = xn


def _merge(x, gate, ya, yb, yc, proj, w_branch, w_out, final_gain, layer, final_norm):
    bn, s_len, _ = x.shape
    tm = MERGE_TM
    zblk = COL_Z // D_MODEL
    gblk = COL_G // D_MODEL

    def tok(width, blk=0):
        return pl.BlockSpec((1, tm, width), lambda b, i: (b, i, blk))

    return pl.pallas_call(
        functools.partial(_merge_kernel, final_norm=final_norm),
        out_shape=jax.ShapeDtypeStruct((bn, s_len, D_MODEL), jnp.float32),
        grid=(bn, s_len // tm),
        in_specs=[tok(D_MODEL),
                  pl.BlockSpec((1, 1, D_MODEL), lambda b, i: (b, 0, 0)),
                  tok(A_WIDTH), tok(B_WIDTH), tok(C_WIDTH),
                  tok(D_MODEL, zblk), tok(D_MODEL, gblk), tok(D_MODEL, gblk + 1), tok(D_MODEL, gblk + 2),
                  pl.BlockSpec((None, BR_WIDTH, D_MODEL), lambda b, i: (layer, 0, 0)),
                  pl.BlockSpec((None, D_MODEL, D_MODEL), lambda b, i: (layer, 0, 0)),
                  pl.BlockSpec((1, D_MODEL), lambda b, i: (0, 0))],
        out_specs=tok(D_MODEL),
        compiler_params=pltpu.CompilerParams(
            dimension_semantics=("parallel", "parallel"),
            vmem_limit_bytes=VMEM_LIMIT),
        name="merge",
    )(x, gate, ya, yb, yc, proj, proj, proj, proj,
      w_branch, w_out, final_gain)


def _w_main_kernel(w_ref, o_ref):
    d = pl.program_id(1)
    w = w_ref[...].astype(jnp.bfloat16)

    @pl.when(d < ROPE_COLS // W_PREP_COLS)
    def _():
        src = lax.broadcasted_iota(jnp.int32, (HEAD_BLOCK, HEAD_BLOCK), 0)
        dst = lax.broadcasted_iota(jnp.int32, (HEAD_BLOCK, HEAD_BLOCK), 1)
        onehot = jnp.where(src == _rope_layout_source(dst), 1.0, 0.0).astype(jnp.bfloat16)
        for blk in range(W_PREP_COLS // HEAD_BLOCK):
            cols = slice(blk * HEAD_BLOCK, (blk + 1) * HEAD_BLOCK)
            o_ref[:, cols] = jnp.dot(w[:, cols], onehot,
                                     preferred_element_type=jnp.float32).astype(o_ref.dtype)

    @pl.when(d >= ROPE_COLS // W_PREP_COLS)
    def _():
        o_ref[...] = w


def _w_vt_kernel(w_ref, o_ref):
    w = w_ref[...].astype(jnp.bfloat16)
    r = lax.broadcasted_iota(jnp.int32, (A_WIDTH, A_WIDTH), 0)
    c = lax.broadcasted_iota(jnp.int32, (A_WIDTH, A_WIDTH), 1)
    eye = jnp.where(r == c, 1.0, 0.0).astype(jnp.bfloat16)
    o_ref[...] = lax.dot_general(eye, w, _NT, preferred_element_type=jnp.float32).astype(o_ref.dtype)


def _split_w_in(w_in):
    depth = w_in.shape[0]
    av_blk = 2 * A_QK_COLS // W_PREP_COLS
    assert 2 * A_QK_COLS % W_PREP_COLS == 0 and A_WIDTH == W_PREP_COLS and ROPE_COLS % W_PREP_COLS == 0
    w_main = pl.pallas_call(
        _w_main_kernel,
        out_shape=jax.ShapeDtypeStruct((depth, D_MODEL, MAIN_COLS), jnp.bfloat16),
        grid=(depth, MAIN_COLS // W_PREP_COLS),
        in_specs=[pl.BlockSpec((None, D_MODEL, W_PREP_COLS),
                               lambda l, d: (l, 0, jnp.where(d < av_blk, d, d + 1)))],
        out_specs=pl.BlockSpec((None, D_MODEL, W_PREP_COLS), lambda l, d: (l, 0, d)),
        name="w_main_prep",
    )(w_in)
    w_vt = pl.pallas_call(
        _w_vt_kernel,
        out_shape=jax.ShapeDtypeStruct((depth, A_WIDTH, D_MODEL), jnp.bfloat16),
        grid=(depth,),
        in_specs=[pl.BlockSpec((None, D_MODEL, A_WIDTH), lambda l: (l, 0, av_blk))],
        out_specs=pl.BlockSpec((None, A_WIDTH, D_MODEL), lambda l: (l, 0, 0)),
        name="w_vt_prep",
    )(w_in)
    return w_main, w_vt


def kernel(x, c, positions, norm_gain, w_ada, b_ada, w_in, diff_lambda, diff_subln_gain, na_rpb,
           w_branch, w_out, final_gain):
    depth = w_in.shape[0]
    bn, s_len, _ = x.shape
    rows = s_len // GRID_W
    tabs = _rope_tables(positions)
    ada = _ada_all_layers(c, w_ada, b_ada)
    fg = final_gain.reshape(1, D_MODEL)
    w_main, w_vt = _split_w_in(w_in)
    w_branch = w_branch.astype(jnp.bfloat16)
    w_out = w_out.astype(jnp.bfloat16)
    for layer in range(depth):
        shift = ada[layer, :, None, :D_MODEL]
        scale = ada[layer, :, None, D_MODEL:2 * D_MODEL]
        gate = ada[layer, :, None, 2 * D_MODEL:]
        proj, vt = _in_proj(x, shift, scale, norm_gain[layer].reshape(1, D_MODEL), tabs, w_main, w_vt,
                            layer)
        lambda_init = 0.8 - 0.6 * math.exp(-0.3 * layer)
        ya = _diff_attention(proj, vt, diff_lambda[layer], diff_subln_gain[layer], lambda_init)
        yb = _band_mixture(proj)
        yc = _neighborhood_attention(proj, _na_tables(na_rpb[layer], rows))
        x = _merge(x, gate, ya, yb, yc, proj, w_branch, w_out, fg, layer,
                   final_norm=(layer == depth - 1))
    return x
```

```python
import functools
import math

import numpy as np
import jax
import jax.numpy as jnp
from jax import lax
from jax.experimental import pallas as pl
from jax.experimental.pallas import tpu as pltpu

D_MODEL = 1024
HEAD_DIM = 64
ROT_DIM = HEAD_DIM // 4
ROT_HALF = ROT_DIM // 2
ROPE_THETA = 500000.0
RMS_EPS = 1e-6
MASK_VALUE = -1e30
LOG2_E = math.log2(math.e)

A_HEADS = 4
A_VDIM = 2 * HEAD_DIM
A_WIDTH = A_HEADS * A_VDIM
A_QK_COLS = A_HEADS * 2 * HEAD_DIM

B_PATTERNS = ((128, 1), (512, 4), (2048, 16))
B_GROUPS = 3
B_HEADS = 4
B_WIDTH = B_HEADS * HEAD_DIM
B_COLS = B_GROUPS * B_WIDTH

GRID_W = 64
C_HEADS = 4
NA_KH = 8
NA_KW = 16
C_WIDTH = C_HEADS * HEAD_DIM

N_BRANCHES = 3
BR_WIDTH = A_WIDTH + B_WIDTH + C_WIDTH

LANES = 128
HEAD_BLOCK = 4 * HEAD_DIM
ROPE_LANES = (HEAD_BLOCK // HEAD_DIM) * ROT_HALF
VMEM_LIMIT = 56 * 1024 * 1024

COL_AQ = 0
COL_AK = COL_AQ + A_QK_COLS
COL_BQ = COL_AK + A_QK_COLS
COL_BK = COL_BQ + B_COLS
COL_BV = COL_BK + B_COLS
COL_CQ = COL_BV + B_COLS
COL_CK = COL_CQ + C_WIDTH
COL_CV = COL_CK + C_WIDTH
COL_Z = COL_CV + C_WIDTH
COL_G = COL_Z + BR_WIDTH
MAIN_COLS = COL_G + N_BRANCHES * D_MODEL
ROPE_COLS = COL_BV

W_PREP_COLS = 512
PROJ_TM = 2048
PROJ_TN = 1024
A_TQ = 2048
A_KC = 512
A_VROWS = A_VDIM + 16
B_TQ_MAX = 128
B_UNROLL_MAX = 8
B_REACH = 64
B_CASES = 3
C_PAIR = 2 * GRID_W
C_KROWS = 10
C_WIN = C_KROWS * GRID_W
C_CASES = 5
C_RB = 8
MERGE_TM = 512

_NT = (((1,), (1,)), ((), ()))


def _sigmoid(x):
    return 1.0 / (1.0 + jnp.exp2(x * (-LOG2_E)))


def _rope_layout_head(col):
    l = col % LANES
    c = col // LANES
    rest = HEAD_DIM - ROT_DIM
    return jnp.where(l < ROPE_LANES, l // ROT_HALF, 2 * c + jnp.where(l >= ROPE_LANES + rest, 1, 0))


def _rope_layout_source(col):
    l = col % LANES
    c = col // LANES
    rest = HEAD_DIM - ROT_DIM
    r = l - ROPE_LANES
    upper = jnp.where(r >= rest, 1, 0)
    rotary_src = HEAD_DIM * (l // ROT_HALF) + ROT_HALF * c + l % ROT_HALF
    other_src = HEAD_DIM * (2 * c + upper) + ROT_DIM + r - rest * upper
    return jnp.where(l < ROPE_LANES, rotary_src, other_src)


def _rope_table_kernel(pos_ref, inv_ref, c_ref, s_ref):
    ang = pos_ref[0] * inv_ref[...]
    lane = lax.broadcasted_iota(jnp.int32, ang.shape, 1)
    c_ref[0] = jnp.where(lane < ROPE_LANES, jnp.cos(ang), 1.0)
    s_ref[0] = jnp.where(lane < ROPE_LANES, jnp.sin(ang), 0.0)


def _rope_tables(positions):
    bn, s_len = positions.shape
    inv = np.float32(ROPE_THETA) ** (-np.arange(0, ROT_DIM, 2, dtype=np.float32) / np.float32(ROT_DIM))
    inv_lane = np.zeros((1, LANES), np.float32)
    inv_lane[0, :ROPE_LANES] = np.tile(inv, ROPE_LANES // ROT_HALF)
    pos = positions.astype(jnp.float32)[..., None]
    ts = 1024
    tab = jax.ShapeDtypeStruct((bn, s_len, LANES), jnp.float32)
    spec = pl.BlockSpec((1, ts, LANES), lambda b, i: (b, i, 0))
    return pl.pallas_call(
        _rope_table_kernel,
        out_shape=(tab, tab),
        grid=(bn, s_len // ts),
        in_specs=[pl.BlockSpec((1, ts, 1), lambda b, i: (b, i, 0)),
                  pl.BlockSpec((1, LANES), lambda b, i: (0, 0))],
        out_specs=(spec, spec),
        name="rope_tables",
    )(pos, jnp.asarray(inv_lane))


def _ada_kernel(c_ref, w_ref, b_ref, o_ref):
    c = c_ref[...]
    c_act = c * _sigmoid(c)
    o_ref[0] = jnp.dot(c_act, w_ref[0], precision=lax.Precision.HIGHEST,
                       preferred_element_type=jnp.float32) + b_ref[0]


def _ada_all_layers(c, w_ada, b_ada):
    depth = w_ada.shape[0]
    bn = c.shape[0]
    rows = 8
    c_pad = jnp.pad(c, ((0, rows - bn), (0, 0)))
    tn = 1024
    out = pl.pallas_call(
        _ada_kernel,
        out_shape=jax.ShapeDtypeStruct((depth, rows, 3 * D_MODEL), jnp.float32),
        grid=(depth, 3 * D_MODEL // tn),
        in_specs=[pl.BlockSpec((rows, D_MODEL), lambda l, j: (0, 0)),
                  pl.BlockSpec((1, D_MODEL, tn), lambda l, j: (l, 0, j)),
                  pl.BlockSpec((1, 1, tn), lambda l, j: (l, 0, j))],
        out_specs=pl.BlockSpec((1, rows, tn), lambda l, j: (l, 0, j)),
        name="adaln",
    )(c_pad, w_ada, b_ada.reshape(depth, 1, 3 * D_MODEL))
    return out[:, :bn]


def _in_proj_kernel(x_ref, shift_ref, scale_ref, gain_ref, c_ref, s_ref, w_ref, wvt_ref,
                    proj_ref, vt_ref, h_ref):
    j = pl.program_id(2)

    @pl.when(j == 0)
    def _():
        xf = x_ref[0]
        ms = jnp.mean(xf * xf, axis=-1, keepdims=True)
        y = xf * lax.rsqrt(ms + RMS_EPS) * gain_ref[...]
        h = (y * (1.0 + scale_ref[0]) + shift_ref[0]).astype(jnp.bfloat16)
        h_ref[...] = h
        vt = lax.dot_general(wvt_ref[...], h, _NT, preferred_element_type=jnp.float32)
        vt_ref[0] = vt.astype(vt_ref.dtype)

    acc = jnp.dot(h_ref[...], w_ref[...], preferred_element_type=jnp.float32)
    blocks_per_tile = PROJ_TN // HEAD_BLOCK
    cc = c_ref[0]
    ss = s_ref[0]
    for blk in range(blocks_per_tile):
        rotary = (j * blocks_per_tile + blk) < (ROPE_COLS // HEAD_BLOCK)
        cb = jnp.where(rotary, cc, 1.0)
        sb = jnp.where(rotary, ss, 0.0)
        lo0 = blk * HEAD_BLOCK
        hi0 = lo0 + LANES
        lo = acc[:, lo0:hi0]
        hi = acc[:, hi0:hi0 + LANES]
        proj_ref[0, :, lo0:hi0] = (lo * cb - hi * sb).astype(proj_ref.dtype)
        proj_ref[0, :, hi0:hi0 + LANES] = (hi * cb + lo * sb).astype(proj_ref.dtype)


def _in_proj(x, shift, scale, gain, tabs, w_main, w_vt, layer):
    bn, s_len, _ = x.shape
    tm, tn = PROJ_TM, PROJ_TN
    tab_spec = pl.BlockSpec((1, tm, LANES), lambda b, i, j: (b, i, 0))
    mod_spec = pl.BlockSpec((1, 1, D_MODEL), lambda b, i, j: (b, 0, 0))
    return pl.pallas_call(
        _in_proj_kernel,
        out_shape=(jax.ShapeDtypeStruct((bn, s_len, MAIN_COLS), jnp.bfloat16),
                   jax.ShapeDtypeStruct((bn, A_WIDTH, s_len), jnp.bfloat16)),
        grid=(bn, s_len // tm, MAIN_COLS // tn),
        in_specs=[pl.BlockSpec((1, tm, D_MODEL), lambda b, i, j: (b, i, 0)),
                  mod_spec, mod_spec,
                  pl.BlockSpec((1, D_MODEL), lambda b, i, j: (0, 0)),
                  tab_spec, tab_spec,
                  pl.BlockSpec((None, D_MODEL, tn), lambda b, i, j: (layer, 0, j)),
                  pl.BlockSpec((None, A_WIDTH, D_MODEL), lambda b, i, j: (layer, 0, 0))],
        out_specs=(pl.BlockSpec((1, tm, tn), lambda b, i, j: (b, i, j)),
                   pl.BlockSpec((1, A_WIDTH, tm), lambda b, i, j: (b, 0, i))),
        scratch_shapes=[pltpu.VMEM((tm, D_MODEL), jnp.bfloat16)],
        compiler_params=pltpu.CompilerParams(
            dimension_semantics=("parallel", "parallel", "arbitrary"),
            vmem_limit_bytes=VMEM_LIMIT),
        name="in_proj",
    )(x, shift, scale, gain, *tabs, w_main, w_vt)


def _diff_attn_kernel(q_ref, k_ref, vt_ref, dl_ref, gain_ref, o_ref,
                      qm_ref, vta_ref, sa_ref, sb_ref, m_ref, acc_ref, *, lambda_init, s_len):
    h = pl.program_id(1)
    tq = q_ref.shape[1]
    n_chunks = s_len // A_KC

    @pl.when(pl.program_id(2) == 0)
    def _():
        vta_ref[:A_VDIM, :] = vt_ref[0]
        row = lax.broadcasted_iota(jnp.int32, (A_VROWS - A_VDIM, s_len), 0)
        vta_ref[A_VDIM:, :] = jnp.where(row == 0, 1.0, 0.0).astype(vta_ref.dtype)

    q = q_ref[0]
    pair = _rope_layout_head(lax.broadcasted_iota(jnp.int32, q.shape, 1))
    base = 2 * (h % 2)
    for c in range(2):
        qsel = jnp.where(pair == base + c, q, jnp.zeros_like(q)).astype(jnp.float32)
        qm_ref[c] = (qsel * (HEAD_DIM ** -0.5 * LOG2_E)).astype(qm_ref.dtype)
        m_ref[c] = jnp.full((1, tq), MASK_VALUE, jnp.float32)
        acc_ref[c] = jnp.zeros((A_VROWS, tq), jnp.float32)

    def scores(i, s_ref):
        kc = k_ref[0, pl.ds(pl.multiple_of(i * A_KC, A_KC), A_KC), :]
        for c in range(2):
            s_ref[c] = lax.dot_general(kc, qm_ref[c], _NT,
                                       preferred_element_type=jnp.float32)

    def update(i, s_ref):
        vt_c = vta_ref[:, pl.ds(pl.multiple_of(i * A_KC, A_KC), A_KC)]
        for c in range(2):
            st = s_ref[c]
            m_old = m_ref[c]
            m_new = jnp.maximum(m_old, jnp.max(st, axis=0, keepdims=True))
            alpha = jnp.exp2(m_old - m_new)
            e = jnp.exp2(st - m_new).astype(jnp.bfloat16)
            pv = jnp.dot(vt_c, e, preferred_element_type=jnp.float32)
            acc_ref[c] = alpha * acc_ref[c] + pv
            m_ref[c] = m_new

    scores(0, sa_ref)

    def body(j, carry):
        scores(2 * j + 1, sb_ref)
        update(2 * j, sa_ref)
        scores(2 * j + 2, sa_ref)
        update(2 * j + 1, sb_ref)
        return carry

    lax.fori_loop(0, n_chunks // 2 - 1, body, 0)
    scores(n_chunks - 1, sb_ref)
    update(n_chunks - 2, sa_ref)
    update(n_chunks - 1, sb_ref)

    dl = dl_ref[...]
    lam = (jnp.exp(jnp.sum(dl[0:1] * dl[1:2], axis=1, keepdims=True))
           - jnp.exp(jnp.sum(dl[2:3] * dl[3:4], axis=1, keepdims=True)) + lambda_init)
    maps = [acc_ref[c, :A_VDIM, :] * (1.0 / acc_ref[c, A_VDIM:A_VDIM + 1, :]) for c in range(2)]
    ot = maps[0] - lam * maps[1]
    ms = jnp.mean(ot * ot, axis=0, keepdims=True)
    y = ot * lax.rsqrt(ms + RMS_EPS) * gain_ref[...] * (1.0 - lambda_init)
    o_ref[0] = y.T.astype(o_ref.dtype)


def _diff_attention(proj, vt, diff_lambda, subln_gain, lambda_init):
    bn, s_len, _ = proj.shape
    tq = A_TQ
    qblk = COL_AQ // HEAD_BLOCK
    kblk = COL_AK // HEAD_BLOCK
    assert s_len % (2 * A_KC) == 0
    return pl.pallas_call(
        functools.partial(_diff_attn_kernel, lambda_init=lambda_init, s_len=s_len),
        out_shape=jax.ShapeDtypeStruct((bn, s_len, A_WIDTH), jnp.bfloat16),
        grid=(bn, A_HEADS, s_len // tq),
        in_specs=[pl.BlockSpec((1, tq, HEAD_BLOCK), lambda b, h, i: (b, i, qblk + h // 2)),
                  pl.BlockSpec((1, s_len, HEAD_BLOCK), lambda b, h, i: (b, 0, kblk + h // 2)),
                  pl.BlockSpec((1, A_VDIM, s_len), lambda b, h, i: (b, h, 0)),
                  pl.BlockSpec((4, HEAD_DIM), lambda b, h, i: (0, 0)),
                  pl.BlockSpec((A_VDIM, 1), lambda b, h, i: (0, 0))],
        out_specs=pl.BlockSpec((1, tq, A_VDIM), lambda b, h, i: (b, i, h)),
        scratch_shapes=[pltpu.VMEM((2, tq, HEAD_BLOCK), jnp.bfloat16),
                        pltpu.VMEM((A_VROWS, s_len), jnp.bfloat16),
                        pltpu.VMEM((2, A_KC, tq), jnp.float32),
                        pltpu.VMEM((2, A_KC, tq), jnp.float32),
                        pltpu.VMEM((2, 1, tq), jnp.float32),
                        pltpu.VMEM((2, A_VROWS, tq), jnp.float32)],
        compiler_params=pltpu.CompilerParams(
            dimension_semantics=("parallel", "parallel", "arbitrary"),
            vmem_limit_bytes=VMEM_LIMIT),
        name="diff_attn",
    )(proj, proj, vt, diff_lambda, subln_gain.reshape(A_VDIM, 1))


def _head_lane_masks(shape, rotary_layout=False):
    col = lax.broadcasted_iota(jnp.int32, shape, 1)
    head = _rope_layout_head(col) if rotary_layout else col // HEAD_DIM
    return [head == h for h in range(HEAD_BLOCK // HEAD_DIM)]


def _stack_heads(q, qmask):
    zero = jnp.zeros_like(q)
    return jnp.concatenate([jnp.where(mk, q, zero) for mk in qmask], axis=0) * (HEAD_DIM ** -0.5)


def _unstack_heads(x, qmask, tq):
    out = x[:tq]
    for h in range(1, len(qmask)):
        out = jnp.where(qmask[h], x[h * tq:(h + 1) * tq], out)
    return out


def _band_group(q_ref, k_ref, v_ref, y_ref, band_ref, stage_ref, tmp_ref, qd_ref, kd_ref, vd_ref,
                run_o_ref, run_l_ref, *, dilation, s_len, first, last):
    length = s_len // dilation
    tq = min(B_TQ_MAX, length)
    win = min(tq + 2 * B_REACH, length)
    unroll = min(B_UNROLL_MAX, s_len // tq)
    nblk = length // tq
    assert length % tq == 0 and (s_len // tq) % unroll == 0 and dilation in (1, 4, 16)
    assert not (last and dilation > 1)
    qmask = _head_lane_masks((tq, HEAD_BLOCK), rotary_layout=True)
    vmask = _head_lane_masks((tq, HEAD_BLOCK))

    if dilation > 1:
        for src, dst in ((q_ref, qd_ref), (k_ref, kd_ref), (v_ref, vd_ref)):
            for slab in range(HEAD_BLOCK // LANES):
                lanes = slice(slab * LANES, (slab + 1) * LANES)
                stage_ref[...] = src[0, :, lanes].astype(jnp.float32)
                for m4 in range(4):
                    quarter = stage_ref[pl.ds(m4, s_len // 4, stride=4), :]
                    if dilation == 4:
                        dst[m4 * length:(m4 + 1) * length, lanes] = quarter.astype(dst.dtype)
                    else:
                        tmp_ref[...] = quarter
                        for mm in range(4):
                            m = m4 + 4 * mm
                            dst[m * length:(m + 1) * length, lanes] = tmp_ref[
                                pl.ds(mm, length, stride=4), :].astype(dst.dtype)
        q_src, k_src, v_src = qd_ref, kd_ref, vd_ref
    else:
        q_src, k_src, v_src = q_ref.at[0], k_ref.at[0], v_ref.at[0]

    def one_block(t):
        m = t // nblk
        q0 = pl.multiple_of((t % nblk) * tq, tq)
        ks = pl.multiple_of(jnp.clip(q0 - B_REACH, 0, length - win), B_REACH)
        base = pl.multiple_of(m * length, tq)
        if dilation > 1:
            rows = pl.ds(q0 * dilation + m, tq, stride=dilation)
        else:
            rows = pl.ds(q0, tq)
        q = q_src[pl.ds(base + q0, tq), :]
        kw = k_src[pl.ds(base + ks, win), :]
        vw = v_src[pl.ds(base + ks, win), :]
        qs = _stack_heads(q, qmask)
        s = lax.dot_general(qs, kw, _NT, preferred_element_type=jnp.float32)
        s = s + band_ref[(q0 - ks) // B_REACH]
        mx = jnp.max(s, axis=-1, keepdims=True)
        e = jnp.exp(s - mx)
        den = jnp.sum(e, axis=-1, keepdims=True)
        o = jnp.dot(e.astype(jnp.bfloat16), vw, preferred_element_type=jnp.float32)
        o = o * (1.0 / den)
        lse = jnp.broadcast_to(mx + jnp.log(den), o.shape)
        o_new = _unstack_heads(o, vmask, tq)
        l_new = _unstack_heads(lse, vmask, tq)
        for slab in range(HEAD_BLOCK // LANES):
            lanes = slice(slab * LANES, (slab + 1) * LANES)
            o_s, l_s = o_new[:, lanes], l_new[:, lanes]
            if not first:
                o_run, l_run = run_o_ref[slab, rows, :], run_l_ref[slab, rows, :]
                l_max = jnp.maximum(l_run, l_s)
                w_run, w_new = jnp.exp(l_run - l_max), jnp.exp(l_s - l_max)
                total = w_run + w_new
                o_s = (o_run * w_run + o_s * w_new) * (1.0 / total)
                l_s = l_max + jnp.log(total)
            if last:
                y_ref[0, rows, lanes] = o_s.astype(y_ref.dtype)
            else:
                run_o_ref[slab, rows, :] = o_s
                run_l_ref[slab, rows, :] = l_s

    def body(i, carry):
        for u in range(unroll):
            one_block(i * unroll + u)
        return carry

    lax.fori_loop(0, dilation * nblk // unroll, body, 0)


def _band_mix_kernel(q_ref, k_ref, v_ref, y_ref, band_ref, *scratch, s_len, dilations):
    g = pl.program_id(1)

    tq = band_ref.shape[1] // B_HEADS
    row = lax.broadcasted_iota(jnp.int32, band_ref.shape[1:], 0) % tq
    col = lax.broadcasted_iota(jnp.int32, band_ref.shape[1:], 1)
    for case in range(B_CASES):
        valid = jnp.abs(col - case * B_REACH - row) <= B_REACH
        band_ref[case] = jnp.where(valid, 0.0, MASK_VALUE)

    for step, dilation in enumerate(dilations):
        @pl.when(g == step)
        def _(step=step, dilation=dilation):
            _band_group(q_ref, k_ref, v_ref, y_ref, band_ref, *scratch, dilation=dilation,
                        s_len=s_len, first=(step == 0), last=(step == len(dilations) - 1))


def _band_mixture(proj):
    bn, s_len, _ = proj.shape
    n_groups = len(B_PATTERNS)
    order = sorted(range(n_groups), key=lambda g: -B_PATTERNS[g][1])
    dilations = tuple(B_PATTERNS[g][1] for g in order)
    for window, dilation in B_PATTERNS:
        assert (window // 2) // dilation == B_REACH
    assert order == list(range(n_groups - 1, -1, -1))

    def in_spec(col0):
        blk = col0 // HEAD_BLOCK
        return pl.BlockSpec((1, s_len, HEAD_BLOCK), lambda b, g: (b, 0, blk + n_groups - 1 - g))

    slabs = HEAD_BLOCK // LANES
    regrouped = pltpu.VMEM((s_len, HEAD_BLOCK), jnp.bfloat16)
    state = pltpu.VMEM((slabs, s_len, LANES), jnp.float32)
    return pl.pallas_call(
        functools.partial(_band_mix_kernel, s_len=s_len, dilations=dilations),
        out_shape=jax.ShapeDtypeStruct((bn, s_len, B_WIDTH), jnp.bfloat16),
        grid=(bn, n_groups),
        in_specs=[in_spec(COL_BQ), in_spec(COL_BK), in_spec(COL_BV)],
        out_specs=pl.BlockSpec((1, s_len, B_WIDTH), lambda b, g: (b, 0, 0)),
        scratch_shapes=[pltpu.VMEM((B_CASES, B_HEADS * B_TQ_MAX, B_TQ_MAX + 2 * B_REACH), jnp.float32),
                        pltpu.VMEM((s_len, LANES), jnp.float32),
                        pltpu.VMEM((s_len // 4, LANES), jnp.float32),
                        regrouped, regrouped, regrouped,
                        state, state],
        compiler_params=pltpu.CompilerParams(
            dimension_semantics=("parallel", "arbitrary"),
            vmem_limit_bytes=VMEM_LIMIT),
        name="band_mix",
    )(proj, proj, proj)


def _na_table_kernel(rpb_ref, tab_ref, *, rows):
    h = pl.program_id(0)
    n_dr = 2 * NA_KH - 1
    n_dc = 2 * NA_KW - 1
    shape = (GRID_W, LANES)
    qc = lax.broadcasted_iota(jnp.int32, shape, 0)
    lane = lax.broadcasted_iota(jnp.int32, shape, 1)
    kc = lane % GRID_W
    dc_i = jnp.clip(kc - qc + (NA_KW - 1), 0, n_dc - 1)
    cs = jnp.clip(qc - NA_KW // 2, 0, GRID_W - NA_KW)
    col_valid = (kc >= cs) & (kc < cs + NA_KW)
    masked = jnp.full(shape, MASK_VALUE, jnp.float32)

    row_bias = []
    for dr in range(n_dr):
        base = (h * n_dr + dr) * n_dc
        t = masked
        for dc in range(n_dc):
            t = jnp.where(dc_i == dc, rpb_ref[base + dc], t)
        row_bias.append(jnp.where(col_valid, t, MASK_VALUE))

    pair_rep = (0, 1, 2, rows // 2 - 2, rows // 2 - 1)
    for case in range(C_CASES):
        ip = pair_rep[case]
        ws = min(max(2 * ip - NA_KH // 2, 0), rows - C_KROWS)
        for qr_l in range(2):
            r = 2 * ip + qr_l
            rs = min(max(r - NA_KH // 2, 0), rows - NA_KH)
            for tile in range(C_WIN // LANES):
                halves = []
                for kr in (ws + 2 * tile, ws + 2 * tile + 1):
                    halves.append(row_bias[kr - r + NA_KH - 1] if rs <= kr < rs + NA_KH else masked)
                tab_ref[case, 0, qr_l * GRID_W:(qr_l + 1) * GRID_W, tile * LANES:(tile + 1) * LANES] = (
                    jnp.where(lane < GRID_W, halves[0], halves[1]))


def _na_tables(rpb, rows):
    flat = rpb.reshape(-1)
    return pl.pallas_call(
        functools.partial(_na_table_kernel, rows=rows),
        out_shape=jax.ShapeDtypeStruct((C_CASES, C_HEADS, C_PAIR, C_WIN), jnp.float32),
        grid=(C_HEADS,),
        in_specs=[pl.BlockSpec(memory_space=pltpu.SMEM)],
        out_specs=pl.BlockSpec((C_CASES, 1, C_PAIR, C_WIN), lambda h: (0, h, 0, 0)),
        name="na_tables",
    )(flat)


def _na_kernel(q_ref, k_ref, v_ref, tab_ref, o_ref, *, rows):
    rb = pl.program_id(1)
    qmask = _head_lane_masks((C_PAIR, HEAD_BLOCK))
    pairs_per_step = C_RB // 2
    for t in range(pairs_per_step):
        ip = rb * pairs_per_step + t
        ws = jnp.clip(2 * ip - NA_KH // 2, 0, rows - C_KROWS)
        case = ip - ws // 2
        k0 = pl.multiple_of(ws * GRID_W, C_PAIR)
        q = q_ref[0, t * C_PAIR:(t + 1) * C_PAIR, :]
        kw = k_ref[0, pl.ds(k0, C_WIN), :]
        vw = v_ref[0, pl.ds(k0, C_WIN), :]
        qs = _stack_heads(q, qmask)
        s = lax.dot_general(qs, kw, _NT, preferred_element_type=jnp.float32)
        s = s + tab_ref[case]
        mx = jnp.max(s, axis=-1, keepdims=True)
        e = jnp.exp(s - mx)
        den = jnp.sum(e, axis=-1, keepdims=True)
        o = jnp.dot(e.astype(jnp.bfloat16), vw, preferred_element_type=jnp.float32)
        o_acc = _unstack_heads(o * (1.0 / den), qmask, C_PAIR)
        o_ref[0, t * C_PAIR:(t + 1) * C_PAIR, :] = o_acc.astype(o_ref.dtype)


def _neighborhood_attention(proj, tables):
    bn, s_len, _ = proj.shape
    rows = s_len // GRID_W
    tq = C_RB * GRID_W
    qblk = COL_CQ // HEAD_BLOCK
    kblk = COL_CK // HEAD_BLOCK
    vblk = COL_CV // HEAD_BLOCK
    return pl.pallas_call(
        functools.partial(_na_kernel, rows=rows),
        out_shape=jax.ShapeDtypeStruct((bn, s_len, C_WIDTH), jnp.bfloat16),
        grid=(bn, rows // C_RB),
        in_specs=[pl.BlockSpec((1, tq, HEAD_BLOCK), lambda b, i: (b, i, qblk)),
                  pl.BlockSpec((1, s_len, HEAD_BLOCK), lambda b, i: (b, 0, kblk)),
                  pl.BlockSpec((1, s_len, HEAD_BLOCK), lambda b, i: (b, 0, vblk)),
                  pl.BlockSpec((C_CASES, C_HEADS * C_PAIR, C_WIN), lambda b, i: (0, 0, 0))],
        out_specs=pl.BlockSpec((1, tq, C_WIDTH), lambda b, i: (b, i, 0)),
        compiler_params=pltpu.CompilerParams(
            dimension_semantics=("parallel", "arbitrary"),
            vmem_limit_bytes=VMEM_LIMIT),
        name="na_attn",
    )(proj, proj, proj, tables.reshape(C_CASES, C_HEADS * C_PAIR, C_WIN))


def _merge_kernel(x_ref, gate_ref, ya_ref, yb_ref, yc_ref, z_ref, g0_ref, g1_ref, g2_ref,
                  wb_ref, wo_ref, fg_ref, o_ref, *, final_norm):
    z = z_ref[0].astype(jnp.float32)
    sz = z * _sigmoid(z)
    ya = (ya_ref[0].astype(jnp.float32) * sz[:, :A_WIDTH]).astype(jnp.bfloat16)
    ybz = (yb_ref[0].astype(jnp.float32) * sz[:, A_WIDTH:A_WIDTH + B_WIDTH]).astype(jnp.bfloat16)
    ycz = (yc_ref[0].astype(jnp.float32) * sz[:, A_WIDTH + B_WIDTH:]).astype(jnp.bfloat16)

    pa = jnp.dot(ya, wb_ref[:A_WIDTH, :], preferred_element_type=jnp.float32)
    pb = jnp.dot(ybz, wb_ref[A_WIDTH:A_WIDTH + B_WIDTH, :], preferred_element_type=jnp.float32)
    pc = jnp.dot(ycz, wb_ref[A_WIDTH + B_WIDTH:, :], preferred_element_type=jnp.float32)
    merged = _sigmoid(g0_ref[0].astype(jnp.float32)) * pa
    merged = merged + _sigmoid(g1_ref[0].astype(jnp.float32)) * pb
    merged = merged + _sigmoid(g2_ref[0].astype(jnp.float32)) * pc
    out = jnp.dot(merged.astype(jnp.bfloat16), wo_ref[...], preferred_element_type=jnp.float32)
    xn = x_ref[0] + gate_ref[0] * out
    if final_norm:
        ms = jnp.mean(xn * xn, axis=-1, keepdims=True)
        xn = xn * lax.rsqrt(ms + RMS_EPS) * fg_ref[...]
    o_ref[0] = xn


def _merge(x, gate, ya, yb, yc, proj, w_branch, w_out, final_gain, layer, final_norm):
    bn, s_len, _ = x.shape
    tm = MERGE_TM
    zblk = COL_Z // D_MODEL
    gblk = COL_G // D_MODEL

    def tok(width, blk=0):
        return pl.BlockSpec((1, tm, width), lambda b, i: (b, i, blk))

    return pl.pallas_call(
        functools.partial(_merge_kernel, final_norm=final_norm),
        out_shape=jax.ShapeDtypeStruct((bn, s_len, D_MODEL), jnp.float32),
        grid=(bn, s_len // tm),
        in_specs=[tok(D_MODEL),
                  pl.BlockSpec((1, 1, D_MODEL), lambda b, i: (b, 0, 0)),
                  tok(A_WIDTH), tok(B_WIDTH), tok(C_WIDTH),
                  tok(D_MODEL, zblk), tok(D_MODEL, gblk), tok(D_MODEL, gblk + 1), tok(D_MODEL, gblk + 2),
                  pl.BlockSpec((None, BR_WIDTH, D_MODEL), lambda b, i: (layer, 0, 0)),
                  pl.BlockSpec((None, D_MODEL, D_MODEL), lambda b, i: (layer, 0, 0)),
                  pl.BlockSpec((1, D_MODEL), lambda b, i: (0, 0))],
        out_specs=tok(D_MODEL),
        compiler_params=pltpu.CompilerParams(
            dimension_semantics=("parallel", "parallel"),
            vmem_limit_bytes=VMEM_LIMIT),
        name="merge",
    )(x, gate, ya, yb, yc, proj, proj, proj, proj,
      w_branch, w_out, final_gain)


def _w_main_kernel(w_ref, o_ref):
    d = pl.program_id(1)
    w = w_ref[...].astype(jnp.bfloat16)

    @pl.when(d < ROPE_COLS // W_PREP_COLS)
    def _():
        src = lax.broadcasted_iota(jnp.int32, (HEAD_BLOCK, HEAD_BLOCK), 0)
        dst = lax.broadcasted_iota(jnp.int32, (HEAD_BLOCK, HEAD_BLOCK), 1)
        onehot = jnp.where(src == _rope_layout_source(dst), 1.0, 0.0).astype(jnp.bfloat16)
        for blk in range(W_PREP_COLS // HEAD_BLOCK):
            cols = slice(blk * HEAD_BLOCK, (blk + 1) * HEAD_BLOCK)
            o_ref[:, cols] = jnp.dot(w[:, cols], onehot,
                                     preferred_element_type=jnp.float32).astype(o_ref.dtype)

    @pl.when(d >= ROPE_COLS // W_PREP_COLS)
    def _():
        o_ref[...] = w


def _w_vt_kernel(w_ref, o_ref):
    w = w_ref[...].astype(jnp.bfloat16)
    r = lax.broadcasted_iota(jnp.int32, (A_WIDTH, A_WIDTH), 0)
    c = lax.broadcasted_iota(jnp.int32, (A_WIDTH, A_WIDTH), 1)
    eye = jnp.where(r == c, 1.0, 0.0).astype(jnp.bfloat16)
    o_ref[...] = lax.dot_general(eye, w, _NT, preferred_element_type=jnp.float32).astype(o_ref.dtype)


def _split_w_in(w_in):
    depth = w_in.shape[0]
    av_blk = 2 * A_QK_COLS // W_PREP_COLS
    assert 2 * A_QK_COLS % W_PREP_COLS == 0 and A_WIDTH == W_PREP_COLS and ROPE_COLS % W_PREP_COLS == 0
    w_main = pl.pallas_call(
        _w_main_kernel,
        out_shape=jax.ShapeDtypeStruct((depth, D_MODEL, MAIN_COLS), jnp.bfloat16),
        grid=(depth, MAIN_COLS // W_PREP_COLS),
        in_specs=[pl.BlockSpec((None, D_MODEL, W_PREP_COLS),
                               lambda l, d: (l, 0, jnp.where(d < av_blk, d, d + 1)))],
        out_specs=pl.BlockSpec((None, D_MODEL, W_PREP_COLS), lambda l, d: (l, 0, d)),
        name="w_main_prep",
    )(w_in)
    w_vt = pl.pallas_call(
        _w_vt_kernel,
        out_shape=jax.ShapeDtypeStruct((depth, A_WIDTH, D_MODEL), jnp.bfloat16),
        grid=(depth,),
        in_specs=[pl.BlockSpec((None, D_MODEL, A_WIDTH), lambda l: (l, 0, av_blk))],
        out_specs=pl.BlockSpec((None, A_WIDTH, D_MODEL), lambda l: (l, 0, 0)),
        name="w_vt_prep",
    )(w_in)
    return w_main, w_vt


def kernel(x, c, positions, norm_gain, w_ada, b_ada, w_in, diff_lambda, diff_subln_gain, na_rpb,
           w_branch, w_out, final_gain):
    depth = w_in.shape[0]
    bn, s_len, _ = x.shape
    rows = s_len // GRID_W
    tabs = _rope_tables(positions)
    ada = _ada_all_layers(c, w_ada, b_ada)
    fg = final_gain.reshape(1, D_MODEL)
    w_main, w_vt = _split_w_in(w_in)
    w_branch = w_branch.astype(jnp.bfloat16)
    w_out = w_out.astype(jnp.bfloat16)
    for layer in range(depth):
        shift = ada[layer, :, None, :D_MODEL]
        scale = ada[layer, :, None, D_MODEL:2 * D_MODEL]
        gate = ada[layer, :, None, 2 * D_MODEL:]
        proj, vt = _in_proj(x, shift, scale, norm_gain[layer].reshape(1, D_MODEL), tabs, w_main, w_vt,
                            layer)
        lambda_init = 0.8 - 0.6 * math.exp(-0.3 * layer)
        ya = _diff_attention(proj, vt, diff_lambda[layer], diff_subln_gain[layer], lambda_init)
        yb = _band_mixture(proj)
        yc = _neighborhood_attention(proj, _na_tables(na_rpb[layer], rows))
        x = _merge(x, gate, ya, yb, yc, proj, w_branch, w_out, fg, layer,
                   final_norm=(layer == depth - 1))
    return x
```

```python
import functools
import math

import numpy as np
import jax
import jax.numpy as jnp
from jax import lax
from jax.experimental import pallas as pl
from jax.experimental.pallas import tpu as pltpu

D_MODEL = 1024
HEAD_DIM = 64
ROT_DIM = HEAD_DIM // 4
ROT_HALF = ROT_DIM // 2
ROPE_THETA = 500000.0
RMS_EPS = 1e-6
MASK_VALUE = -1e30
LOG2_E = math.log2(math.e)

A_HEADS = 4
A_VDIM = 2 * HEAD_DIM
A_WIDTH = A_HEADS * A_VDIM
A_QK_COLS = A_HEADS * 2 * HEAD_DIM

B_PATTERNS = ((128, 1), (512, 4), (2048, 16))
B_GROUPS = 3
B_HEADS = 4
B_WIDTH = B_HEADS * HEAD_DIM
B_COLS = B_GROUPS * B_WIDTH

GRID_W = 64
C_HEADS = 4
NA_KH = 8
NA_KW = 16
C_WIDTH = C_HEADS * HEAD_DIM

N_BRANCHES = 3
BR_WIDTH = A_WIDTH + B_WIDTH + C_WIDTH

LANES = 128
HEAD_BLOCK = 4 * HEAD_DIM
ROPE_LANES = (HEAD_BLOCK // HEAD_DIM) * ROT_HALF
VMEM_LIMIT = 56 * 1024 * 1024

COL_AQ = 0
COL_AK = COL_AQ + A_QK_COLS
COL_BQ = COL_AK + A_QK_COLS
COL_BK = COL_BQ + B_COLS
COL_BV = COL_BK + B_COLS
COL_CQ = COL_BV + B_COLS
COL_CK = COL_CQ + C_WIDTH
COL_CV = COL_CK + C_WIDTH
COL_Z = COL_CV + C_WIDTH
COL_G = COL_Z + BR_WIDTH
MAIN_COLS = COL_G + N_BRANCHES * D_MODEL
ROPE_COLS = COL_BV

W_PREP_COLS = 512
PROJ_TM = 2048
PROJ_TN = 1024
A_TQ = 2048
A_KC = 512
A_VROWS = A_VDIM + 16
B_TQ_MAX = 128
B_UNROLL_MAX = 8
B_REACH = 64
B_CASES = 3
C_PAIR = 2 * GRID_W
C_KROWS = 10
C_WIN = C_KROWS * GRID_W
C_CASES = 5
C_RB = 16
MERGE_TM = 512

_NT = (((1,), (1,)), ((), ()))


def _sigmoid(x):
    return 1.0 / (1.0 + jnp.exp2(x * (-LOG2_E)))


def _rope_layout_head(col):
    l = col % LANES
    c = col // LANES
    rest = HEAD_DIM - ROT_DIM
    return jnp.where(l < ROPE_LANES, l // ROT_HALF, 2 * c + jnp.where(l >= ROPE_LANES + rest, 1, 0))


def _rope_layout_source(col):
    l = col % LANES
    c = col // LANES
    rest = HEAD_DIM - ROT_DIM
    r = l - ROPE_LANES
    upper = jnp.where(r >= rest, 1, 0)
    rotary_src = HEAD_DIM * (l // ROT_HALF) + ROT_HALF * c + l % ROT_HALF
    other_src = HEAD_DIM * (2 * c + upper) + ROT_DIM + r - rest * upper
    return jnp.where(l < ROPE_LANES, rotary_src, other_src)


def _rope_table_kernel(pos_ref, inv_ref, c_ref, s_ref):
    ang = pos_ref[0] * inv_ref[...]
    lane = lax.broadcasted_iota(jnp.int32, ang.shape, 1)
    c_ref[0] = jnp.where(lane < ROPE_LANES, jnp.cos(ang), 1.0)
    s_ref[0] = jnp.where(lane < ROPE_LANES, jnp.sin(ang), 0.0)


def _rope_tables(positions):
    bn, s_len = positions.shape
    inv = np.float32(ROPE_THETA) ** (-np.arange(0, ROT_DIM, 2, dtype=np.float32) / np.float32(ROT_DIM))
    inv_lane = np.zeros((1, LANES), np.float32)
    inv_lane[0, :ROPE_LANES] = np.tile(inv, ROPE_LANES // ROT_HALF)
    pos = positions.astype(jnp.float32)[..., None]
    ts = 1024
    tab = jax.ShapeDtypeStruct((bn, s_len, LANES), jnp.float32)
    spec = pl.BlockSpec((1, ts, LANES), lambda b, i: (b, i, 0))
    return pl.pallas_call(
        _rope_table_kernel,
        out_shape=(tab, tab),
        grid=(bn, s_len // ts),
        in_specs=[pl.BlockSpec((1, ts, 1), lambda b, i: (b, i, 0)),
                  pl.BlockSpec((1, LANES), lambda b, i: (0, 0))],
        out_specs=(spec, spec),
        name="rope_tables",
    )(pos, jnp.asarray(inv_lane))


def _ada_kernel(c_ref, w_ref, b_ref, o_ref):
    c = c_ref[...]
    c_act = c * _sigmoid(c)
    o_ref[0] = jnp.dot(c_act, w_ref[0], precision=lax.Precision.HIGHEST,
                       preferred_element_type=jnp.float32) + b_ref[0]


def _ada_all_layers(c, w_ada, b_ada):
    depth = w_ada.shape[0]
    bn = c.shape[0]
    rows = 8
    c_pad = jnp.pad(c, ((0, rows - bn), (0, 0)))
    tn = 1024
    out = pl.pallas_call(
        _ada_kernel,
        out_shape=jax.ShapeDtypeStruct((depth, rows, 3 * D_MODEL), jnp.float32),
        grid=(depth, 3 * D_MODEL // tn),
        in_specs=[pl.BlockSpec((rows, D_MODEL), lambda l, j: (0, 0)),
                  pl.BlockSpec((1, D_MODEL, tn), lambda l, j: (l, 0, j)),
                  pl.BlockSpec((1, 1, tn), lambda l, j: (l, 0, j))],
        out_specs=pl.BlockSpec((1, rows, tn), lambda l, j: (l, 0, j)),
        name="adaln",
    )(c_pad, w_ada, b_ada.reshape(depth, 1, 3 * D_MODEL))
    return out[:, :bn]


def _in_proj_kernel(x_ref, shift_ref, scale_ref, gain_ref, c_ref, s_ref, w_ref, wvt_ref,
                    proj_ref, vt_ref, h_ref):
    j = pl.program_id(2)

    @pl.when(j == 0)
    def _():
        xf = x_ref[0]
        ms = jnp.mean(xf * xf, axis=-1, keepdims=True)
        y = xf * lax.rsqrt(ms + RMS_EPS) * gain_ref[...]
        h = (y * (1.0 + scale_ref[0]) + shift_ref[0]).astype(jnp.bfloat16)
        h_ref[...] = h
        vt = lax.dot_general(wvt_ref[...], h, _NT, preferred_element_type=jnp.float32)
        vt_ref[0] = vt.astype(vt_ref.dtype)

    acc = jnp.dot(h_ref[...], w_ref[...], preferred_element_type=jnp.float32)
    blocks_per_tile = PROJ_TN // HEAD_BLOCK
    cc = c_ref[0]
    ss = s_ref[0]
    for blk in range(blocks_per_tile):
        rotary = (j * blocks_per_tile + blk) < (ROPE_COLS // HEAD_BLOCK)
        cb = jnp.where(rotary, cc, 1.0)
        sb = jnp.where(rotary, ss, 0.0)
        lo0 = blk * HEAD_BLOCK
        hi0 = lo0 + LANES
        lo = acc[:, lo0:hi0]
        hi = acc[:, hi0:hi0 + LANES]
        proj_ref[0, :, lo0:hi0] = (lo * cb - hi * sb).astype(proj_ref.dtype)
        proj_ref[0, :, hi0:hi0 + LANES] = (hi * cb + lo * sb).astype(proj_ref.dtype)


def _in_proj(x, shift, scale, gain, tabs, w_main, w_vt, layer):
    bn, s_len, _ = x.shape
    tm, tn = PROJ_TM, PROJ_TN
    tab_spec = pl.BlockSpec((1, tm, LANES), lambda b, i, j: (b, i, 0))
    mod_spec = pl.BlockSpec((1, 1, D_MODEL), lambda b, i, j: (b, 0, 0))
    return pl.pallas_call(
        _in_proj_kernel,
        out_shape=(jax.ShapeDtypeStruct((bn, s_len, MAIN_COLS), jnp.bfloat16),
                   jax.ShapeDtypeStruct((bn, A_WIDTH, s_len), jnp.bfloat16)),
        grid=(bn, s_len // tm, MAIN_COLS // tn),
        in_specs=[pl.BlockSpec((1, tm, D_MODEL), lambda b, i, j: (b, i, 0)),
                  mod_spec, mod_spec,
                  pl.BlockSpec((1, D_MODEL), lambda b, i, j: (0, 0)),
                  tab_spec, tab_spec,
                  pl.BlockSpec((None, D_MODEL, tn), lambda b, i, j: (layer, 0, j)),
                  pl.BlockSpec((None, A_WIDTH, D_MODEL), lambda b, i, j: (layer, 0, 0))],
        out_specs=(pl.BlockSpec((1, tm, tn), lambda b, i, j: (b, i, j)),
                   pl.BlockSpec((1, A_WIDTH, tm), lambda b, i, j: (b, 0, i))),
        scratch_shapes=[pltpu.VMEM((tm, D_MODEL), jnp.bfloat16)],
        compiler_params=pltpu.CompilerParams(
            dimension_semantics=("parallel", "parallel", "arbitrary"),
            vmem_limit_bytes=VMEM_LIMIT),
        name="in_proj",
    )(x, shift, scale, gain, *tabs, w_main, w_vt)


def _diff_attn_kernel(q_ref, k_ref, vt_ref, dl_ref, gain_ref, o_ref,
                      qm_ref, vta_ref, sa_ref, sb_ref, m_ref, acc_ref, *, lambda_init, s_len):
    h = pl.program_id(1)
    tq = q_ref.shape[1]
    n_chunks = s_len // A_KC

    @pl.when(pl.program_id(2) == 0)
    def _():
        vta_ref[:A_VDIM, :] = vt_ref[0]
        row = lax.broadcasted_iota(jnp.int32, (A_VROWS - A_VDIM, s_len), 0)
        vta_ref[A_VDIM:, :] = jnp.where(row == 0, 1.0, 0.0).astype(vta_ref.dtype)

    q = q_ref[0]
    pair = _rope_layout_head(lax.broadcasted_iota(jnp.int32, q.shape, 1))
    base = 2 * (h % 2)
    for c in range(2):
        qsel = jnp.where(pair == base + c, q, jnp.zeros_like(q)).astype(jnp.float32)
        qm_ref[c] = (qsel * (HEAD_DIM ** -0.5 * LOG2_E)).astype(qm_ref.dtype)
        m_ref[c] = jnp.full((1, tq), MASK_VALUE, jnp.float32)
        acc_ref[c] = jnp.zeros((A_VROWS, tq), jnp.float32)

    def scores(i, s_ref):
        kc = k_ref[0, pl.ds(pl.multiple_of(i * A_KC, A_KC), A_KC), :]
        for c in range(2):
            s_ref[c] = lax.dot_general(kc, qm_ref[c], _NT,
                                       preferred_element_type=jnp.float32)

    def update(i, s_ref):
        vt_c = vta_ref[:, pl.ds(pl.multiple_of(i * A_KC, A_KC), A_KC)]
        for c in range(2):
            st = s_ref[c]
            m_old = m_ref[c]
            m_new = jnp.maximum(m_old, jnp.max(st, axis=0, keepdims=True))
            alpha = jnp.exp2(m_old - m_new)
            e = jnp.exp2(st - m_new).astype(jnp.bfloat16)
            pv = jnp.dot(vt_c, e, preferred_element_type=jnp.float32)
            acc_ref[c] = alpha * acc_ref[c] + pv
            m_ref[c] = m_new

    scores(0, sa_ref)

    def body(j, carry):
        scores(2 * j + 1, sb_ref)
        update(2 * j, sa_ref)
        scores(2 * j + 2, sa_ref)
        update(2 * j + 1, sb_ref)
        return carry

    lax.fori_loop(0, n_chunks // 2 - 1, body, 0)
    scores(n_chunks - 1, sb_ref)
    update(n_chunks - 2, sa_ref)
    update(n_chunks - 1, sb_ref)

    dl = dl_ref[...]
    lam = (jnp.exp(jnp.sum(dl[0:1] * dl[1:2], axis=1, keepdims=True))
           - jnp.exp(jnp.sum(dl[2:3] * dl[3:4], axis=1, keepdims=True)) + lambda_init)
    maps = [acc_ref[c, :A_VDIM, :] * (1.0 / acc_ref[c, A_VDIM:A_VDIM + 1, :]) for c in range(2)]
    ot = maps[0] - lam * maps[1]
    ms = jnp.mean(ot * ot, axis=0, keepdims=True)
    y = ot * lax.rsqrt(ms + RMS_EPS) * gain_ref[...] * (1.0 - lambda_init)
    o_ref[0] = y.T.astype(o_ref.dtype)


def _diff_attention(proj, vt, diff_lambda, subln_gain, lambda_init):
    bn, s_len, _ = proj.shape
    tq = A_TQ
    qblk = COL_AQ // HEAD_BLOCK
    kblk = COL_AK // HEAD_BLOCK
    assert s_len % (2 * A_KC) == 0
    return pl.pallas_call(
        functools.partial(_diff_attn_kernel, lambda_init=lambda_init, s_len=s_len),
        out_shape=jax.ShapeDtypeStruct((bn, s_len, A_WIDTH), jnp.bfloat16),
        grid=(bn, A_HEADS, s_len // tq),
        in_specs=[pl.BlockSpec((1, tq, HEAD_BLOCK), lambda b, h, i: (b, i, qblk + h // 2)),
                  pl.BlockSpec((1, s_len, HEAD_BLOCK), lambda b, h, i: (b, 0, kblk + h // 2)),
                  pl.BlockSpec((1, A_VDIM, s_len), lambda b, h, i: (b, h, 0)),
                  pl.BlockSpec((4, HEAD_DIM), lambda b, h, i: (0, 0)),
                  pl.BlockSpec((A_VDIM, 1), lambda b, h, i: (0, 0))],
        out_specs=pl.BlockSpec((1, tq, A_VDIM), lambda b, h, i: (b, i, h)),
        scratch_shapes=[pltpu.VMEM((2, tq, HEAD_BLOCK), jnp.bfloat16),
                        pltpu.VMEM((A_VROWS, s_len), jnp.bfloat16),
                        pltpu.VMEM((2, A_KC, tq), jnp.float32),
                        pltpu.VMEM((2, A_KC, tq), jnp.float32),
                        pltpu.VMEM((2, 1, tq), jnp.float32),
                        pltpu.VMEM((2, A_VROWS, tq), jnp.float32)],
        compiler_params=pltpu.CompilerParams(
            dimension_semantics=("parallel", "parallel", "arbitrary"),
            vmem_limit_bytes=VMEM_LIMIT),
        name="diff_attn",
    )(proj, proj, vt, diff_lambda, subln_gain.reshape(A_VDIM, 1))


def _head_lane_masks(shape, rotary_layout=False):
    col = lax.broadcasted_iota(jnp.int32, shape, 1)
    head = _rope_layout_head(col) if rotary_layout else col // HEAD_DIM
    return [head == h for h in range(HEAD_BLOCK // HEAD_DIM)]


def _stack_heads(q, qmask, scale):
    q = (q.astype(jnp.float32) * scale).astype(q.dtype)
    zero = jnp.zeros_like(q)
    return jnp.concatenate([jnp.where(mk, q, zero) for mk in qmask], axis=0)


def _unstack_heads(x, qmask, tq):
    out = x[:tq]
    for h in range(1, len(qmask)):
        out = jnp.where(qmask[h], x[h * tq:(h + 1) * tq], out)
    return out


def _band_group(q_ref, k_ref, v_ref, y_ref, band_ref, stage_ref, tmp_ref, qd_ref, kd_ref, vd_ref,
                run_o_ref, run_l_ref, *, dilation, s_len, first, last):
    length = s_len // dilation
    tq = min(B_TQ_MAX, length)
    win = min(tq + 2 * B_REACH, length)
    unroll = min(B_UNROLL_MAX, s_len // tq)
    nblk = length // tq
    assert length % tq == 0 and (s_len // tq) % unroll == 0 and dilation in (1, 4, 16)
    assert not (last and dilation > 1)
    qmask = _head_lane_masks((tq, HEAD_BLOCK), rotary_layout=True)
    vmask = _head_lane_masks((tq, HEAD_BLOCK))

    if dilation > 1:
        for src, dst in ((q_ref, qd_ref), (k_ref, kd_ref), (v_ref, vd_ref)):
            for slab in range(HEAD_BLOCK // LANES):
                lanes = slice(slab * LANES, (slab + 1) * LANES)
                stage_ref[...] = src[0, :, lanes].astype(jnp.float32)
                for m4 in range(4):
                    quarter = stage_ref[pl.ds(m4, s_len // 4, stride=4), :]
                    if dilation == 4:
                        dst[m4 * length:(m4 + 1) * length, lanes] = quarter.astype(dst.dtype)
                    else:
                        tmp_ref[...] = quarter
                        for mm in range(4):
                            m = m4 + 4 * mm
                            dst[m * length:(m + 1) * length, lanes] = tmp_ref[
                                pl.ds(mm, length, stride=4), :].astype(dst.dtype)
        q_src, k_src, v_src = qd_ref, kd_ref, vd_ref
    else:
        q_src, k_src, v_src = q_ref.at[0], k_ref.at[0], v_ref.at[0]

    def one_block(t):
        m = t // nblk
        q0 = pl.multiple_of((t % nblk) * tq, tq)
        ks = pl.multiple_of(jnp.clip(q0 - B_REACH, 0, length - win), B_REACH)
        base = pl.multiple_of(m * length, tq)
        if dilation > 1:
            rows = pl.ds(q0 * dilation + m, tq, stride=dilation)
        else:
            rows = pl.ds(q0, tq)
        q = q_src[pl.ds(base + q0, tq), :]
        kw = k_src[pl.ds(base + ks, win), :]
        vw = v_src[pl.ds(base + ks, win), :]
        qs = _stack_heads(q, qmask, HEAD_DIM ** -0.5 * LOG2_E)
        s = lax.dot_general(qs, kw, _NT, preferred_element_type=jnp.float32)
        s = s + band_ref[(q0 - ks) // B_REACH]
        mx = jnp.max(s, axis=-1, keepdims=True)
        e = jnp.exp2(s - mx)
        den = jnp.sum(e, axis=-1, keepdims=True)
        o = jnp.dot(e.astype(jnp.bfloat16), vw, preferred_element_type=jnp.float32)
        o = o * (1.0 / den)
        lse = jnp.broadcast_to(mx + jnp.log2(den), o.shape)
        o_new = _unstack_heads(o, vmask, tq)
        l_new = _unstack_heads(lse, vmask, tq)
        for slab in range(HEAD_BLOCK // LANES):
            lanes = slice(slab * LANES, (slab + 1) * LANES)
            o_s, l_s = o_new[:, lanes], l_new[:, lanes]
            if not first:
                o_run, l_run = run_o_ref[slab, rows, :], run_l_ref[slab, rows, :]
                l_max = jnp.maximum(l_run, l_s)
                w_run, w_new = jnp.exp2(l_run - l_max), jnp.exp2(l_s - l_max)
                total = w_run + w_new
                o_s = (o_run * w_run + o_s * w_new) * (1.0 / total)
                l_s = l_max + jnp.log2(total)
            if last:
                y_ref[0, rows, lanes] = o_s.astype(y_ref.dtype)
            else:
                run_o_ref[slab, rows, :] = o_s
                run_l_ref[slab, rows, :] = l_s

    def body(i, carry):
        for u in range(unroll):
            one_block(i * unroll + u)
        return carry

    lax.fori_loop(0, dilation * nblk // unroll, body, 0)


def _band_mix_kernel(q_ref, k_ref, v_ref, y_ref, band_ref, *scratch, s_len, dilations):
    g = pl.program_id(1)

    tq = band_ref.shape[1] // B_HEADS
    row = lax.broadcasted_iota(jnp.int32, band_ref.shape[1:], 0) % tq
    col = lax.broadcasted_iota(jnp.int32, band_ref.shape[1:], 1)
    for case in range(B_CASES):
        valid = jnp.abs(col - case * B_REACH - row) <= B_REACH
        band_ref[case] = jnp.where(valid, 0.0, MASK_VALUE)

    for step, dilation in enumerate(dilations):
        @pl.when(g == step)
        def _(step=step, dilation=dilation):
            _band_group(q_ref, k_ref, v_ref, y_ref, band_ref, *scratch, dilation=dilation,
                        s_len=s_len, first=(step == 0), last=(step == len(dilations) - 1))


def _band_mixture(proj):
    bn, s_len, _ = proj.shape
    n_groups = len(B_PATTERNS)
    order = sorted(range(n_groups), key=lambda g: -B_PATTERNS[g][1])
    dilations = tuple(B_PATTERNS[g][1] for g in order)
    for window, dilation in B_PATTERNS:
        assert (window // 2) // dilation == B_REACH
    assert order == list(range(n_groups - 1, -1, -1))

    def in_spec(col0):
        blk = col0 // HEAD_BLOCK
        return pl.BlockSpec((1, s_len, HEAD_BLOCK), lambda b, g: (b, 0, blk + n_groups - 1 - g))

    slabs = HEAD_BLOCK // LANES
    regrouped = pltpu.VMEM((s_len, HEAD_BLOCK), jnp.bfloat16)
    state = pltpu.VMEM((slabs, s_len, LANES), jnp.float32)
    return pl.pallas_call(
        functools.partial(_band_mix_kernel, s_len=s_len, dilations=dilations),
        out_shape=jax.ShapeDtypeStruct((bn, s_len, B_WIDTH), jnp.bfloat16),
        grid=(bn, n_groups),
        in_specs=[in_spec(COL_BQ), in_spec(COL_BK), in_spec(COL_BV)],
        out_specs=pl.BlockSpec((1, s_len, B_WIDTH), lambda b, g: (b, 0, 0)),
        scratch_shapes=[pltpu.VMEM((B_CASES, B_HEADS * B_TQ_MAX, B_TQ_MAX + 2 * B_REACH), jnp.float32),
                        pltpu.VMEM((s_len, LANES), jnp.float32),
                        pltpu.VMEM((s_len // 4, LANES), jnp.float32),
                        regrouped, regrouped, regrouped,
                        state, state],
        compiler_params=pltpu.CompilerParams(
            dimension_semantics=("parallel", "arbitrary"),
            vmem_limit_bytes=VMEM_LIMIT),
        name="band_mix",
    )(proj, proj, proj)


def _na_table_kernel(rpb_ref, tab_ref, *, rows):
    h = pl.program_id(0)
    n_dr = 2 * NA_KH - 1
    n_dc = 2 * NA_KW - 1
    shape = (GRID_W, LANES)
    qc = lax.broadcasted_iota(jnp.int32, shape, 0)
    lane = lax.broadcasted_iota(jnp.int32, shape, 1)
    kc = lane % GRID_W
    dc_i = jnp.clip(kc - qc + (NA_KW - 1), 0, n_dc - 1)
    cs = jnp.clip(qc - NA_KW // 2, 0, GRID_W - NA_KW)
    col_valid = (kc >= cs) & (kc < cs + NA_KW)
    masked = jnp.full(shape, MASK_VALUE, jnp.float32)

    row_bias = []
    for dr in range(n_dr):
        base = (h * n_dr + dr) * n_dc
        t = masked
        for dc in range(n_dc):
            t = jnp.where(dc_i == dc, rpb_ref[base + dc], t)
        row_bias.append(jnp.where(col_valid, t, MASK_VALUE))

    pair_rep = (0, 1, 2, rows // 2 - 2, rows // 2 - 1)
    for case in range(C_CASES):
        ip = pair_rep[case]
        ws = min(max(2 * ip - NA_KH // 2, 0), rows - C_KROWS)
        for qr_l in range(2):
            r = 2 * ip + qr_l
            rs = min(max(r - NA_KH // 2, 0), rows - NA_KH)
            for tile in range(C_WIN // LANES):
                halves = []
                for kr in (ws + 2 * tile, ws + 2 * tile + 1):
                    halves.append(row_bias[kr - r + NA_KH - 1] if rs <= kr < rs + NA_KH else masked)
                tab_ref[case, 0, qr_l * GRID_W:(qr_l + 1) * GRID_W, tile * LANES:(tile + 1) * LANES] = (
                    jnp.where(lane < GRID_W, halves[0], halves[1]))


def _na_tables(rpb, rows):
    flat = rpb.reshape(-1)
    return pl.pallas_call(
        functools.partial(_na_table_kernel, rows=rows),
        out_shape=jax.ShapeDtypeStruct((C_CASES, C_HEADS, C_PAIR, C_WIN), jnp.float32),
        grid=(C_HEADS,),
        in_specs=[pl.BlockSpec(memory_space=pltpu.SMEM)],
        out_specs=pl.BlockSpec((C_CASES, 1, C_PAIR, C_WIN), lambda h: (0, h, 0, 0)),
        name="na_tables",
    )(flat)


def _na_kernel(q_ref, k_ref, v_ref, tab_ref, o_ref, *, rows):
    rb = pl.program_id(1)
    qmask = _head_lane_masks((C_PAIR, HEAD_BLOCK))
    pairs_per_step = C_RB // 2
    for t in range(pairs_per_step):
        ip = rb * pairs_per_step + t
        ws = jnp.clip(2 * ip - NA_KH // 2, 0, rows - C_KROWS)
        case = ip - ws // 2
        k0 = pl.multiple_of(ws * GRID_W, C_PAIR)
        q = q_ref[0, t * C_PAIR:(t + 1) * C_PAIR, :]
        kw = k_ref[0, pl.ds(k0, C_WIN), :]
        vw = v_ref[0, pl.ds(k0, C_WIN), :]
        qs = _stack_heads(q, qmask, HEAD_DIM ** -0.5)
        s = lax.dot_general(qs, kw, _NT, preferred_element_type=jnp.float32)
        s = s + tab_ref[case]
        mx = jnp.max(s, axis=-1, keepdims=True)
        e = jnp.exp(s - mx)
        den = jnp.sum(e, axis=-1, keepdims=True)
        o = jnp.dot(e.astype(jnp.bfloat16), vw, preferred_element_type=jnp.float32)
        o_acc = _unstack_heads(o * (1.0 / den), qmask, C_PAIR)
        o_ref[0, t * C_PAIR:(t + 1) * C_PAIR, :] = o_acc.astype(o_ref.dtype)


def _neighborhood_attention(proj, tables):
    bn, s_len, _ = proj.shape
    rows = s_len // GRID_W
    tq = C_RB * GRID_W
    qblk = COL_CQ // HEAD_BLOCK
    kblk = COL_CK // HEAD_BLOCK
    vblk = COL_CV // HEAD_BLOCK
    return pl.pallas_call(
        functools.partial(_na_kernel, rows=rows),
        out_shape=jax.ShapeDtypeStruct((bn, s_len, C_WIDTH), jnp.bfloat16),
        grid=(bn, rows // C_RB),
        in_specs=[pl.BlockSpec((1, tq, HEAD_BLOCK), lambda b, i: (b, i, qblk)),
                  pl.BlockSpec((1, s_len, HEAD_BLOCK), lambda b, i: (b, 0, kblk)),
                  pl.BlockSpec((1, s_len, HEAD_BLOCK), lambda b, i: (b, 0, vblk)),
                  pl.BlockSpec((C_CASES, C_HEADS * C_PAIR, C_WIN), lambda b, i: (0, 0, 0))],
        out_specs=pl.BlockSpec((1, tq, C_WIDTH), lambda b, i: (b, i, 0)),
        compiler_params=pltpu.CompilerParams(
            dimension_semantics=("parallel", "arbitrary"),
            vmem_limit_bytes=VMEM_LIMIT),
        name="na_attn",
    )(proj, proj, proj, tables.reshape(C_CASES, C_HEADS * C_PAIR, C_WIN))


def _merge_kernel(x_ref, gate_ref, ya_ref, yb_ref, yc_ref, z_ref, g0_ref, g1_ref, g2_ref,
                  wb_ref, wo_ref, fg_ref, o_ref, *, final_norm):
    z = z_ref[0].astype(jnp.float32)
    sz = z * _sigmoid(z)
    ya = (ya_ref[0].astype(jnp.float32) * sz[:, :A_WIDTH]).astype(jnp.bfloat16)
    ybz = (yb_ref[0].astype(jnp.float32) * sz[:, A_WIDTH:A_WIDTH + B_WIDTH]).astype(jnp.bfloat16)
    ycz = (yc_ref[0].astype(jnp.float32) * sz[:, A_WIDTH + B_WIDTH:]).astype(jnp.bfloat16)

    pa = jnp.dot(ya, wb_ref[:A_WIDTH, :], preferred_element_type=jnp.float32)
    pb = jnp.dot(ybz, wb_ref[A_WIDTH:A_WIDTH + B_WIDTH, :], preferred_element_type=jnp.float32)
    pc = jnp.dot(ycz, wb_ref[A_WIDTH + B_WIDTH:, :], preferred_element_type=jnp.float32)
    merged = _sigmoid(g0_ref[0].astype(jnp.float32)) * pa
    merged = merged + _sigmoid(g1_ref[0].astype(jnp.float32)) * pb
    merged = merged + _sigmoid(g2_ref[0].astype(jnp.float32)) * pc
    out = jnp.dot(merged.astype(jnp.bfloat16), wo_ref[...], preferred_element_type=jnp.float32)
    xn = x_ref[0] + gate_ref[0] * out
    if final_norm:
        ms = jnp.mean(xn * xn, axis=-1, keepdims=True)
        xn = xn * lax.rsqrt(ms + RMS_EPS) * fg_ref[...]
    o_ref[0] = xn


def _merge(x, gate, ya, yb, yc, proj, w_branch, w_out, final_gain, layer, final_norm):
    bn, s_len, _ = x.shape
    tm = MERGE_TM
    zblk = COL_Z // D_MODEL
    gblk = COL_G // D_MODEL

    def tok(width, blk=0):
        return pl.BlockSpec((1, tm, width), lambda b, i: (b, i, blk))

    return pl.pallas_call(
        functools.partial(_merge_kernel, final_norm=final_norm),
        out_shape=jax.ShapeDtypeStruct((bn, s_len, D_MODEL), jnp.float32),
        grid=(bn, s_len // tm),
        in_specs=[tok(D_MODEL),
                  pl.BlockSpec((1, 1, D_MODEL), lambda b, i: (b, 0, 0)),
                  tok(A_WIDTH), tok(B_WIDTH), tok(C_WIDTH),
                  tok(D_MODEL, zblk), tok(D_MODEL, gblk), tok(D_MODEL, gblk + 1), tok(D_MODEL, gblk + 2),
                  pl.BlockSpec((None, BR_WIDTH, D_MODEL), lambda b, i: (layer, 0, 0)),
                  pl.BlockSpec((None, D_MODEL, D_MODEL), lambda b, i: (layer, 0, 0)),
                  pl.BlockSpec((1, D_MODEL), lambda b, i: (0, 0))],
        out_specs=tok(D_MODEL),
        compiler_params=pltpu.CompilerParams(
            dimension_semantics=("parallel", "parallel"),
            vmem_limit_bytes=VMEM_LIMIT),
        name="merge",
    )(x, gate, ya, yb, yc, proj, proj, proj, proj,
      w_branch, w_out, final_gain)


def _w_main_kernel(w_ref, o_ref):
    d = pl.program_id(1)
    w = w_ref[...].astype(jnp.bfloat16)

    @pl.when(d < ROPE_COLS // W_PREP_COLS)
    def _():
        src = lax.broadcasted_iota(jnp.int32, (HEAD_BLOCK, HEAD_BLOCK), 0)
        dst = lax.broadcasted_iota(jnp.int32, (HEAD_BLOCK, HEAD_BLOCK), 1)
        onehot = jnp.where(src == _rope_layout_source(dst), 1.0, 0.0).astype(jnp.bfloat16)
        for blk in range(W_PREP_COLS // HEAD_BLOCK):
            cols = slice(blk * HEAD_BLOCK, (blk + 1) * HEAD_BLOCK)
            o_ref[:, cols] = jnp.dot(w[:, cols], onehot,
                                     preferred_element_type=jnp.float32).astype(o_ref.dtype)

    @pl.when(d >= ROPE_COLS // W_PREP_COLS)
    def _():
        o_ref[...] = w


def _w_vt_kernel(w_ref, o_ref):
    w = w_ref[...].astype(jnp.bfloat16)
    r = lax.broadcasted_iota(jnp.int32, (A_WIDTH, A_WIDTH), 0)
    c = lax.broadcasted_iota(jnp.int32, (A_WIDTH, A_WIDTH), 1)
    eye = jnp.where(r == c, 1.0, 0.0).astype(jnp.bfloat16)
    o_ref[...] = lax.dot_general(eye, w, _NT, preferred_element_type=jnp.float32).astype(o_ref.dtype)


def _split_w_in(w_in):
    depth = w_in.shape[0]
    av_blk = 2 * A_QK_COLS // W_PREP_COLS
    assert 2 * A_QK_COLS % W_PREP_COLS == 0 and A_WIDTH == W_PREP_COLS and ROPE_COLS % W_PREP_COLS == 0
    w_main = pl.pallas_call(
        _w_main_kernel,
        out_shape=jax.ShapeDtypeStruct((depth, D_MODEL, MAIN_COLS), jnp.bfloat16),
        grid=(depth, MAIN_COLS // W_PREP_COLS),
        in_specs=[pl.BlockSpec((None, D_MODEL, W_PREP_COLS),
                               lambda l, d: (l, 0, jnp.where(d < av_blk, d, d + 1)))],
        out_specs=pl.BlockSpec((None, D_MODEL, W_PREP_COLS), lambda l, d: (l, 0, d)),
        name="w_main_prep",
    )(w_in)
    w_vt = pl.pallas_call(
        _w_vt_kernel,
        out_shape=jax.ShapeDtypeStruct((depth, A_WIDTH, D_MODEL), jnp.bfloat16),
        grid=(depth,),
        in_specs=[pl.BlockSpec((None, D_MODEL, A_WIDTH), lambda l: (l, 0, av_blk))],
        out_specs=pl.BlockSpec((None, A_WIDTH, D_MODEL), lambda l: (l, 0, 0)),
        name="w_vt_prep",
    )(w_in)
    return w_main, w_vt


def kernel(x, c, positions, norm_gain, w_ada, b_ada, w_in, diff_lambda, diff_subln_gain, na_rpb,
           w_branch, w_out, final_gain):
    depth = w_in.shape[0]
    bn, s_len, _ = x.shape
    rows = s_len // GRID_W
    tabs = _rope_tables(positions)
    ada = _ada_all_layers(c, w_ada, b_ada)
    fg = final_gain.reshape(1, D_MODEL)
    w_main, w_vt = _split_w_in(w_in)
    w_branch = w_branch.astype(jnp.bfloat16)
    w_out = w_out.astype(jnp.bfloat16)
    for layer in range(depth):
        shift = ada[layer, :, None, :D_MODEL]
        scale = ada[layer, :, None, D_MODEL:2 * D_MODEL]
        gate = ada[layer, :, None, 2 * D_MODEL:]
        proj, vt = _in_proj(x, shift, scale, norm_gain[layer].reshape(1, D_MODEL), tabs, w_main, w_vt,
                            layer)
        lambda_init = 0.8 - 0.6 * math.exp(-0.3 * layer)
        ya = _diff_attention(proj, vt, diff_lambda[layer], diff_subln_gain[layer], lambda_init)
        yb = _band_mixture(proj)
        yc = _neighborhood_attention(proj, _na_tables(na_rpb[layer], rows))
        x = _merge(x, gate, ya, yb, yc, proj, w_branch, w_out, fg, layer,
                   final_norm=(layer == depth - 1))
    return x
```

```python
import functools
import math

import numpy as np
import jax
import jax.numpy as jnp
from jax import lax
from jax.experimental import pallas as pl
from jax.experimental.pallas import tpu as pltpu

D_MODEL = 1024
HEAD_DIM = 64
ROT_DIM = HEAD_DIM // 4
ROT_HALF = ROT_DIM // 2
ROPE_THETA = 500000.0
RMS_EPS = 1e-6
MASK_VALUE = -1e30
LOG2_E = math.log2(math.e)

A_HEADS = 4
A_VDIM = 2 * HEAD_DIM
A_WIDTH = A_HEADS * A_VDIM
A_QK_COLS = A_HEADS * 2 * HEAD_DIM

B_PATTERNS = ((128, 1), (512, 4), (2048, 16))
B_GROUPS = 3
B_HEADS = 4
B_WIDTH = B_HEADS * HEAD_DIM
B_COLS = B_GROUPS * B_WIDTH

GRID_W = 64
C_HEADS = 4
NA_KH = 8
NA_KW = 16
C_WIDTH = C_HEADS * HEAD_DIM

N_BRANCHES = 3
BR_WIDTH = A_WIDTH + B_WIDTH + C_WIDTH

LANES = 128
SUBLANES_F32 = 8
SUBLANES_BF16 = 16
HEAD_BLOCK = 4 * HEAD_DIM
ROPE_LANES = (HEAD_BLOCK // HEAD_DIM) * ROT_HALF
VMEM_LIMIT = 56 * 1024 * 1024

COL_AQ = 0
COL_AK = COL_AQ + A_QK_COLS
COL_BQ = COL_AK + A_QK_COLS
COL_BK = COL_BQ + B_COLS
COL_BV = COL_BK + B_COLS
COL_CQ = COL_BV + B_COLS
COL_CK = COL_CQ + C_WIDTH
COL_CV = COL_CK + C_WIDTH
COL_Z = COL_CV + C_WIDTH
COL_G = COL_Z + BR_WIDTH
MAIN_COLS = COL_G + N_BRANCHES * D_MODEL
ROPE_COLS = COL_BV

W_PREP_COLS = 512
PROJ_TM = 2048
PROJ_TN = 1024
A_TQ = 2048
A_KC = 512
A_VROWS = A_VDIM + SUBLANES_BF16
TABLE_TS = 1024
ADA_TN = 1024
B_TQ_MAX = 128
B_UNROLL_MAX = 8
B_REACH = 64
B_CASES = 3
C_PAIR = 2 * GRID_W
C_KROWS = 10
C_WIN = C_KROWS * GRID_W
C_CASES = 5
C_RB = 32
MERGE_TM = 1024

_NT = (((1,), (1,)), ((), ()))


def _sigmoid(x):
    return 1.0 / (1.0 + jnp.exp2(x * (-LOG2_E)))


def _rope_layout_head(col):
    l = col % LANES
    c = col // LANES
    rest = HEAD_DIM - ROT_DIM
    return jnp.where(l < ROPE_LANES, l // ROT_HALF, 2 * c + jnp.where(l >= ROPE_LANES + rest, 1, 0))


def _rope_layout_source(col):
    l = col % LANES
    c = col // LANES
    rest = HEAD_DIM - ROT_DIM
    r = l - ROPE_LANES
    upper = jnp.where(r >= rest, 1, 0)
    rotary_src = HEAD_DIM * (l // ROT_HALF) + ROT_HALF * c + l % ROT_HALF
    other_src = HEAD_DIM * (2 * c + upper) + ROT_DIM + r - rest * upper
    return jnp.where(l < ROPE_LANES, rotary_src, other_src)


def _rope_table_kernel(pos_ref, inv_ref, c_ref, s_ref):
    ang = pos_ref[0] * inv_ref[...]
    lane = lax.broadcasted_iota(jnp.int32, ang.shape, 1)
    c_ref[0] = jnp.where(lane < ROPE_LANES, jnp.cos(ang), 1.0)
    s_ref[0] = jnp.where(lane < ROPE_LANES, jnp.sin(ang), 0.0)


def _rope_tables(positions):
    bn, s_len = positions.shape
    inv = np.float32(ROPE_THETA) ** (-np.arange(0, ROT_DIM, 2, dtype=np.float32) / np.float32(ROT_DIM))
    inv_lane = np.zeros((1, LANES), np.float32)
    inv_lane[0, :ROPE_LANES] = np.tile(inv, ROPE_LANES // ROT_HALF)
    pos = positions.astype(jnp.float32)[..., None]
    ts = TABLE_TS
    tab = jax.ShapeDtypeStruct((bn, s_len, LANES), jnp.float32)
    spec = pl.BlockSpec((1, ts, LANES), lambda b, i: (b, i, 0))
    return pl.pallas_call(
        _rope_table_kernel,
        out_shape=(tab, tab),
        grid=(bn, s_len // ts),
        in_specs=[pl.BlockSpec((1, ts, 1), lambda b, i: (b, i, 0)),
                  pl.BlockSpec((1, LANES), lambda b, i: (0, 0))],
        out_specs=(spec, spec),
        name="rope_tables",
    )(pos, jnp.asarray(inv_lane))


def _ada_kernel(c_ref, w_ref, b_ref, o_ref):
    c = c_ref[...]
    c_act = c * _sigmoid(c)
    o_ref[0] = jnp.dot(c_act, w_ref[0], precision=lax.Precision.HIGHEST,
                       preferred_element_type=jnp.float32) + b_ref[0]


def _ada_all_layers(c, w_ada, b_ada):
    depth = w_ada.shape[0]
    bn = c.shape[0]
    rows = SUBLANES_F32 * pl.cdiv(bn, SUBLANES_F32)
    c_pad = jnp.pad(c, ((0, rows - bn), (0, 0)))
    tn = ADA_TN
    out = pl.pallas_call(
        _ada_kernel,
        out_shape=jax.ShapeDtypeStruct((depth, rows, 3 * D_MODEL), jnp.float32),
        grid=(depth, 3 * D_MODEL // tn),
        in_specs=[pl.BlockSpec((rows, D_MODEL), lambda l, j: (0, 0)),
                  pl.BlockSpec((1, D_MODEL, tn), lambda l, j: (l, 0, j)),
                  pl.BlockSpec((1, 1, tn), lambda l, j: (l, 0, j))],
        out_specs=pl.BlockSpec((1, rows, tn), lambda l, j: (l, 0, j)),
        name="adaln",
    )(c_pad, w_ada, b_ada.reshape(depth, 1, 3 * D_MODEL))
    return out[:, :bn]


def _in_proj_kernel(x_ref, shift_ref, scale_ref, gain_ref, c_ref, s_ref, w_ref, wvt_ref,
                    proj_ref, vt_ref, h_ref):
    j = pl.program_id(2)

    @pl.when(j == 0)
    def _():
        xf = x_ref[0]
        ms = jnp.mean(xf * xf, axis=-1, keepdims=True)
        y = xf * lax.rsqrt(ms + RMS_EPS) * gain_ref[...]
        h = (y * (1.0 + scale_ref[0]) + shift_ref[0]).astype(jnp.bfloat16)
        h_ref[...] = h
        vt = lax.dot_general(wvt_ref[...], h, _NT, preferred_element_type=jnp.float32)
        vt_ref[0] = vt.astype(vt_ref.dtype)

    acc = jnp.dot(h_ref[...], w_ref[...], preferred_element_type=jnp.float32)
    blocks_per_tile = PROJ_TN // HEAD_BLOCK
    cc = c_ref[0]
    ss = s_ref[0]
    for blk in range(blocks_per_tile):
        rotary = (j * blocks_per_tile + blk) < (ROPE_COLS // HEAD_BLOCK)
        cb = jnp.where(rotary, cc, 1.0)
        sb = jnp.where(rotary, ss, 0.0)
        lo0 = blk * HEAD_BLOCK
        hi0 = lo0 + LANES
        lo = acc[:, lo0:hi0]
        hi = acc[:, hi0:hi0 + LANES]
        proj_ref[0, :, lo0:hi0] = (lo * cb - hi * sb).astype(proj_ref.dtype)
        proj_ref[0, :, hi0:hi0 + LANES] = (hi * cb + lo * sb).astype(proj_ref.dtype)


def _in_proj(x, shift, scale, gain, tabs, w_main, w_vt, layer):
    bn, s_len, _ = x.shape
    tm, tn = PROJ_TM, PROJ_TN
    tab_spec = pl.BlockSpec((1, tm, LANES), lambda b, i, j: (b, i, 0))
    mod_spec = pl.BlockSpec((1, 1, D_MODEL), lambda b, i, j: (b, 0, 0))
    return pl.pallas_call(
        _in_proj_kernel,
        out_shape=(jax.ShapeDtypeStruct((bn, s_len, MAIN_COLS), jnp.bfloat16),
                   jax.ShapeDtypeStruct((bn, A_WIDTH, s_len), jnp.bfloat16)),
        grid=(bn, s_len // tm, MAIN_COLS // tn),
        in_specs=[pl.BlockSpec((1, tm, D_MODEL), lambda b, i, j: (b, i, 0)),
                  mod_spec, mod_spec,
                  pl.BlockSpec((1, D_MODEL), lambda b, i, j: (0, 0)),
                  tab_spec, tab_spec,
                  pl.BlockSpec((None, D_MODEL, tn), lambda b, i, j: (layer, 0, j)),
                  pl.BlockSpec((None, A_WIDTH, D_MODEL), lambda b, i, j: (layer, 0, 0))],
        out_specs=(pl.BlockSpec((1, tm, tn), lambda b, i, j: (b, i, j)),
                   pl.BlockSpec((1, A_WIDTH, tm), lambda b, i, j: (b, 0, i))),
        scratch_shapes=[pltpu.VMEM((tm, D_MODEL), jnp.bfloat16)],
        compiler_params=pltpu.CompilerParams(
            dimension_semantics=("parallel", "parallel", "arbitrary"),
            vmem_limit_bytes=VMEM_LIMIT),
        name="in_proj",
    )(x, shift, scale, gain, *tabs, w_main, w_vt)


def _diff_attn_kernel(q_ref, k_ref, vt_ref, dl_ref, gain_ref, o_ref,
                      qm_ref, vta_ref, sa_ref, sb_ref, m_ref, acc_ref, *, lambda_init, s_len):
    h = pl.program_id(1)
    tq = q_ref.shape[1]
    n_chunks = s_len // A_KC

    @pl.when(pl.program_id(2) == 0)
    def _():
        vta_ref[:A_VDIM, :] = vt_ref[0]
        row = lax.broadcasted_iota(jnp.int32, (A_VROWS - A_VDIM, s_len), 0)
        vta_ref[A_VDIM:, :] = jnp.where(row == 0, 1.0, 0.0).astype(vta_ref.dtype)

    q = q_ref[0]
    pair = _rope_layout_head(lax.broadcasted_iota(jnp.int32, q.shape, 1))
    base = 2 * (h % 2)
    for c in range(2):
        qsel = jnp.where(pair == base + c, q, jnp.zeros_like(q)).astype(jnp.float32)
        qm_ref[c] = (qsel * (HEAD_DIM ** -0.5 * LOG2_E)).astype(qm_ref.dtype)
        m_ref[c] = jnp.full((1, tq), MASK_VALUE, jnp.float32)
        acc_ref[c] = jnp.zeros((A_VROWS, tq), jnp.float32)

    def scores(i, s_ref):
        kc = k_ref[0, pl.ds(pl.multiple_of(i * A_KC, A_KC), A_KC), :]
        for c in range(2):
            s_ref[c] = lax.dot_general(kc, qm_ref[c], _NT,
                                       preferred_element_type=jnp.float32)

    def update(i, s_ref):
        vt_c = vta_ref[:, pl.ds(pl.multiple_of(i * A_KC, A_KC), A_KC)]
        for c in range(2):
            st = s_ref[c]
            m_old = m_ref[c]
            m_new = jnp.maximum(m_old, jnp.max(st, axis=0, keepdims=True))
            alpha = jnp.exp2(m_old - m_new)
            e = jnp.exp2(st - m_new).astype(jnp.bfloat16)
            pv = jnp.dot(vt_c, e, preferred_element_type=jnp.float32)
            acc_ref[c] = alpha * acc_ref[c] + pv
            m_ref[c] = m_new

    scores(0, sa_ref)

    def body(j, carry):
        scores(2 * j + 1, sb_ref)
        update(2 * j, sa_ref)
        scores(2 * j + 2, sa_ref)
        update(2 * j + 1, sb_ref)
        return carry

    lax.fori_loop(0, n_chunks // 2 - 1, body, 0)
    scores(n_chunks - 1, sb_ref)
    update(n_chunks - 2, sa_ref)
    update(n_chunks - 1, sb_ref)

    dl = dl_ref[...]
    lam = (jnp.exp(jnp.sum(dl[0:1] * dl[1:2], axis=1, keepdims=True))
           - jnp.exp(jnp.sum(dl[2:3] * dl[3:4], axis=1, keepdims=True)) + lambda_init)
    maps = [acc_ref[c, :A_VDIM, :] * (1.0 / acc_ref[c, A_VDIM:A_VDIM + 1, :]) for c in range(2)]
    ot = maps[0] - lam * maps[1]
    ms = jnp.mean(ot * ot, axis=0, keepdims=True)
    y = ot * lax.rsqrt(ms + RMS_EPS) * gain_ref[...] * (1.0 - lambda_init)
    o_ref[0] = y.T.astype(o_ref.dtype)


def _diff_attention(proj, vt, diff_lambda, subln_gain, lambda_init):
    bn, s_len, _ = proj.shape
    tq = A_TQ
    qblk = COL_AQ // HEAD_BLOCK
    kblk = COL_AK // HEAD_BLOCK
    assert s_len % (2 * A_KC) == 0
    return pl.pallas_call(
        functools.partial(_diff_attn_kernel, lambda_init=lambda_init, s_len=s_len),
        out_shape=jax.ShapeDtypeStruct((bn, s_len, A_WIDTH), jnp.bfloat16),
        grid=(bn, A_HEADS, s_len // tq),
        in_specs=[pl.BlockSpec((1, tq, HEAD_BLOCK), lambda b, h, i: (b, i, qblk + h // 2)),
                  pl.BlockSpec((1, s_len, HEAD_BLOCK), lambda b, h, i: (b, 0, kblk + h // 2)),
                  pl.BlockSpec((1, A_VDIM, s_len), lambda b, h, i: (b, h, 0)),
                  pl.BlockSpec((4, HEAD_DIM), lambda b, h, i: (0, 0)),
                  pl.BlockSpec((A_VDIM, 1), lambda b, h, i: (0, 0))],
        out_specs=pl.BlockSpec((1, tq, A_VDIM), lambda b, h, i: (b, i, h)),
        scratch_shapes=[pltpu.VMEM((2, tq, HEAD_BLOCK), jnp.bfloat16),
                        pltpu.VMEM((A_VROWS, s_len), jnp.bfloat16),
                        pltpu.VMEM((2, A_KC, tq), jnp.float32),
                        pltpu.VMEM((2, A_KC, tq), jnp.float32),
                        pltpu.VMEM((2, 1, tq), jnp.float32),
                        pltpu.VMEM((2, A_VROWS, tq), jnp.float32)],
        compiler_params=pltpu.CompilerParams(
            dimension_semantics=("parallel", "parallel", "arbitrary"),
            vmem_limit_bytes=VMEM_LIMIT),
        name="diff_attn",
    )(proj, proj, vt, diff_lambda, subln_gain.reshape(A_VDIM, 1))


def _head_lane_masks(shape, rotary_layout=False):
    col = lax.broadcasted_iota(jnp.int32, shape, 1)
    head = _rope_layout_head(col) if rotary_layout else col // HEAD_DIM
    return [head == h for h in range(HEAD_BLOCK // HEAD_DIM)]


def _stack_heads(q, qmask, scale):
    q = (q.astype(jnp.float32) * scale).astype(q.dtype)
    zero = jnp.zeros_like(q)
    return jnp.concatenate([jnp.where(mk, q, zero) for mk in qmask], axis=0)


def _unstack_heads(x, qmask, tq):
    out = x[:tq]
    for h in range(1, len(qmask)):
        out = jnp.where(qmask[h], x[h * tq:(h + 1) * tq], out)
    return out


def _band_group(q_ref, k_ref, v_ref, y_ref, band_ref, stage_ref, tmp_ref, qd_ref, kd_ref, vd_ref,
                run_o_ref, run_l_ref, *, dilation, s_len, first, last):
    length = s_len // dilation
    tq = min(B_TQ_MAX, length)
    win = min(tq + 2 * B_REACH, length)
    unroll = min(B_UNROLL_MAX, s_len // tq)
    nblk = length // tq
    assert length % tq == 0 and (s_len // tq) % unroll == 0 and dilation in (1, 4, 16)
    assert not (last and dilation > 1)
    qmask = _head_lane_masks((tq, HEAD_BLOCK), rotary_layout=True)
    vmask = _head_lane_masks((tq, HEAD_BLOCK))

    if dilation > 1:
        for src, dst in ((q_ref, qd_ref), (k_ref, kd_ref), (v_ref, vd_ref)):
            for slab in range(HEAD_BLOCK // LANES):
                lanes = slice(slab * LANES, (slab + 1) * LANES)
                stage_ref[...] = src[0, :, lanes].astype(jnp.float32)
                for m4 in range(4):
                    quarter = stage_ref[pl.ds(m4, s_len // 4, stride=4), :]
                    if dilation == 4:
                        dst[m4 * length:(m4 + 1) * length, lanes] = quarter.astype(dst.dtype)
                    else:
                        tmp_ref[...] = quarter
                        for mm in range(4):
                            m = m4 + 4 * mm
                            dst[m * length:(m + 1) * length, lanes] = tmp_ref[
                                pl.ds(mm, length, stride=4), :].astype(dst.dtype)
        q_src, k_src, v_src = qd_ref, kd_ref, vd_ref
    else:
        q_src, k_src, v_src = q_ref.at[0], k_ref.at[0], v_ref.at[0]

    def one_block(t):
        m = t // nblk
        q0 = pl.multiple_of((t % nblk) * tq, tq)
        ks = pl.multiple_of(jnp.clip(q0 - B_REACH, 0, length - win), B_REACH)
        base = pl.multiple_of(m * length, tq)
        if dilation > 1:
            rows = pl.ds(q0 * dilation + m, tq, stride=dilation)
        else:
            rows = pl.ds(q0, tq)
        q = q_src[pl.ds(base + q0, tq), :]
        kw = k_src[pl.ds(base + ks, win), :]
        vw = v_src[pl.ds(base + ks, win), :]
        qs = _stack_heads(q, qmask, HEAD_DIM ** -0.5 * LOG2_E)
        s = lax.dot_general(qs, kw, _NT, preferred_element_type=jnp.float32)
        s = s + band_ref[(q0 - ks) // B_REACH]
        mx = jnp.max(s, axis=-1, keepdims=True)
        e = jnp.exp2(s - mx)
        den = jnp.sum(e, axis=-1, keepdims=True)
        o = jnp.dot(e.astype(jnp.bfloat16), vw, preferred_element_type=jnp.float32)
        o = o * (1.0 / den)
        lse = jnp.broadcast_to(mx + jnp.log2(den), o.shape)
        o_new = _unstack_heads(o, vmask, tq)
        l_new = _unstack_heads(lse, vmask, tq)
        for slab in range(HEAD_BLOCK // LANES):
            lanes = slice(slab * LANES, (slab + 1) * LANES)
            o_s, l_s = o_new[:, lanes], l_new[:, lanes]
            if not first:
                o_run, l_run = run_o_ref[slab, rows, :], run_l_ref[slab, rows, :]
                l_max = jnp.maximum(l_run, l_s)
                w_run, w_new = jnp.exp2(l_run - l_max), jnp.exp2(l_s - l_max)
                total = w_run + w_new
                o_s = (o_run * w_run + o_s * w_new) * (1.0 / total)
                l_s = l_max + jnp.log2(total)
            if last:
                y_ref[0, rows, lanes] = o_s.astype(y_ref.dtype)
            else:
                run_o_ref[slab, rows, :] = o_s
                run_l_ref[slab, rows, :] = l_s

    def body(i, carry):
        for u in range(unroll):
            one_block(i * unroll + u)
        return carry

    lax.fori_loop(0, dilation * nblk // unroll, body, 0)


def _band_mix_kernel(q_ref, k_ref, v_ref, y_ref, band_ref, *scratch, s_len, dilations):
    g = pl.program_id(1)

    tq = band_ref.shape[1] // B_HEADS
    row = lax.broadcasted_iota(jnp.int32, band_ref.shape[1:], 0) % tq
    col = lax.broadcasted_iota(jnp.int32, band_ref.shape[1:], 1)
    for case in range(B_CASES):
        valid = jnp.abs(col - case * B_REACH - row) <= B_REACH
        band_ref[case] = jnp.where(valid, 0.0, MASK_VALUE)

    for step, dilation in enumerate(dilations):
        @pl.when(g == step)
        def _(step=step, dilation=dilation):
            _band_group(q_ref, k_ref, v_ref, y_ref, band_ref, *scratch, dilation=dilation,
                        s_len=s_len, first=(step == 0), last=(step == len(dilations) - 1))


def _band_mixture(proj):
    bn, s_len, _ = proj.shape
    n_groups = len(B_PATTERNS)
    order = sorted(range(n_groups), key=lambda g: -B_PATTERNS[g][1])
    dilations = tuple(B_PATTERNS[g][1] for g in order)
    for window, dilation in B_PATTERNS:
        assert (window // 2) // dilation == B_REACH
    assert order == list(range(n_groups - 1, -1, -1))

    def in_spec(col0):
        blk = col0 // HEAD_BLOCK
        return pl.BlockSpec((1, s_len, HEAD_BLOCK), lambda b, g: (b, 0, blk + n_groups - 1 - g))

    slabs = HEAD_BLOCK // LANES
    regrouped = pltpu.VMEM((s_len, HEAD_BLOCK), jnp.bfloat16)
    state = pltpu.VMEM((slabs, s_len, LANES), jnp.float32)
    return pl.pallas_call(
        functools.partial(_band_mix_kernel, s_len=s_len, dilations=dilations),
        out_shape=jax.ShapeDtypeStruct((bn, s_len, B_WIDTH), jnp.bfloat16),
        grid=(bn, n_groups),
        in_specs=[in_spec(COL_BQ), in_spec(COL_BK), in_spec(COL_BV)],
        out_specs=pl.BlockSpec((1, s_len, B_WIDTH), lambda b, g: (b, 0, 0)),
        scratch_shapes=[pltpu.VMEM((B_CASES, B_HEADS * B_TQ_MAX, B_TQ_MAX + 2 * B_REACH), jnp.float32),
                        pltpu.VMEM((s_len, LANES), jnp.float32),
                        pltpu.VMEM((s_len // 4, LANES), jnp.float32),
                        regrouped, regrouped, regrouped,
                        state, state],
        compiler_params=pltpu.CompilerParams(
            dimension_semantics=("parallel", "arbitrary"),
            vmem_limit_bytes=VMEM_LIMIT),
        name="band_mix",
    )(proj, proj, proj)


def _na_table_kernel(rpb_ref, tab_ref, *, rows):
    h = pl.program_id(0)
    n_dr = 2 * NA_KH - 1
    n_dc = 2 * NA_KW - 1
    shape = (GRID_W, LANES)
    qc = lax.broadcasted_iota(jnp.int32, shape, 0)
    lane = lax.broadcasted_iota(jnp.int32, shape, 1)
    kc = lane % GRID_W
    dc_i = jnp.clip(kc - qc + (NA_KW - 1), 0, n_dc - 1)
    cs = jnp.clip(qc - NA_KW // 2, 0, GRID_W - NA_KW)
    col_valid = (kc >= cs) & (kc < cs + NA_KW)
    masked = jnp.full(shape, MASK_VALUE, jnp.float32)

    row_bias = []
    for dr in range(n_dr):
        base = (h * n_dr + dr) * n_dc
        t = masked
        for dc in range(n_dc):
            t = jnp.where(dc_i == dc, rpb_ref[base + dc], t)
        row_bias.append(jnp.where(col_valid, t, MASK_VALUE))

    pair_rep = (0, 1, 2, rows // 2 - 2, rows // 2 - 1)
    for case in range(C_CASES):
        ip = pair_rep[case]
        ws = min(max(2 * ip - NA_KH // 2, 0), rows - C_KROWS)
        for qr_l in range(2):
            r = 2 * ip + qr_l
            rs = min(max(r - NA_KH // 2, 0), rows - NA_KH)
            for tile in range(C_WIN // LANES):
                halves = []
                for kr in (ws + 2 * tile, ws + 2 * tile + 1):
                    halves.append(row_bias[kr - r + NA_KH - 1] if rs <= kr < rs + NA_KH else masked)
                tab_ref[case, 0, qr_l * GRID_W:(qr_l + 1) * GRID_W, tile * LANES:(tile + 1) * LANES] = (
                    jnp.where(lane < GRID_W, halves[0], halves[1]))


def _na_tables(rpb, rows):
    flat = rpb.reshape(-1)
    return pl.pallas_call(
        functools.partial(_na_table_kernel, rows=rows),
        out_shape=jax.ShapeDtypeStruct((C_CASES, C_HEADS, C_PAIR, C_WIN), jnp.float32),
        grid=(C_HEADS,),
        in_specs=[pl.BlockSpec(memory_space=pltpu.SMEM)],
        out_specs=pl.BlockSpec((C_CASES, 1, C_PAIR, C_WIN), lambda h: (0, h, 0, 0)),
        name="na_tables",
    )(flat)


def _na_kernel(q_ref, k_ref, v_ref, tab_ref, o_ref, *, rows):
    rb = pl.program_id(1)
    qmask = _head_lane_masks((C_PAIR, HEAD_BLOCK))
    pairs_per_step = C_RB // 2
    for t in range(pairs_per_step):
        ip = rb * pairs_per_step + t
        ws = jnp.clip(2 * ip - NA_KH // 2, 0, rows - C_KROWS)
        case = ip - ws // 2
        k0 = pl.multiple_of(ws * GRID_W, C_PAIR)
        q = q_ref[0, t * C_PAIR:(t + 1) * C_PAIR, :]
        kw = k_ref[0, pl.ds(k0, C_WIN), :]
        vw = v_ref[0, pl.ds(k0, C_WIN), :]
        qs = _stack_heads(q, qmask, HEAD_DIM ** -0.5)
        s = lax.dot_general(qs, kw, _NT, preferred_element_type=jnp.float32)
        s = s + tab_ref[case]
        mx = jnp.max(s, axis=-1, keepdims=True)
        e = jnp.exp(s - mx)
        den = jnp.sum(e, axis=-1, keepdims=True)
        o = jnp.dot(e.astype(jnp.bfloat16), vw, preferred_element_type=jnp.float32)
        o_acc = _unstack_heads(o * (1.0 / den), qmask, C_PAIR)
        o_ref[0, t * C_PAIR:(t + 1) * C_PAIR, :] = o_acc.astype(o_ref.dtype)


def _neighborhood_attention(proj, tables):
    bn, s_len, _ = proj.shape
    rows = s_len // GRID_W
    tq = C_RB * GRID_W
    qblk = COL_CQ // HEAD_BLOCK
    kblk = COL_CK // HEAD_BLOCK
    vblk = COL_CV // HEAD_BLOCK
    return pl.pallas_call(
        functools.partial(_na_kernel, rows=rows),
        out_shape=jax.ShapeDtypeStruct((bn, s_len, C_WIDTH), jnp.bfloat16),
        grid=(bn, rows // C_RB),
        in_specs=[pl.BlockSpec((1, tq, HEAD_BLOCK), lambda b, i: (b, i, qblk)),
                  pl.BlockSpec((1, s_len, HEAD_BLOCK), lambda b, i: (b, 0, kblk)),
                  pl.BlockSpec((1, s_len, HEAD_BLOCK), lambda b, i: (b, 0, vblk)),
                  pl.BlockSpec((C_CASES, C_HEADS * C_PAIR, C_WIN), lambda b, i: (0, 0, 0))],
        out_specs=pl.BlockSpec((1, tq, C_WIDTH), lambda b, i: (b, i, 0)),
        compiler_params=pltpu.CompilerParams(
            dimension_semantics=("parallel", "arbitrary"),
            vmem_limit_bytes=VMEM_LIMIT),
        name="na_attn",
    )(proj, proj, proj, tables.reshape(C_CASES, C_HEADS * C_PAIR, C_WIN))


def _merge_kernel(x_ref, gate_ref, ya_ref, yb_ref, yc_ref, z_ref, g0_ref, g1_ref, g2_ref,
                  wb_ref, wo_ref, fg_ref, o_ref, *, final_norm):
    z = z_ref[0].astype(jnp.float32)
    sz = z * _sigmoid(z)
    ya = (ya_ref[0].astype(jnp.float32) * sz[:, :A_WIDTH]).astype(jnp.bfloat16)
    ybz = (yb_ref[0].astype(jnp.float32) * sz[:, A_WIDTH:A_WIDTH + B_WIDTH]).astype(jnp.bfloat16)
    ycz = (yc_ref[0].astype(jnp.float32) * sz[:, A_WIDTH + B_WIDTH:]).astype(jnp.bfloat16)

    pa = jnp.dot(ya, wb_ref[:A_WIDTH, :], preferred_element_type=jnp.float32)
    pb = jnp.dot(ybz, wb_ref[A_WIDTH:A_WIDTH + B_WIDTH, :], preferred_element_type=jnp.float32)
    pc = jnp.dot(ycz, wb_ref[A_WIDTH + B_WIDTH:, :], preferred_element_type=jnp.float32)
    merged = _sigmoid(g0_ref[0].astype(jnp.float32)) * pa
    merged = merged + _sigmoid(g1_ref[0].astype(jnp.float32)) * pb
    merged = merged + _sigmoid(g2_ref[0].astype(jnp.float32)) * pc
    out = jnp.dot(merged.astype(jnp.bfloat16), wo_ref[...], preferred_element_type=jnp.float32)
    xn = x_ref[0] + gate_ref[0] * out
    if final_norm:
        ms = jnp.mean(xn * xn, axis=-1, keepdims=True)
        xn = xn * lax.rsqrt(ms + RMS_EPS) * fg_ref[...]
    o_ref[0] = xn


def _merge(x, gate, ya, yb, yc, proj, w_branch, w_out, final_gain, layer, final_norm):
    bn, s_len, _ = x.shape
    tm = MERGE_TM
    zblk = COL_Z // D_MODEL
    gblk = COL_G // D_MODEL

    def tok(width, blk=0):
        return pl.BlockSpec((1, tm, width), lambda b, i: (b, i, blk))

    return pl.pallas_call(
        functools.partial(_merge_kernel, final_norm=final_norm),
        out_shape=jax.ShapeDtypeStruct((bn, s_len, D_MODEL), jnp.float32),
        grid=(bn, s_len // tm),
        in_specs=[tok(D_MODEL),
                  pl.BlockSpec((1, 1, D_MODEL), lambda b, i: (b, 0, 0)),
                  tok(A_WIDTH), tok(B_WIDTH), tok(C_WIDTH),
                  tok(D_MODEL, zblk), tok(D_MODEL, gblk), tok(D_MODEL, gblk + 1), tok(D_MODEL, gblk + 2),
                  pl.BlockSpec((None, BR_WIDTH, D_MODEL), lambda b, i: (layer, 0, 0)),
                  pl.BlockSpec((None, D_MODEL, D_MODEL), lambda b, i: (layer, 0, 0)),
                  pl.BlockSpec((1, D_MODEL), lambda b, i: (0, 0))],
        out_specs=tok(D_MODEL),
        compiler_params=pltpu.CompilerParams(
            dimension_semantics=("parallel", "parallel"),
            vmem_limit_bytes=VMEM_LIMIT),
        name="merge",
    )(x, gate, ya, yb, yc, proj, proj, proj, proj,
      w_branch, w_out, final_gain)


def _w_main_kernel(w_ref, o_ref):
    d = pl.program_id(1)
    w = w_ref[...].astype(jnp.bfloat16)

    @pl.when(d < ROPE_COLS // W_PREP_COLS)
    def _():
        src = lax.broadcasted_iota(jnp.int32, (HEAD_BLOCK, HEAD_BLOCK), 0)
        dst = lax.broadcasted_iota(jnp.int32, (HEAD_BLOCK, HEAD_BLOCK), 1)
        onehot = jnp.where(src == _rope_layout_source(dst), 1.0, 0.0).astype(jnp.bfloat16)
        for blk in range(W_PREP_COLS // HEAD_BLOCK):
            cols = slice(blk * HEAD_BLOCK, (blk + 1) * HEAD_BLOCK)
            o_ref[:, cols] = jnp.dot(w[:, cols], onehot,
                                     preferred_element_type=jnp.float32).astype(o_ref.dtype)

    @pl.when(d >= ROPE_COLS // W_PREP_COLS)
    def _():
        o_ref[...] = w


def _w_vt_kernel(w_ref, o_ref):
    w = w_ref[...].astype(jnp.bfloat16)
    r = lax.broadcasted_iota(jnp.int32, (A_WIDTH, A_WIDTH), 0)
    c = lax.broadcasted_iota(jnp.int32, (A_WIDTH, A_WIDTH), 1)
    eye = jnp.where(r == c, 1.0, 0.0).astype(jnp.bfloat16)
    o_ref[...] = lax.dot_general(eye, w, _NT, preferred_element_type=jnp.float32).astype(o_ref.dtype)


def _split_w_in(w_in):
    depth = w_in.shape[0]
    av_blk = 2 * A_QK_COLS // W_PREP_COLS
    assert 2 * A_QK_COLS % W_PREP_COLS == 0 and A_WIDTH == W_PREP_COLS and ROPE_COLS % W_PREP_COLS == 0
    w_main = pl.pallas_call(
        _w_main_kernel,
        out_shape=jax.ShapeDtypeStruct((depth, D_MODEL, MAIN_COLS), jnp.bfloat16),
        grid=(depth, MAIN_COLS // W_PREP_COLS),
        in_specs=[pl.BlockSpec((None, D_MODEL, W_PREP_COLS),
                               lambda l, d: (l, 0, jnp.where(d < av_blk, d, d + 1)))],
        out_specs=pl.BlockSpec((None, D_MODEL, W_PREP_COLS), lambda l, d: (l, 0, d)),
        name="w_main_prep",
    )(w_in)
    w_vt = pl.pallas_call(
        _w_vt_kernel,
        out_shape=jax.ShapeDtypeStruct((depth, A_WIDTH, D_MODEL), jnp.bfloat16),
        grid=(depth,),
        in_specs=[pl.BlockSpec((None, D_MODEL, A_WIDTH), lambda l: (l, 0, av_blk))],
        out_specs=pl.BlockSpec((None, A_WIDTH, D_MODEL), lambda l: (l, 0, 0)),
        name="w_vt_prep",
    )(w_in)
    return w_main, w_vt


def kernel(x, c, positions, norm_gain, w_ada, b_ada, w_in, diff_lambda, diff_subln_gain, na_rpb,
           w_branch, w_out, final_gain):
    depth = w_in.shape[0]
    bn, s_len, _ = x.shape
    rows = s_len // GRID_W
    tabs = _rope_tables(positions)
    ada = _ada_all_layers(c, w_ada, b_ada)
    fg = final_gain.reshape(1, D_MODEL)
    w_main, w_vt = _split_w_in(w_in)
    w_branch = w_branch.astype(jnp.bfloat16)
    w_out = w_out.astype(jnp.bfloat16)
    for layer in range(depth):
        shift = ada[layer, :, None, :D_MODEL]
        scale = ada[layer, :, None, D_MODEL:2 * D_MODEL]
        gate = ada[layer, :, None, 2 * D_MODEL:]
        proj, vt = _in_proj(x, shift, scale, norm_gain[layer].reshape(1, D_MODEL), tabs, w_main, w_vt,
                            layer)
        lambda_init = 0.8 - 0.6 * math.exp(-0.3 * layer)
        ya = _diff_attention(proj, vt, diff_lambda[layer], diff_subln_gain[layer], lambda_init)
        yb = _band_mixture(proj)
        yc = _neighborhood_attention(proj, _na_tables(na_rpb[layer], rows))
        x = _merge(x, gate, ya, yb, yc, proj, w_branch, w_out, fg, layer,
                   final_norm=(layer == depth - 1))
    return x
```

```python
import functools
import math

import numpy as np
import jax
import jax.numpy as jnp
from jax import lax
from jax.experimental import pallas as pl
from jax.experimental.pallas import tpu as pltpu

D_MODEL = 1024
HEAD_DIM = 64
ROT_DIM = HEAD_DIM // 4
ROT_HALF = ROT_DIM // 2
ROPE_THETA = 500000.0
RMS_EPS = 1e-6
MASK_VALUE = -1e30
LOG2_E = math.log2(math.e)

A_HEADS = 4
A_VDIM = 2 * HEAD_DIM
A_WIDTH = A_HEADS * A_VDIM
A_QK_COLS = A_HEADS * 2 * HEAD_DIM

B_PATTERNS = ((128, 1), (512, 4), (2048, 16))
B_GROUPS = 3
B_HEADS = 4
B_WIDTH = B_HEADS * HEAD_DIM
B_COLS = B_GROUPS * B_WIDTH

GRID_W = 64
C_HEADS = 4
NA_KH = 8
NA_KW = 16
C_WIDTH = C_HEADS * HEAD_DIM

N_BRANCHES = 3
BR_WIDTH = A_WIDTH + B_WIDTH + C_WIDTH

LANES = 128
SUBLANES_F32 = 8
SUBLANES_BF16 = 16
HEAD_BLOCK = 4 * HEAD_DIM
ROPE_LANES = (HEAD_BLOCK // HEAD_DIM) * ROT_HALF
VMEM_LIMIT = 56 * 1024 * 1024

COL_AQ = 0
COL_AK = COL_AQ + A_QK_COLS
COL_BQ = COL_AK + A_QK_COLS
COL_BK = COL_BQ + B_COLS
COL_BV = COL_BK + B_COLS
COL_CQ = COL_BV + B_COLS
COL_CK = COL_CQ + C_WIDTH
COL_CV = COL_CK + C_WIDTH
COL_Z = COL_CV + C_WIDTH
COL_G = COL_Z + BR_WIDTH
MAIN_COLS = COL_G + N_BRANCHES * D_MODEL
ATTN_COLS = COL_Z
GATE_COLS = MAIN_COLS - ATTN_COLS
ROPE_COLS = COL_BV

W_PREP_COLS = 512
PROJ_TM = 2048
PROJ_TN = 1024
A_TQ = 2048
A_KC = 512
A_VROWS = A_VDIM + SUBLANES_BF16
TABLE_TS = 1024
ADA_TN = 1024
B_TQ_MAX = 128
B_UNROLL_MAX = 8
B_REACH = 64
B_CASES = 3
C_PAIR = 2 * GRID_W
C_KROWS = 10
C_WIN = C_KROWS * GRID_W
C_CASES = 5
C_RB = 32
MERGE_TM = 512

_NT = (((1,), (1,)), ((), ()))


def _sigmoid(x):
    return 1.0 / (1.0 + jnp.exp2(x * (-LOG2_E)))


def _rope_layout_head(col):
    l = col % LANES
    c = col // LANES
    rest = HEAD_DIM - ROT_DIM
    return jnp.where(l < ROPE_LANES, l // ROT_HALF, 2 * c + jnp.where(l >= ROPE_LANES + rest, 1, 0))


def _rope_layout_source(col):
    l = col % LANES
    c = col // LANES
    rest = HEAD_DIM - ROT_DIM
    r = l - ROPE_LANES
    upper = jnp.where(r >= rest, 1, 0)
    rotary_src = HEAD_DIM * (l // ROT_HALF) + ROT_HALF * c + l % ROT_HALF
    other_src = HEAD_DIM * (2 * c + upper) + ROT_DIM + r - rest * upper
    return jnp.where(l < ROPE_LANES, rotary_src, other_src)


def _rope_table_kernel(pos_ref, inv_ref, c_ref, s_ref):
    ang = pos_ref[0] * inv_ref[...]
    lane = lax.broadcasted_iota(jnp.int32, ang.shape, 1)
    c_ref[0] = jnp.where(lane < ROPE_LANES, jnp.cos(ang), 1.0)
    s_ref[0] = jnp.where(lane < ROPE_LANES, jnp.sin(ang), 0.0)


def _rope_tables(positions):
    bn, s_len = positions.shape
    inv = np.float32(ROPE_THETA) ** (-np.arange(0, ROT_DIM, 2, dtype=np.float32) / np.float32(ROT_DIM))
    inv_lane = np.zeros((1, LANES), np.float32)
    inv_lane[0, :ROPE_LANES] = np.tile(inv, ROPE_LANES // ROT_HALF)
    pos = positions.astype(jnp.float32)[..., None]
    ts = TABLE_TS
    tab = jax.ShapeDtypeStruct((bn, s_len, LANES), jnp.float32)
    spec = pl.BlockSpec((1, ts, LANES), lambda b, i: (b, i, 0))
    return pl.pallas_call(
        _rope_table_kernel,
        out_shape=(tab, tab),
        grid=(bn, s_len // ts),
        in_specs=[pl.BlockSpec((1, ts, 1), lambda b, i: (b, i, 0)),
                  pl.BlockSpec((1, LANES), lambda b, i: (0, 0))],
        out_specs=(spec, spec),
        name="rope_tables",
    )(pos, jnp.asarray(inv_lane))


def _ada_kernel(c_ref, w_ref, b_ref, o_ref):
    c = c_ref[...]
    c_act = c * _sigmoid(c)
    o_ref[0] = jnp.dot(c_act, w_ref[0], precision=lax.Precision.HIGHEST,
                       preferred_element_type=jnp.float32) + b_ref[0]


def _ada_all_layers(c, w_ada, b_ada):
    depth = w_ada.shape[0]
    bn = c.shape[0]
    rows = SUBLANES_F32 * pl.cdiv(bn, SUBLANES_F32)
    c_pad = jnp.pad(c, ((0, rows - bn), (0, 0)))
    tn = ADA_TN
    out = pl.pallas_call(
        _ada_kernel,
        out_shape=jax.ShapeDtypeStruct((depth, rows, 3 * D_MODEL), jnp.float32),
        grid=(depth, 3 * D_MODEL // tn),
        in_specs=[pl.BlockSpec((rows, D_MODEL), lambda l, j: (0, 0)),
                  pl.BlockSpec((1, D_MODEL, tn), lambda l, j: (l, 0, j)),
                  pl.BlockSpec((1, 1, tn), lambda l, j: (l, 0, j))],
        out_specs=pl.BlockSpec((1, rows, tn), lambda l, j: (l, 0, j)),
        name="adaln",
    )(c_pad, w_ada, b_ada.reshape(depth, 1, 3 * D_MODEL))
    return out[:, :bn]


def _in_proj_kernel(x_ref, shift_ref, scale_ref, gain_ref, c_ref, s_ref, w_ref, wvt_ref,
                    proj_ref, vt_ref, h_ref):
    j = pl.program_id(2)

    @pl.when(j == 0)
    def _():
        xf = x_ref[0]
        ms = jnp.mean(xf * xf, axis=-1, keepdims=True)
        y = xf * lax.rsqrt(ms + RMS_EPS) * gain_ref[...]
        h = (y * (1.0 + scale_ref[0]) + shift_ref[0]).astype(jnp.bfloat16)
        h_ref[0] = h
        vt = lax.dot_general(wvt_ref[...], h, _NT, preferred_element_type=jnp.float32)
        vt_ref[0] = vt.astype(vt_ref.dtype)

    acc = jnp.dot(h_ref[0], w_ref[...], preferred_element_type=jnp.float32)
    blocks_per_tile = PROJ_TN // HEAD_BLOCK
    cc = c_ref[0]
    ss = s_ref[0]
    for blk in range(blocks_per_tile):
        rotary = (j * blocks_per_tile + blk) < (ROPE_COLS // HEAD_BLOCK)
        cb = jnp.where(rotary, cc, 1.0)
        sb = jnp.where(rotary, ss, 0.0)
        lo0 = blk * HEAD_BLOCK
        hi0 = lo0 + LANES
        lo = acc[:, lo0:hi0]
        hi = acc[:, hi0:hi0 + LANES]
        proj_ref[0, :, lo0:hi0] = (lo * cb - hi * sb).astype(proj_ref.dtype)
        proj_ref[0, :, hi0:hi0 + LANES] = (hi * cb + lo * sb).astype(proj_ref.dtype)


def _in_proj(x, shift, scale, gain, tabs, w_main, w_vt, layer):
    bn, s_len, _ = x.shape
    tm, tn = PROJ_TM, PROJ_TN
    tab_spec = pl.BlockSpec((1, tm, LANES), lambda b, i, j: (b, i, 0))
    mod_spec = pl.BlockSpec((1, 1, D_MODEL), lambda b, i, j: (b, 0, 0))
    return pl.pallas_call(
        _in_proj_kernel,
        out_shape=(jax.ShapeDtypeStruct((bn, s_len, ATTN_COLS), jnp.bfloat16),
                   jax.ShapeDtypeStruct((bn, A_WIDTH, s_len), jnp.bfloat16),
                   jax.ShapeDtypeStruct((bn, s_len, D_MODEL), jnp.bfloat16)),
        grid=(bn, s_len // tm, ATTN_COLS // tn),
        in_specs=[pl.BlockSpec((1, tm, D_MODEL), lambda b, i, j: (b, i, 0)),
                  mod_spec, mod_spec,
                  pl.BlockSpec((1, D_MODEL), lambda b, i, j: (0, 0)),
                  tab_spec, tab_spec,
                  pl.BlockSpec((None, D_MODEL, tn), lambda b, i, j: (layer, 0, j)),
                  pl.BlockSpec((None, A_WIDTH, D_MODEL), lambda b, i, j: (layer, 0, 0))],
        out_specs=(pl.BlockSpec((1, tm, tn), lambda b, i, j: (b, i, j)),
                   pl.BlockSpec((1, A_WIDTH, tm), lambda b, i, j: (b, 0, i)),
                   pl.BlockSpec((1, tm, D_MODEL), lambda b, i, j: (b, i, 0))),
        compiler_params=pltpu.CompilerParams(
            dimension_semantics=("parallel", "parallel", "arbitrary"),
            vmem_limit_bytes=VMEM_LIMIT),
        name="in_proj",
    )(x, shift, scale, gain, *tabs, w_main, w_vt)


def _diff_attn_kernel(q_ref, k_ref, vt_ref, dl_ref, gain_ref, o_ref,
                      qm_ref, vta_ref, sa_ref, sb_ref, m_ref, acc_ref, *, lambda_init, s_len):
    h = pl.program_id(1)
    tq = q_ref.shape[1]
    n_chunks = s_len // A_KC

    @pl.when(pl.program_id(2) == 0)
    def _():
        vta_ref[:A_VDIM, :] = vt_ref[0]
        row = lax.broadcasted_iota(jnp.int32, (A_VROWS - A_VDIM, s_len), 0)
        vta_ref[A_VDIM:, :] = jnp.where(row == 0, 1.0, 0.0).astype(vta_ref.dtype)

    q = q_ref[0]
    pair = _rope_layout_head(lax.broadcasted_iota(jnp.int32, q.shape, 1))
    base = 2 * (h % 2)
    for c in range(2):
        qsel = jnp.where(pair == base + c, q, jnp.zeros_like(q)).astype(jnp.float32)
        qm_ref[c] = (qsel * (HEAD_DIM ** -0.5 * LOG2_E)).astype(qm_ref.dtype)
        m_ref[c] = jnp.full((1, tq), MASK_VALUE, jnp.float32)
        acc_ref[c] = jnp.zeros((A_VROWS, tq), jnp.float32)

    def scores(i, s_ref):
        kc = k_ref[0, pl.ds(pl.multiple_of(i * A_KC, A_KC), A_KC), :]
        for c in range(2):
            s_ref[c] = lax.dot_general(kc, qm_ref[c], _NT,
                                       preferred_element_type=jnp.float32)

    def update(i, s_ref):
        vt_c = vta_ref[:, pl.ds(pl.multiple_of(i * A_KC, A_KC), A_KC)]
        for c in range(2):
            st = s_ref[c]
            m_old = m_ref[c]
            m_new = jnp.maximum(m_old, jnp.max(st, axis=0, keepdims=True))
            alpha = jnp.exp2(m_old - m_new)
            e = jnp.exp2(st - m_new).astype(jnp.bfloat16)
            pv = jnp.dot(vt_c, e, preferred_element_type=jnp.float32)
            acc_ref[c] = alpha * acc_ref[c] + pv
            m_ref[c] = m_new

    scores(0, sa_ref)

    def body(j, carry):
        scores(2 * j + 1, sb_ref)
        update(2 * j, sa_ref)
        scores(2 * j + 2, sa_ref)
        update(2 * j + 1, sb_ref)
        return carry

    lax.fori_loop(0, n_chunks // 2 - 1, body, 0)
    scores(n_chunks - 1, sb_ref)
    update(n_chunks - 2, sa_ref)
    update(n_chunks - 1, sb_ref)

    dl = dl_ref[...]
    lam = (jnp.exp(jnp.sum(dl[0:1] * dl[1:2], axis=1, keepdims=True))
           - jnp.exp(jnp.sum(dl[2:3] * dl[3:4], axis=1, keepdims=True)) + lambda_init)
    maps = [acc_ref[c, :A_VDIM, :] * (1.0 / acc_ref[c, A_VDIM:A_VDIM + 1, :]) for c in range(2)]
    ot = maps[0] - lam * maps[1]
    ms = jnp.mean(ot * ot, axis=0, keepdims=True)
    y = ot * lax.rsqrt(ms + RMS_EPS) * gain_ref[...] * (1.0 - lambda_init)
    o_ref[0] = y.T.astype(o_ref.dtype)


def _diff_attention(proj, vt, diff_lambda, subln_gain, lambda_init):
    bn, s_len, _ = proj.shape
    tq = A_TQ
    qblk = COL_AQ // HEAD_BLOCK
    kblk = COL_AK // HEAD_BLOCK
    assert s_len % (2 * A_KC) == 0
    return pl.pallas_call(
        functools.partial(_diff_attn_kernel, lambda_init=lambda_init, s_len=s_len),
        out_shape=jax.ShapeDtypeStruct((bn, s_len, A_WIDTH), jnp.bfloat16),
        grid=(bn, A_HEADS, s_len // tq),
        in_specs=[pl.BlockSpec((1, tq, HEAD_BLOCK), lambda b, h, i: (b, i, qblk + h // 2)),
                  pl.BlockSpec((1, s_len, HEAD_BLOCK), lambda b, h, i: (b, 0, kblk + h // 2)),
                  pl.BlockSpec((1, A_VDIM, s_len), lambda b, h, i: (b, h, 0)),
                  pl.BlockSpec((4, HEAD_DIM), lambda b, h, i: (0, 0)),
                  pl.BlockSpec((A_VDIM, 1), lambda b, h, i: (0, 0))],
        out_specs=pl.BlockSpec((1, tq, A_VDIM), lambda b, h, i: (b, i, h)),
        scratch_shapes=[pltpu.VMEM((2, tq, HEAD_BLOCK), jnp.bfloat16),
                        pltpu.VMEM((A_VROWS, s_len), jnp.bfloat16),
                        pltpu.VMEM((2, A_KC, tq), jnp.float32),
                        pltpu.VMEM((2, A_KC, tq), jnp.float32),
                        pltpu.VMEM((2, 1, tq), jnp.float32),
                        pltpu.VMEM((2, A_VROWS, tq), jnp.float32)],
        compiler_params=pltpu.CompilerParams(
            dimension_semantics=("parallel", "parallel", "arbitrary"),
            vmem_limit_bytes=VMEM_LIMIT),
        name="diff_attn",
    )(proj, proj, vt, diff_lambda, subln_gain.reshape(A_VDIM, 1))


def _head_lane_masks(shape, rotary_layout=False):
    col = lax.broadcasted_iota(jnp.int32, shape, 1)
    head = _rope_layout_head(col) if rotary_layout else col // HEAD_DIM
    return [head == h for h in range(HEAD_BLOCK // HEAD_DIM)]


def _stack_heads(q, qmask, scale):
    q = (q.astype(jnp.float32) * scale).astype(q.dtype)
    zero = jnp.zeros_like(q)
    return jnp.concatenate([jnp.where(mk, q, zero) for mk in qmask], axis=0)


def _unstack_heads(x, qmask, tq):
    out = x[:tq]
    for h in range(1, len(qmask)):
        out = jnp.where(qmask[h], x[h * tq:(h + 1) * tq], out)
    return out


def _band_group(q_ref, k_ref, v_ref, y_ref, band_ref, stage_ref, tmp_ref, qd_ref, kd_ref, vd_ref,
                run_o_ref, run_l_ref, *, dilation, s_len, first, last):
    length = s_len // dilation
    tq = min(B_TQ_MAX, length)
    win = min(tq + 2 * B_REACH, length)
    unroll = min(B_UNROLL_MAX, s_len // tq)
    nblk = length // tq
    assert length % tq == 0 and (s_len // tq) % unroll == 0 and dilation in (1, 4, 16)
    assert not (last and dilation > 1)
    qmask = _head_lane_masks((tq, HEAD_BLOCK), rotary_layout=True)
    vmask = _head_lane_masks((tq, HEAD_BLOCK))

    if dilation > 1:
        for src, dst in ((q_ref, qd_ref), (k_ref, kd_ref), (v_ref, vd_ref)):
            for slab in range(HEAD_BLOCK // LANES):
                lanes = slice(slab * LANES, (slab + 1) * LANES)
                stage_ref[...] = src[0, :, lanes].astype(jnp.float32)
                for m4 in range(4):
                    quarter = stage_ref[pl.ds(m4, s_len // 4, stride=4), :]
                    if dilation == 4:
                        dst[m4 * length:(m4 + 1) * length, lanes] = quarter.astype(dst.dtype)
                    else:
                        tmp_ref[...] = quarter
                        for mm in range(4):
                            m = m4 + 4 * mm
                            dst[m * length:(m + 1) * length, lanes] = tmp_ref[
                                pl.ds(mm, length, stride=4), :].astype(dst.dtype)
        q_src, k_src, v_src = qd_ref, kd_ref, vd_ref
    else:
        q_src, k_src, v_src = q_ref.at[0], k_ref.at[0], v_ref.at[0]

    def one_block(t):
        m = t // nblk
        q0 = pl.multiple_of((t % nblk) * tq, tq)
        ks = pl.multiple_of(jnp.clip(q0 - B_REACH, 0, length - win), B_REACH)
        base = pl.multiple_of(m * length, tq)
        if dilation > 1:
            rows = pl.ds(q0 * dilation + m, tq, stride=dilation)
        else:
            rows = pl.ds(q0, tq)
        q = q_src[pl.ds(base + q0, tq), :]
        kw = k_src[pl.ds(base + ks, win), :]
        vw = v_src[pl.ds(base + ks, win), :]
        qs = _stack_heads(q, qmask, HEAD_DIM ** -0.5 * LOG2_E)
        s = lax.dot_general(qs, kw, _NT, preferred_element_type=jnp.float32)
        s = s + band_ref[(q0 - ks) // B_REACH]
        mx = jnp.max(s, axis=-1, keepdims=True)
        e = jnp.exp2(s - mx)
        den = jnp.sum(e, axis=-1, keepdims=True)
        o = jnp.dot(e.astype(jnp.bfloat16), vw, preferred_element_type=jnp.float32)
        o = o * (1.0 / den)
        lse = jnp.broadcast_to(mx + jnp.log2(den), o.shape)
        o_new = _unstack_heads(o, vmask, tq)
        l_new = _unstack_heads(lse, vmask, tq)
        for slab in range(HEAD_BLOCK // LANES):
            lanes = slice(slab * LANES, (slab + 1) * LANES)
            o_s, l_s = o_new[:, lanes], l_new[:, lanes]
            if not first:
                o_run, l_run = run_o_ref[slab, rows, :], run_l_ref[slab, rows, :]
                l_max = jnp.maximum(l_run, l_s)
                w_run, w_new = jnp.exp2(l_run - l_max), jnp.exp2(l_s - l_max)
                total = w_run + w_new
                o_s = (o_run * w_run + o_s * w_new) * (1.0 / total)
                l_s = l_max + jnp.log2(total)
            if last:
                y_ref[0, rows, lanes] = o_s.astype(y_ref.dtype)
            else:
                run_o_ref[slab, rows, :] = o_s
                run_l_ref[slab, rows, :] = l_s

    def body(i, carry):
        for u in range(unroll):
            one_block(i * unroll + u)
        return carry

    lax.fori_loop(0, dilation * nblk // unroll, body, 0)


def _band_mix_kernel(q_ref, k_ref, v_ref, y_ref, band_ref, *scratch, s_len, dilations):
    g = pl.program_id(1)

    tq = band_ref.shape[1] // B_HEADS
    row = lax.broadcasted_iota(jnp.int32, band_ref.shape[1:], 0) % tq
    col = lax.broadcasted_iota(jnp.int32, band_ref.shape[1:], 1)
    for case in range(B_CASES):
        valid = jnp.abs(col - case * B_REACH - row) <= B_REACH
        band_ref[case] = jnp.where(valid, 0.0, MASK_VALUE)

    for step, dilation in enumerate(dilations):
        @pl.when(g == step)
        def _(step=step, dilation=dilation):
            _band_group(q_ref, k_ref, v_ref, y_ref, band_ref, *scratch, dilation=dilation,
                        s_len=s_len, first=(step == 0), last=(step == len(dilations) - 1))


def _band_mixture(proj):
    bn, s_len, _ = proj.shape
    n_groups = len(B_PATTERNS)
    order = sorted(range(n_groups), key=lambda g: -B_PATTERNS[g][1])
    dilations = tuple(B_PATTERNS[g][1] for g in order)
    for window, dilation in B_PATTERNS:
        assert (window // 2) // dilation == B_REACH
    assert order == list(range(n_groups - 1, -1, -1))

    def in_spec(col0):
        blk = col0 // HEAD_BLOCK
        return pl.BlockSpec((1, s_len, HEAD_BLOCK), lambda b, g: (b, 0, blk + n_groups - 1 - g))

    slabs = HEAD_BLOCK // LANES
    regrouped = pltpu.VMEM((s_len, HEAD_BLOCK), jnp.bfloat16)
    state = pltpu.VMEM((slabs, s_len, LANES), jnp.float32)
    return pl.pallas_call(
        functools.partial(_band_mix_kernel, s_len=s_len, dilations=dilations),
        out_shape=jax.ShapeDtypeStruct((bn, s_len, B_WIDTH), jnp.bfloat16),
        grid=(bn, n_groups),
        in_specs=[in_spec(COL_BQ), in_spec(COL_BK), in_spec(COL_BV)],
        out_specs=pl.BlockSpec((1, s_len, B_WIDTH), lambda b, g: (b, 0, 0)),
        scratch_shapes=[pltpu.VMEM((B_CASES, B_HEADS * B_TQ_MAX, B_TQ_MAX + 2 * B_REACH), jnp.float32),
                        pltpu.VMEM((s_len, LANES), jnp.float32),
                        pltpu.VMEM((s_len // 4, LANES), jnp.float32),
                        regrouped, regrouped, regrouped,
                        state, state],
        compiler_params=pltpu.CompilerParams(
            dimension_semantics=("parallel", "arbitrary"),
            vmem_limit_bytes=VMEM_LIMIT),
        name="band_mix",
    )(proj, proj, proj)


def _na_table_kernel(rpb_ref, tab_ref, *, rows):
    h = pl.program_id(0)
    n_dr = 2 * NA_KH - 1
    n_dc = 2 * NA_KW - 1
    shape = (GRID_W, LANES)
    qc = lax.broadcasted_iota(jnp.int32, shape, 0)
    lane = lax.broadcasted_iota(jnp.int32, shape, 1)
    kc = lane % GRID_W
    dc_i = jnp.clip(kc - qc + (NA_KW - 1), 0, n_dc - 1)
    cs = jnp.clip(qc - NA_KW // 2, 0, GRID_W - NA_KW)
    col_valid = (kc >= cs) & (kc < cs + NA_KW)
    masked = jnp.full(shape, MASK_VALUE, jnp.float32)

    row_bias = []
    for dr in range(n_dr):
        base = (h * n_dr + dr) * n_dc
        t = masked
        for dc in range(n_dc):
            t = jnp.where(dc_i == dc, rpb_ref[base + dc], t)
        row_bias.append(jnp.where(col_valid, t, MASK_VALUE))

    pair_rep = (0, 1, 2, rows // 2 - 2, rows // 2 - 1)
    for case in range(C_CASES):
        ip = pair_rep[case]
        ws = min(max(2 * ip - NA_KH // 2, 0), rows - C_KROWS)
        for qr_l in range(2):
            r = 2 * ip + qr_l
            rs = min(max(r - NA_KH // 2, 0), rows - NA_KH)
            for tile in range(C_WIN // LANES):
                halves = []
                for kr in (ws + 2 * tile, ws + 2 * tile + 1):
                    halves.append(row_bias[kr - r + NA_KH - 1] if rs <= kr < rs + NA_KH else masked)
                tab_ref[case, 0, qr_l * GRID_W:(qr_l + 1) * GRID_W, tile * LANES:(tile + 1) * LANES] = (
                    jnp.where(lane < GRID_W, halves[0], halves[1]))


def _na_tables(rpb, rows):
    flat = rpb.reshape(-1)
    return pl.pallas_call(
        functools.partial(_na_table_kernel, rows=rows),
        out_shape=jax.ShapeDtypeStruct((C_CASES, C_HEADS, C_PAIR, C_WIN), jnp.float32),
        grid=(C_HEADS,),
        in_specs=[pl.BlockSpec(memory_space=pltpu.SMEM)],
        out_specs=pl.BlockSpec((C_CASES, 1, C_PAIR, C_WIN), lambda h: (0, h, 0, 0)),
        name="na_tables",
    )(flat)


def _na_kernel(q_ref, k_ref, v_ref, tab_ref, o_ref, *, rows):
    rb = pl.program_id(1)
    qmask = _head_lane_masks((C_PAIR, HEAD_BLOCK))
    pairs_per_step = C_RB // 2
    for t in range(pairs_per_step):
        ip = rb * pairs_per_step + t
        ws = jnp.clip(2 * ip - NA_KH // 2, 0, rows - C_KROWS)
        case = ip - ws // 2
        k0 = pl.multiple_of(ws * GRID_W, C_PAIR)
        q = q_ref[0, t * C_PAIR:(t + 1) * C_PAIR, :]
        kw = k_ref[0, pl.ds(k0, C_WIN), :]
        vw = v_ref[0, pl.ds(k0, C_WIN), :]
        qs = _stack_heads(q, qmask, HEAD_DIM ** -0.5)
        s = lax.dot_general(qs, kw, _NT, preferred_element_type=jnp.float32)
        s = s + tab_ref[case]
        mx = jnp.max(s, axis=-1, keepdims=True)
        e = jnp.exp(s - mx)
        den = jnp.sum(e, axis=-1, keepdims=True)
        o = jnp.dot(e.astype(jnp.bfloat16), vw, preferred_element_type=jnp.float32)
        o_acc = _unstack_heads(o * (1.0 / den), qmask, C_PAIR)
        o_ref[0, t * C_PAIR:(t + 1) * C_PAIR, :] = o_acc.astype(o_ref.dtype)


def _neighborhood_attention(proj, tables):
    bn, s_len, _ = proj.shape
    rows = s_len // GRID_W
    tq = C_RB * GRID_W
    qblk = COL_CQ // HEAD_BLOCK
    kblk = COL_CK // HEAD_BLOCK
    vblk = COL_CV // HEAD_BLOCK
    return pl.pallas_call(
        functools.partial(_na_kernel, rows=rows),
        out_shape=jax.ShapeDtypeStruct((bn, s_len, C_WIDTH), jnp.bfloat16),
        grid=(bn, rows // C_RB),
        in_specs=[pl.BlockSpec((1, tq, HEAD_BLOCK), lambda b, i: (b, i, qblk)),
                  pl.BlockSpec((1, s_len, HEAD_BLOCK), lambda b, i: (b, 0, kblk)),
                  pl.BlockSpec((1, s_len, HEAD_BLOCK), lambda b, i: (b, 0, vblk)),
                  pl.BlockSpec((C_CASES, C_HEADS * C_PAIR, C_WIN), lambda b, i: (0, 0, 0))],
        out_specs=pl.BlockSpec((1, tq, C_WIDTH), lambda b, i: (b, i, 0)),
        compiler_params=pltpu.CompilerParams(
            dimension_semantics=("parallel", "arbitrary"),
            vmem_limit_bytes=VMEM_LIMIT),
        name="na_attn",
    )(proj, proj, proj, tables.reshape(C_CASES, C_HEADS * C_PAIR, C_WIN))


def _merge_kernel(x_ref, gate_ref, h_ref, ya_ref, yb_ref, yc_ref, wg_ref, wb_ref, wo_ref, fg_ref,
                  o_ref, *, final_norm):
    h = h_ref[0]

    def gate_cols(k):
        return jnp.dot(h, wg_ref[:, k * D_MODEL:(k + 1) * D_MODEL], preferred_element_type=jnp.float32)

    z = gate_cols(0)
    sz = z * _sigmoid(z)
    ya = (ya_ref[0].astype(jnp.float32) * sz[:, :A_WIDTH]).astype(jnp.bfloat16)
    ybz = (yb_ref[0].astype(jnp.float32) * sz[:, A_WIDTH:A_WIDTH + B_WIDTH]).astype(jnp.bfloat16)
    ycz = (yc_ref[0].astype(jnp.float32) * sz[:, A_WIDTH + B_WIDTH:]).astype(jnp.bfloat16)

    pa = jnp.dot(ya, wb_ref[:A_WIDTH, :], preferred_element_type=jnp.float32)
    pb = jnp.dot(ybz, wb_ref[A_WIDTH:A_WIDTH + B_WIDTH, :], preferred_element_type=jnp.float32)
    pc = jnp.dot(ycz, wb_ref[A_WIDTH + B_WIDTH:, :], preferred_element_type=jnp.float32)
    merged = _sigmoid(gate_cols(1)) * pa
    merged = merged + _sigmoid(gate_cols(2)) * pb
    merged = merged + _sigmoid(gate_cols(3)) * pc
    out = jnp.dot(merged.astype(jnp.bfloat16), wo_ref[...], preferred_element_type=jnp.float32)
    xn = x_ref[0] + gate_ref[0] * out
    if final_norm:
        ms = jnp.mean(xn * xn, axis=-1, keepdims=True)
        xn = xn * lax.rsqrt(ms + RMS_EPS) * fg_ref[...]
    o_ref[0] = xn


def _merge(x, gate, h, ya, yb, yc, w_main, w_branch, w_out, final_gain, layer, final_norm):
    bn, s_len, _ = x.shape
    tm = MERGE_TM
    assert ATTN_COLS % GATE_COLS == 0 and GATE_COLS == (1 + N_BRANCHES) * D_MODEL
    gate_blk = ATTN_COLS // GATE_COLS

    def tok(width):
        return pl.BlockSpec((1, tm, width), lambda b, i: (b, i, 0))

    return pl.pallas_call(
        functools.partial(_merge_kernel, final_norm=final_norm),
        out_shape=jax.ShapeDtypeStruct((bn, s_len, D_MODEL), jnp.float32),
        grid=(bn, s_len // tm),
        in_specs=[tok(D_MODEL),
                  pl.BlockSpec((1, 1, D_MODEL), lambda b, i: (b, 0, 0)),
                  tok(D_MODEL), tok(A_WIDTH), tok(B_WIDTH), tok(C_WIDTH),
                  pl.BlockSpec((None, D_MODEL, GATE_COLS), lambda b, i: (layer, 0, gate_blk)),
                  pl.BlockSpec((None, BR_WIDTH, D_MODEL), lambda b, i: (layer, 0, 0)),
                  pl.BlockSpec((None, D_MODEL, D_MODEL), lambda b, i: (layer, 0, 0)),
                  pl.BlockSpec((1, D_MODEL), lambda b, i: (0, 0))],
        out_specs=tok(D_MODEL),
        compiler_params=pltpu.CompilerParams(
            dimension_semantics=("parallel", "parallel"),
            vmem_limit_bytes=VMEM_LIMIT),
        name="merge",
    )(x, gate, h, ya, yb, yc, w_main, w_branch, w_out, final_gain)


def _w_main_kernel(w_ref, o_ref):
    d = pl.program_id(1)
    w = w_ref[...].astype(jnp.bfloat16)

    @pl.when(d < ROPE_COLS // W_PREP_COLS)
    def _():
        src = lax.broadcasted_iota(jnp.int32, (HEAD_BLOCK, HEAD_BLOCK), 0)
        dst = lax.broadcasted_iota(jnp.int32, (HEAD_BLOCK, HEAD_BLOCK), 1)
        onehot = jnp.where(src == _rope_layout_source(dst), 1.0, 0.0).astype(jnp.bfloat16)
        for blk in range(W_PREP_COLS // HEAD_BLOCK):
            cols = slice(blk * HEAD_BLOCK, (blk + 1) * HEAD_BLOCK)
            o_ref[:, cols] = jnp.dot(w[:, cols], onehot,
                                     preferred_element_type=jnp.float32).astype(o_ref.dtype)

    @pl.when(d >= ROPE_COLS // W_PREP_COLS)
    def _():
        o_ref[...] = w


def _w_vt_kernel(w_ref, o_ref):
    w = w_ref[...].astype(jnp.bfloat16)
    r = lax.broadcasted_iota(jnp.int32, (A_WIDTH, A_WIDTH), 0)
    c = lax.broadcasted_iota(jnp.int32, (A_WIDTH, A_WIDTH), 1)
    eye = jnp.where(r == c, 1.0, 0.0).astype(jnp.bfloat16)
    o_ref[...] = lax.dot_general(eye, w, _NT, preferred_element_type=jnp.float32).astype(o_ref.dtype)


def _split_w_in(w_in):
    depth = w_in.shape[0]
    av_blk = 2 * A_QK_COLS // W_PREP_COLS
    assert 2 * A_QK_COLS % W_PREP_COLS == 0 and A_WIDTH == W_PREP_COLS and ROPE_COLS % W_PREP_COLS == 0
    w_main = pl.pallas_call(
        _w_main_kernel,
        out_shape=jax.ShapeDtypeStruct((depth, D_MODEL, MAIN_COLS), jnp.bfloat16),
        grid=(depth, MAIN_COLS // W_PREP_COLS),
        in_specs=[pl.BlockSpec((None, D_MODEL, W_PREP_COLS),
                               lambda l, d: (l, 0, jnp.where(d < av_blk, d, d + 1)))],
        out_specs=pl.BlockSpec((None, D_MODEL, W_PREP_COLS), lambda l, d: (l, 0, d)),
        name="w_main_prep",
    )(w_in)
    w_vt = pl.pallas_call(
        _w_vt_kernel,
        out_shape=jax.ShapeDtypeStruct((depth, A_WIDTH, D_MODEL), jnp.bfloat16),
        grid=(depth,),
        in_specs=[pl.BlockSpec((None, D_MODEL, A_WIDTH), lambda l: (l, 0, av_blk))],
        out_specs=pl.BlockSpec((None, A_WIDTH, D_MODEL), lambda l: (l, 0, 0)),
        name="w_vt_prep",
    )(w_in)
    return w_main, w_vt


def kernel(x, c, positions, norm_gain, w_ada, b_ada, w_in, diff_lambda, diff_subln_gain, na_rpb,
           w_branch, w_out, final_gain):
    depth = w_in.shape[0]
    bn, s_len, _ = x.shape
    rows = s_len // GRID_W
    tabs = _rope_tables(positions)
    ada = _ada_all_layers(c, w_ada, b_ada)
    fg = final_gain.reshape(1, D_MODEL)
    w_main, w_vt = _split_w_in(w_in)
    w_branch = w_branch.astype(jnp.bfloat16)
    w_out = w_out.astype(jnp.bfloat16)
    for layer in range(depth):
        shift = ada[layer, :, None, :D_MODEL]
        scale = ada[layer, :, None, D_MODEL:2 * D_MODEL]
        gate = ada[layer, :, None, 2 * D_MODEL:]
        proj, vt, h = _in_proj(x, shift, scale, norm_gain[layer].reshape(1, D_MODEL), tabs, w_main,
                               w_vt, layer)
        lambda_init = 0.8 - 0.6 * math.exp(-0.3 * layer)
        ya = _diff_attention(proj, vt, diff_lambda[layer], diff_subln_gain[layer], lambda_init)
        yb = _band_mixture(proj)
        yc = _neighborhood_attention(proj, _na_tables(na_rpb[layer], rows))
        x = _merge(x, gate, h, ya, yb, yc, w_main, w_branch, w_out, fg, layer,
                   final_norm=(layer == depth - 1))
    return x
```

```python
import functools
import math

import numpy as np
import jax
import jax.numpy as jnp
from jax import lax
from jax.experimental import pallas as pl
from jax.experimental.pallas import tpu as pltpu

D_MODEL = 1024
HEAD_DIM = 64
ROT_DIM = HEAD_DIM // 4
ROT_HALF = ROT_DIM // 2
ROPE_THETA = 500000.0
RMS_EPS = 1e-6
MASK_VALUE = -1e30
LOG2_E = math.log2(math.e)

A_HEADS = 4
A_VDIM = 2 * HEAD_DIM
A_WIDTH = A_HEADS * A_VDIM
A_QK_COLS = A_HEADS * 2 * HEAD_DIM

B_PATTERNS = ((128, 1), (512, 4), (2048, 16))
B_GROUPS = 3
B_HEADS = 4
B_WIDTH = B_HEADS * HEAD_DIM
B_COLS = B_GROUPS * B_WIDTH

GRID_W = 64
C_HEADS = 4
NA_KH = 8
NA_KW = 16
C_WIDTH = C_HEADS * HEAD_DIM

N_BRANCHES = 3
BR_WIDTH = A_WIDTH + B_WIDTH + C_WIDTH

LANES = 128
SUBLANES_F32 = 8
SUBLANES_BF16 = 16
HEAD_BLOCK = 4 * HEAD_DIM
ROPE_LANES = (HEAD_BLOCK // HEAD_DIM) * ROT_HALF
VMEM_LIMIT = 56 * 1024 * 1024

COL_AQ = 0
COL_AK = COL_AQ + A_QK_COLS
COL_BQ = COL_AK + A_QK_COLS
COL_BK = COL_BQ + B_COLS
COL_BV = COL_BK + B_COLS
COL_CQ = COL_BV + B_COLS
COL_CK = COL_CQ + C_WIDTH
COL_CV = COL_CK + C_WIDTH
COL_Z = COL_CV + C_WIDTH
COL_G = COL_Z + BR_WIDTH
MAIN_COLS = COL_G + N_BRANCHES * D_MODEL
ATTN_COLS = COL_Z
GATE_COLS = MAIN_COLS - ATTN_COLS
ROPE_COLS = COL_BV

W_PREP_COLS = 512
PROJ_TM = 2048
PROJ_TN = 1024
A_TQ = 2048
A_KC = 512
A_VROWS = A_VDIM + SUBLANES_BF16
TABLE_TS = 1024
TABLE_PACK = LANES // ROPE_LANES
ADA_TN = 1024
B_TQ_MAX = 128
B_UNROLL_MAX = 8
B_REACH = 64
B_CASES = 3
C_PAIR = 2 * GRID_W
C_KROWS = 10
C_WIN = C_KROWS * GRID_W
C_CASES = 5
C_RB = 32
MERGE_TM = 1024

_NT = (((1,), (1,)), ((), ()))


def _sigmoid(x):
    return 1.0 / (1.0 + jnp.exp2(x * (-LOG2_E)))


def _rope_layout_head(col):
    l = col % LANES
    c = col // LANES
    rest = HEAD_DIM - ROT_DIM
    return jnp.where(l < ROPE_LANES, l // ROT_HALF, 2 * c + jnp.where(l >= ROPE_LANES + rest, 1, 0))


def _rope_layout_source(col):
    l = col % LANES
    c = col // LANES
    rest = HEAD_DIM - ROT_DIM
    r = l - ROPE_LANES
    upper = jnp.where(r >= rest, 1, 0)
    rotary_src = HEAD_DIM * (l // ROT_HALF) + ROT_HALF * c + l % ROT_HALF
    other_src = HEAD_DIM * (2 * c + upper) + ROT_DIM + r - rest * upper
    return jnp.where(l < ROPE_LANES, rotary_src, other_src)


def _rope_table_kernel(pos_ref, inv_ref, c_ref, s_ref):
    ang = pos_ref[0] * inv_ref[...]
    c_ref[0] = jnp.cos(ang)
    s_ref[0] = jnp.sin(ang)


def _rope_tables(positions):
    bn, s_len = positions.shape
    inv = np.float32(ROPE_THETA) ** (-np.arange(0, ROT_DIM, 2, dtype=np.float32) / np.float32(ROT_DIM))
    inv_lane = np.tile(inv, LANES // ROT_HALF).reshape(1, LANES)
    pos = jnp.repeat(positions.astype(jnp.float32).reshape(bn, s_len // TABLE_PACK, TABLE_PACK),
                     ROPE_LANES, axis=-1)
    ts = TABLE_TS // TABLE_PACK
    tab = jax.ShapeDtypeStruct((bn, s_len // TABLE_PACK, LANES), jnp.float32)
    spec = pl.BlockSpec((1, ts, LANES), lambda b, i: (b, i, 0))
    return pl.pallas_call(
        _rope_table_kernel,
        out_shape=(tab, tab),
        grid=(bn, s_len // TABLE_TS),
        in_specs=[spec, pl.BlockSpec((1, LANES), lambda b, i: (0, 0))],
        out_specs=(spec, spec),
        name="rope_tables",
    )(pos, jnp.asarray(inv_lane))


def _ada_kernel(c_ref, w_ref, b_ref, o_ref):
    c = c_ref[...]
    c_act = c * _sigmoid(c)
    o_ref[0] = jnp.dot(c_act, w_ref[0], precision=lax.Precision.HIGHEST,
                       preferred_element_type=jnp.float32) + b_ref[0]


def _ada_all_layers(c, w_ada, b_ada):
    depth = w_ada.shape[0]
    bn = c.shape[0]
    rows = SUBLANES_F32 * pl.cdiv(bn, SUBLANES_F32)
    c_pad = jnp.pad(c, ((0, rows - bn), (0, 0)))
    tn = ADA_TN
    out = pl.pallas_call(
        _ada_kernel,
        out_shape=jax.ShapeDtypeStruct((depth, rows, 3 * D_MODEL), jnp.float32),
        grid=(depth, 3 * D_MODEL // tn),
        in_specs=[pl.BlockSpec((rows, D_MODEL), lambda l, j: (0, 0)),
                  pl.BlockSpec((1, D_MODEL, tn), lambda l, j: (l, 0, j)),
                  pl.BlockSpec((1, 1, tn), lambda l, j: (l, 0, j))],
        out_specs=pl.BlockSpec((1, rows, tn), lambda l, j: (l, 0, j)),
        name="adaln",
    )(c_pad, w_ada, b_ada.reshape(depth, 1, 3 * D_MODEL))
    return out[:, :bn]


def _in_proj_kernel(x_ref, shift_ref, scale_ref, gain_ref, cp_ref, sp_ref, w_ref, wvt_ref,
                    proj_ref, vt_ref, h_ref, c_ref, s_ref):
    j = pl.program_id(2)

    @pl.when(j == 0)
    def _():
        lane = lax.broadcasted_iota(jnp.int32, cp_ref.shape[1:], 1)
        rows = cp_ref.shape[1]
        for t in range(TABLE_PACK):
            shift = (LANES - ROPE_LANES * t) % LANES
            ct = cp_ref[0] if shift == 0 else pltpu.roll(cp_ref[0], shift, 1)
            st = sp_ref[0] if shift == 0 else pltpu.roll(sp_ref[0], shift, 1)
            c_ref[pl.ds(t, rows, stride=TABLE_PACK), :] = jnp.where(lane < ROPE_LANES, ct, 1.0)
            s_ref[pl.ds(t, rows, stride=TABLE_PACK), :] = jnp.where(lane < ROPE_LANES, st, 0.0)

        xf = x_ref[0]
        ms = jnp.mean(xf * xf, axis=-1, keepdims=True)
        y = xf * lax.rsqrt(ms + RMS_EPS) * gain_ref[...]
        h = (y * (1.0 + scale_ref[0]) + shift_ref[0]).astype(jnp.bfloat16)
        h_ref[0] = h
        vt = lax.dot_general(wvt_ref[...], h, _NT, preferred_element_type=jnp.float32)
        vt_ref[0] = vt.astype(vt_ref.dtype)

    acc = jnp.dot(h_ref[0], w_ref[...], preferred_element_type=jnp.float32)
    blocks_per_tile = PROJ_TN // HEAD_BLOCK
    cc = c_ref[...]
    ss = s_ref[...]
    for blk in range(blocks_per_tile):
        rotary = (j * blocks_per_tile + blk) < (ROPE_COLS // HEAD_BLOCK)
        cb = jnp.where(rotary, cc, 1.0)
        sb = jnp.where(rotary, ss, 0.0)
        lo0 = blk * HEAD_BLOCK
        hi0 = lo0 + LANES
        lo = acc[:, lo0:hi0]
        hi = acc[:, hi0:hi0 + LANES]
        proj_ref[0, :, lo0:hi0] = (lo * cb - hi * sb).astype(proj_ref.dtype)
        proj_ref[0, :, hi0:hi0 + LANES] = (hi * cb + lo * sb).astype(proj_ref.dtype)


def _in_proj(x, shift, scale, gain, tabs, w_main, w_vt, layer):
    bn, s_len, _ = x.shape
    tm, tn = PROJ_TM, PROJ_TN
    tab_spec = pl.BlockSpec((1, tm // TABLE_PACK, LANES), lambda b, i, j: (b, i, 0))
    mod_spec = pl.BlockSpec((1, 1, D_MODEL), lambda b, i, j: (b, 0, 0))
    return pl.pallas_call(
        _in_proj_kernel,
        out_shape=(jax.ShapeDtypeStruct((bn, s_len, ATTN_COLS), jnp.bfloat16),
                   jax.ShapeDtypeStruct((bn, A_WIDTH, s_len), jnp.bfloat16),
                   jax.ShapeDtypeStruct((bn, s_len, D_MODEL), jnp.bfloat16)),
        grid=(bn, s_len // tm, ATTN_COLS // tn),
        in_specs=[pl.BlockSpec((1, tm, D_MODEL), lambda b, i, j: (b, i, 0)),
                  mod_spec, mod_spec,
                  pl.BlockSpec((1, D_MODEL), lambda b, i, j: (0, 0)),
                  tab_spec, tab_spec,
                  pl.BlockSpec((None, D_MODEL, tn), lambda b, i, j: (layer, 0, j)),
                  pl.BlockSpec((None, A_WIDTH, D_MODEL), lambda b, i, j: (layer, 0, 0))],
        out_specs=(pl.BlockSpec((1, tm, tn), lambda b, i, j: (b, i, j)),
                   pl.BlockSpec((1, A_WIDTH, tm), lambda b, i, j: (b, 0, i)),
                   pl.BlockSpec((1, tm, D_MODEL), lambda b, i, j: (b, i, 0))),
        scratch_shapes=[pltpu.VMEM((tm, LANES), jnp.float32),
                        pltpu.VMEM((tm, LANES), jnp.float32)],
        compiler_params=pltpu.CompilerParams(
            dimension_semantics=("parallel", "parallel", "arbitrary"),
            vmem_limit_bytes=VMEM_LIMIT),
        name="in_proj",
    )(x, shift, scale, gain, *tabs, w_main, w_vt)


def _diff_attn_kernel(q_ref, k_ref, vt_ref, dl_ref, gain_ref, o_ref,
                      qm_ref, vta_ref, sa_ref, sb_ref, m_ref, acc_ref, *, lambda_init, s_len):
    h = pl.program_id(1)
    tq = q_ref.shape[1]
    n_chunks = s_len // A_KC

    @pl.when(pl.program_id(2) == 0)
    def _():
        vta_ref[:A_VDIM, :] = vt_ref[0]
        row = lax.broadcasted_iota(jnp.int32, (A_VROWS - A_VDIM, s_len), 0)
        vta_ref[A_VDIM:, :] = jnp.where(row == 0, 1.0, 0.0).astype(vta_ref.dtype)

    q = q_ref[0]
    pair = _rope_layout_head(lax.broadcasted_iota(jnp.int32, q.shape, 1))
    base = 2 * (h % 2)
    for c in range(2):
        qsel = jnp.where(pair == base + c, q, jnp.zeros_like(q)).astype(jnp.float32)
        qm_ref[c] = (qsel * (HEAD_DIM ** -0.5 * LOG2_E)).astype(qm_ref.dtype)
        m_ref[c] = jnp.full((1, tq), MASK_VALUE, jnp.float32)
        acc_ref[c] = jnp.zeros((A_VROWS, tq), jnp.float32)

    def scores(i, s_ref):
        kc = k_ref[0, pl.ds(pl.multiple_of(i * A_KC, A_KC), A_KC), :]
        for c in range(2):
            s_ref[c] = lax.dot_general(kc, qm_ref[c], _NT,
                                       preferred_element_type=jnp.float32)

    def update(i, s_ref):
        vt_c = vta_ref[:, pl.ds(pl.multiple_of(i * A_KC, A_KC), A_KC)]
        for c in range(2):
            st = s_ref[c]
            m_old = m_ref[c]
            m_new = jnp.maximum(m_old, jnp.max(st, axis=0, keepdims=True))
            alpha = jnp.exp2(m_old - m_new)
            e = jnp.exp2(st - m_new).astype(jnp.bfloat16)
            pv = jnp.dot(vt_c, e, preferred_element_type=jnp.float32)
            acc_ref[c] = alpha * acc_ref[c] + pv
            m_ref[c] = m_new

    scores(0, sa_ref)

    def body(j, carry):
        scores(2 * j + 1, sb_ref)
        update(2 * j, sa_ref)
        scores(2 * j + 2, sa_ref)
        update(2 * j + 1, sb_ref)
        return carry

    lax.fori_loop(0, n_chunks // 2 - 1, body, 0)
    scores(n_chunks - 1, sb_ref)
    update(n_chunks - 2, sa_ref)
    update(n_chunks - 1, sb_ref)

    dl = dl_ref[...]
    lam = (jnp.exp(jnp.sum(dl[0:1] * dl[1:2], axis=1, keepdims=True))
           - jnp.exp(jnp.sum(dl[2:3] * dl[3:4], axis=1, keepdims=True)) + lambda_init)
    maps = [acc_ref[c, :A_VDIM, :] * (1.0 / acc_ref[c, A_VDIM:A_VDIM + 1, :]) for c in range(2)]
    ot = maps[0] - lam * maps[1]
    ms = jnp.mean(ot * ot, axis=0, keepdims=True)
    y = ot * lax.rsqrt(ms + RMS_EPS) * gain_ref[...] * (1.0 - lambda_init)
    o_ref[0] = y.T.astype(o_ref.dtype)


def _diff_attention(proj, vt, diff_lambda, subln_gain, lambda_init):
    bn, s_len, _ = proj.shape
    tq = A_TQ
    qblk = COL_AQ // HEAD_BLOCK
    kblk = COL_AK // HEAD_BLOCK
    assert s_len % (2 * A_KC) == 0
    return pl.pallas_call(
        functools.partial(_diff_attn_kernel, lambda_init=lambda_init, s_len=s_len),
        out_shape=jax.ShapeDtypeStruct((bn, s_len, A_WIDTH), jnp.bfloat16),
        grid=(bn, A_HEADS, s_len // tq),
        in_specs=[pl.BlockSpec((1, tq, HEAD_BLOCK), lambda b, h, i: (b, i, qblk + h // 2)),
                  pl.BlockSpec((1, s_len, HEAD_BLOCK), lambda b, h, i: (b, 0, kblk + h // 2)),
                  pl.BlockSpec((1, A_VDIM, s_len), lambda b, h, i: (b, h, 0)),
                  pl.BlockSpec((4, HEAD_DIM), lambda b, h, i: (0, 0)),
                  pl.BlockSpec((A_VDIM, 1), lambda b, h, i: (0, 0))],
        out_specs=pl.BlockSpec((1, tq, A_VDIM), lambda b, h, i: (b, i, h)),
        scratch_shapes=[pltpu.VMEM((2, tq, HEAD_BLOCK), jnp.bfloat16),
                        pltpu.VMEM((A_VROWS, s_len), jnp.bfloat16),
                        pltpu.VMEM((2, A_KC, tq), jnp.float32),
                        pltpu.VMEM((2, A_KC, tq), jnp.float32),
                        pltpu.VMEM((2, 1, tq), jnp.float32),
                        pltpu.VMEM((2, A_VROWS, tq), jnp.float32)],
        compiler_params=pltpu.CompilerParams(
            dimension_semantics=("parallel", "parallel", "arbitrary"),
            vmem_limit_bytes=VMEM_LIMIT),
        name="diff_attn",
    )(proj, proj, vt, diff_lambda, subln_gain.reshape(A_VDIM, 1))


def _head_lane_masks(shape, rotary_layout=False):
    col = lax.broadcasted_iota(jnp.int32, shape, 1)
    head = _rope_layout_head(col) if rotary_layout else col // HEAD_DIM
    return [head == h for h in range(HEAD_BLOCK // HEAD_DIM)]


def _stack_heads(q, qmask, scale):
    q = (q.astype(jnp.float32) * scale).astype(q.dtype)
    zero = jnp.zeros_like(q)
    return jnp.concatenate([jnp.where(mk, q, zero) for mk in qmask], axis=0)


def _unstack_heads(x, qmask, tq):
    out = x[:tq]
    for h in range(1, len(qmask)):
        out = jnp.where(qmask[h], x[h * tq:(h + 1) * tq], out)
    return out


def _band_group(q_ref, k_ref, v_ref, y_ref, band_ref, stage_ref, tmp_ref, qd_ref, kd_ref, vd_ref,
                run_o_ref, run_l_ref, *, dilation, s_len, first, last):
    length = s_len // dilation
    tq = min(B_TQ_MAX, length)
    win = min(tq + 2 * B_REACH, length)
    unroll = min(B_UNROLL_MAX, s_len // tq)
    nblk = length // tq
    assert length % tq == 0 and (s_len // tq) % unroll == 0 and dilation in (1, 4, 16)
    assert not (last and dilation > 1)
    qmask = _head_lane_masks((tq, HEAD_BLOCK), rotary_layout=True)
    vmask = _head_lane_masks((tq, HEAD_BLOCK))

    if dilation > 1:
        for src, dst in ((q_ref, qd_ref), (k_ref, kd_ref), (v_ref, vd_ref)):
            for slab in range(HEAD_BLOCK // LANES):
                lanes = slice(slab * LANES, (slab + 1) * LANES)
                stage_ref[...] = src[0, :, lanes].astype(jnp.float32)
                for m4 in range(4):
                    quarter = stage_ref[pl.ds(m4, s_len // 4, stride=4), :]
                    if dilation == 4:
                        dst[m4 * length:(m4 + 1) * length, lanes] = quarter.astype(dst.dtype)
                    else:
                        tmp_ref[...] = quarter
                        for mm in range(4):
                            m = m4 + 4 * mm
                            dst[m * length:(m + 1) * length, lanes] = tmp_ref[
                                pl.ds(mm, length, stride=4), :].astype(dst.dtype)
        q_src, k_src, v_src = qd_ref, kd_ref, vd_ref
    else:
        q_src, k_src, v_src = q_ref.at[0], k_ref.at[0], v_ref.at[0]

    def one_block(t):
        m = t // nblk
        q0 = pl.multiple_of((t % nblk) * tq, tq)
        ks = pl.multiple_of(jnp.clip(q0 - B_REACH, 0, length - win), B_REACH)
        base = pl.multiple_of(m * length, tq)
        if dilation > 1:
            rows = pl.ds(q0 * dilation + m, tq, stride=dilation)
        else:
            rows = pl.ds(q0, tq)
        q = q_src[pl.ds(base + q0, tq), :]
        kw = k_src[pl.ds(base + ks, win), :]
        vw = v_src[pl.ds(base + ks, win), :]
        qs = _stack_heads(q, qmask, HEAD_DIM ** -0.5 * LOG2_E)
        s = lax.dot_general(qs, kw, _NT, preferred_element_type=jnp.float32)
        s = s + band_ref[(q0 - ks) // B_REACH]
        mx = jnp.max(s, axis=-1, keepdims=True)
        e = jnp.exp2(s - mx)
        den = jnp.sum(e, axis=-1, keepdims=True)
        o = jnp.dot(e.astype(jnp.bfloat16), vw, preferred_element_type=jnp.float32)
        o = o * (1.0 / den)
        lse = jnp.broadcast_to(mx + jnp.log2(den), o.shape)
        o_new = _unstack_heads(o, vmask, tq)
        l_new = _unstack_heads(lse, vmask, tq)
        for slab in range(HEAD_BLOCK // LANES):
            lanes = slice(slab * LANES, (slab + 1) * LANES)
            o_s, l_s = o_new[:, lanes], l_new[:, lanes]
            if not first:
                o_run, l_run = run_o_ref[slab, rows, :], run_l_ref[slab, rows, :]
                l_max = jnp.maximum(l_run, l_s)
                w_run, w_new = jnp.exp2(l_run - l_max), jnp.exp2(l_s - l_max)
                total = w_run + w_new
                o_s = (o_run * w_run + o_s * w_new) * (1.0 / total)
                l_s = l_max + jnp.log2(total)
            if last:
                y_ref[0, rows, lanes] = o_s.astype(y_ref.dtype)
            else:
                run_o_ref[slab, rows, :] = o_s
                run_l_ref[slab, rows, :] = l_s

    def body(i, carry):
        for u in range(unroll):
            one_block(i * unroll + u)
        return carry

    lax.fori_loop(0, dilation * nblk // unroll, body, 0)


def _band_mix_kernel(q_ref, k_ref, v_ref, y_ref, band_ref, *scratch, s_len, dilations):
    g = pl.program_id(1)

    tq = band_ref.shape[1] // B_HEADS
    row = lax.broadcasted_iota(jnp.int32, band_ref.shape[1:], 0) % tq
    col = lax.broadcasted_iota(jnp.int32, band_ref.shape[1:], 1)
    for case in range(B_CASES):
        valid = jnp.abs(col - case * B_REACH - row) <= B_REACH
        band_ref[case] = jnp.where(valid, 0.0, MASK_VALUE)

    for step, dilation in enumerate(dilations):
        @pl.when(g == step)
        def _(step=step, dilation=dilation):
            _band_group(q_ref, k_ref, v_ref, y_ref, band_ref, *scratch, dilation=dilation,
                        s_len=s_len, first=(step == 0), last=(step == len(dilations) - 1))


def _band_mixture(proj):
    bn, s_len, _ = proj.shape
    n_groups = len(B_PATTERNS)
    order = sorted(range(n_groups), key=lambda g: -B_PATTERNS[g][1])
    dilations = tuple(B_PATTERNS[g][1] for g in order)
    for window, dilation in B_PATTERNS:
        assert (window // 2) // dilation == B_REACH
    assert order == list(range(n_groups - 1, -1, -1))

    def in_spec(col0):
        blk = col0 // HEAD_BLOCK
        return pl.BlockSpec((1, s_len, HEAD_BLOCK), lambda b, g: (b, 0, blk + n_groups - 1 - g))

    slabs = HEAD_BLOCK // LANES
    regrouped = pltpu.VMEM((s_len, HEAD_BLOCK), jnp.bfloat16)
    state = pltpu.VMEM((slabs, s_len, LANES), jnp.float32)
    return pl.pallas_call(
        functools.partial(_band_mix_kernel, s_len=s_len, dilations=dilations),
        out_shape=jax.ShapeDtypeStruct((bn, s_len, B_WIDTH), jnp.bfloat16),
        grid=(bn, n_groups),
        in_specs=[in_spec(COL_BQ), in_spec(COL_BK), in_spec(COL_BV)],
        out_specs=pl.BlockSpec((1, s_len, B_WIDTH), lambda b, g: (b, 0, 0)),
        scratch_shapes=[pltpu.VMEM((B_CASES, B_HEADS * B_TQ_MAX, B_TQ_MAX + 2 * B_REACH), jnp.float32),
                        pltpu.VMEM((s_len, LANES), jnp.float32),
                        pltpu.VMEM((s_len // 4, LANES), jnp.float32),
                        regrouped, regrouped, regrouped,
                        state, state],
        compiler_params=pltpu.CompilerParams(
            dimension_semantics=("parallel", "arbitrary"),
            vmem_limit_bytes=VMEM_LIMIT),
        name="band_mix",
    )(proj, proj, proj)


def _na_table_kernel(rpb_ref, tab_ref, *, rows):
    h = pl.program_id(0)
    n_dr = 2 * NA_KH - 1
    n_dc = 2 * NA_KW - 1
    shape = (GRID_W, LANES)
    qc = lax.broadcasted_iota(jnp.int32, shape, 0)
    lane = lax.broadcasted_iota(jnp.int32, shape, 1)
    kc = lane % GRID_W
    dc_i = jnp.clip(kc - qc + (NA_KW - 1), 0, n_dc - 1)
    cs = jnp.clip(qc - NA_KW // 2, 0, GRID_W - NA_KW)
    col_valid = (kc >= cs) & (kc < cs + NA_KW)
    masked = jnp.full(shape, MASK_VALUE, jnp.float32)

    row_bias = []
    for dr in range(n_dr):
        base = (h * n_dr + dr) * n_dc
        t = masked
        for dc in range(n_dc):
            t = jnp.where(dc_i == dc, rpb_ref[base + dc], t)
        row_bias.append(jnp.where(col_valid, t, MASK_VALUE))

    pair_rep = (0, 1, 2, rows // 2 - 2, rows // 2 - 1)
    for case in range(C_CASES):
        ip = pair_rep[case]
        ws = min(max(2 * ip - NA_KH // 2, 0), rows - C_KROWS)
        for qr_l in range(2):
            r = 2 * ip + qr_l
            rs = min(max(r - NA_KH // 2, 0), rows - NA_KH)
            for tile in range(C_WIN // LANES):
                halves = []
                for kr in (ws + 2 * tile, ws + 2 * tile + 1):
                    halves.append(row_bias[kr - r + NA_KH - 1] if rs <= kr < rs + NA_KH else masked)
                tab_ref[case, 0, qr_l * GRID_W:(qr_l + 1) * GRID_W, tile * LANES:(tile + 1) * LANES] = (
                    jnp.where(lane < GRID_W, halves[0], halves[1]))


def _na_tables(rpb, rows):
    flat = rpb.reshape(-1)
    return pl.pallas_call(
        functools.partial(_na_table_kernel, rows=rows),
        out_shape=jax.ShapeDtypeStruct((C_CASES, C_HEADS, C_PAIR, C_WIN), jnp.float32),
        grid=(C_HEADS,),
        in_specs=[pl.BlockSpec(memory_space=pltpu.SMEM)],
        out_specs=pl.BlockSpec((C_CASES, 1, C_PAIR, C_WIN), lambda h: (0, h, 0, 0)),
        name="na_tables",
    )(flat)


def _na_kernel(q_ref, k_ref, v_ref, tab_ref, o_ref, *, rows):
    rb = pl.program_id(1)
    qmask = _head_lane_masks((C_PAIR, HEAD_BLOCK))
    pairs_per_step = C_RB // 2
    for t in range(pairs_per_step):
        ip = rb * pairs_per_step + t
        ws = jnp.clip(2 * ip - NA_KH // 2, 0, rows - C_KROWS)
        case = ip - ws // 2
        k0 = pl.multiple_of(ws * GRID_W, C_PAIR)
        q = q_ref[0, t * C_PAIR:(t + 1) * C_PAIR, :]
        kw = k_ref[0, pl.ds(k0, C_WIN), :]
        vw = v_ref[0, pl.ds(k0, C_WIN), :]
        qs = _stack_heads(q, qmask, HEAD_DIM ** -0.5)
        s = lax.dot_general(qs, kw, _NT, preferred_element_type=jnp.float32)
        s = s + tab_ref[case]
        mx = jnp.max(s, axis=-1, keepdims=True)
        e = jnp.exp(s - mx)
        den = jnp.sum(e, axis=-1, keepdims=True)
        o = jnp.dot(e.astype(jnp.bfloat16), vw, preferred_element_type=jnp.float32)
        o_acc = _unstack_heads(o * (1.0 / den), qmask, C_PAIR)
        o_ref[0, t * C_PAIR:(t + 1) * C_PAIR, :] = o_acc.astype(o_ref.dtype)


def _neighborhood_attention(proj, tables):
    bn, s_len, _ = proj.shape
    rows = s_len // GRID_W
    tq = C_RB * GRID_W
    qblk = COL_CQ // HEAD_BLOCK
    kblk = COL_CK // HEAD_BLOCK
    vblk = COL_CV // HEAD_BLOCK
    return pl.pallas_call(
        functools.partial(_na_kernel, rows=rows),
        out_shape=jax.ShapeDtypeStruct((bn, s_len, C_WIDTH), jnp.bfloat16),
        grid=(bn, rows // C_RB),
        in_specs=[pl.BlockSpec((1, tq, HEAD_BLOCK), lambda b, i: (b, i, qblk)),
                  pl.BlockSpec((1, s_len, HEAD_BLOCK), lambda b, i: (b, 0, kblk)),
                  pl.BlockSpec((1, s_len, HEAD_BLOCK), lambda b, i: (b, 0, vblk)),
                  pl.BlockSpec((C_CASES, C_HEADS * C_PAIR, C_WIN), lambda b, i: (0, 0, 0))],
        out_specs=pl.BlockSpec((1, tq, C_WIDTH), lambda b, i: (b, i, 0)),
        compiler_params=pltpu.CompilerParams(
            dimension_semantics=("parallel", "arbitrary"),
            vmem_limit_bytes=VMEM_LIMIT),
        name="na_attn",
    )(proj, proj, proj, tables.reshape(C_CASES, C_HEADS * C_PAIR, C_WIN))


def _merge_kernel(x_ref, gate_ref, h_ref, ya_ref, yb_ref, yc_ref, wg_ref, wb_ref, wo_ref, fg_ref,
                  o_ref, *, final_norm):
    h = h_ref[0]

    def gate_cols(k):
        return jnp.dot(h, wg_ref[:, k * D_MODEL:(k + 1) * D_MODEL], preferred_element_type=jnp.float32)

    z = gate_cols(0)
    sz = z * _sigmoid(z)
    ya = (ya_ref[0].astype(jnp.float32) * sz[:, :A_WIDTH]).astype(jnp.bfloat16)
    ybz = (yb_ref[0].astype(jnp.float32) * sz[:, A_WIDTH:A_WIDTH + B_WIDTH]).astype(jnp.bfloat16)
    ycz = (yc_ref[0].astype(jnp.float32) * sz[:, A_WIDTH + B_WIDTH:]).astype(jnp.bfloat16)

    pa = jnp.dot(ya, wb_ref[:A_WIDTH, :], preferred_element_type=jnp.float32)
    pb = jnp.dot(ybz, wb_ref[A_WIDTH:A_WIDTH + B_WIDTH, :], preferred_element_type=jnp.float32)
    pc = jnp.dot(ycz, wb_ref[A_WIDTH + B_WIDTH:, :], preferred_element_type=jnp.float32)
    merged = _sigmoid(gate_cols(1)) * pa
    merged = merged + _sigmoid(gate_cols(2)) * pb
    merged = merged + _sigmoid(gate_cols(3)) * pc
    out = jnp.dot(merged.astype(jnp.bfloat16), wo_ref[...], preferred_element_type=jnp.float32)
    xn = x_ref[0] + gate_ref[0] * out
    if final_norm:
        ms = jnp.mean(xn * xn, axis=-1, keepdims=True)
        xn = xn * lax.rsqrt(ms + RMS_EPS) * fg_ref[...]
    o_ref[0] = xn


def _merge(x, gate, h, ya, yb, yc, w_main, w_branch, w_out, final_gain, layer, final_norm):
    bn, s_len, _ = x.shape
    tm = MERGE_TM
    assert ATTN_COLS % GATE_COLS == 0 and GATE_COLS == (1 + N_BRANCHES) * D_MODEL
    gate_blk = ATTN_COLS // GATE_COLS

    def tok(width):
        return pl.BlockSpec((1, tm, width), lambda b, i: (b, i, 0))

    return pl.pallas_call(
        functools.partial(_merge_kernel, final_norm=final_norm),
        out_shape=jax.ShapeDtypeStruct((bn, s_len, D_MODEL), jnp.float32),
        grid=(bn, s_len // tm),
        in_specs=[tok(D_MODEL),
                  pl.BlockSpec((1, 1, D_MODEL), lambda b, i: (b, 0, 0)),
                  tok(D_MODEL), tok(A_WIDTH), tok(B_WIDTH), tok(C_WIDTH),
                  pl.BlockSpec((None, D_MODEL, GATE_COLS), lambda b, i: (layer, 0, gate_blk)),
                  pl.BlockSpec((None, BR_WIDTH, D_MODEL), lambda b, i: (layer, 0, 0)),
                  pl.BlockSpec((None, D_MODEL, D_MODEL), lambda b, i: (layer, 0, 0)),
                  pl.BlockSpec((1, D_MODEL), lambda b, i: (0, 0))],
        out_specs=tok(D_MODEL),
        compiler_params=pltpu.CompilerParams(
            dimension_semantics=("parallel", "parallel"),
            vmem_limit_bytes=VMEM_LIMIT),
        name="merge",
    )(x, gate, h, ya, yb, yc, w_main, w_branch, w_out, final_gain)


def _w_main_kernel(w_ref, o_ref):
    d = pl.program_id(1)
    w = w_ref[...].astype(jnp.bfloat16)

    @pl.when(d < ROPE_COLS // W_PREP_COLS)
    def _():
        src = lax.broadcasted_iota(jnp.int32, (HEAD_BLOCK, HEAD_BLOCK), 0)
        dst = lax.broadcasted_iota(jnp.int32, (HEAD_BLOCK, HEAD_BLOCK), 1)
        onehot = jnp.where(src == _rope_layout_source(dst), 1.0, 0.0).astype(jnp.bfloat16)
        for blk in range(W_PREP_COLS // HEAD_BLOCK):
            cols = slice(blk * HEAD_BLOCK, (blk + 1) * HEAD_BLOCK)
            o_ref[:, cols] = jnp.dot(w[:, cols], onehot,
                                     preferred_element_type=jnp.float32).astype(o_ref.dtype)

    @pl.when(d >= ROPE_COLS // W_PREP_COLS)
    def _():
        o_ref[...] = w


def _w_vt_kernel(w_ref, o_ref):
    w = w_ref[...].astype(jnp.bfloat16)
    r = lax.broadcasted_iota(jnp.int32, (A_WIDTH, A_WIDTH), 0)
    c = lax.broadcasted_iota(jnp.int32, (A_WIDTH, A_WIDTH), 1)
    eye = jnp.where(r == c, 1.0, 0.0).astype(jnp.bfloat16)
    o_ref[...] = lax.dot_general(eye, w, _NT, preferred_element_type=jnp.float32).astype(o_ref.dtype)


def _split_w_in(w_in):
    depth = w_in.shape[0]
    av_blk = 2 * A_QK_COLS // W_PREP_COLS
    assert 2 * A_QK_COLS % W_PREP_COLS == 0 and A_WIDTH == W_PREP_COLS and ROPE_COLS % W_PREP_COLS == 0
    w_main = pl.pallas_call(
        _w_main_kernel,
        out_shape=jax.ShapeDtypeStruct((depth, D_MODEL, MAIN_COLS), jnp.bfloat16),
        grid=(depth, MAIN_COLS // W_PREP_COLS),
        in_specs=[pl.BlockSpec((None, D_MODEL, W_PREP_COLS),
                               lambda l, d: (l, 0, jnp.where(d < av_blk, d, d + 1)))],
        out_specs=pl.BlockSpec((None, D_MODEL, W_PREP_COLS), lambda l, d: (l, 0, d)),
        name="w_main_prep",
    )(w_in)
    w_vt = pl.pallas_call(
        _w_vt_kernel,
        out_shape=jax.ShapeDtypeStruct((depth, A_WIDTH, D_MODEL), jnp.bfloat16),
        grid=(depth,),
        in_specs=[pl.BlockSpec((None, D_MODEL, A_WIDTH), lambda l: (l, 0, av_blk))],
        out_specs=pl.BlockSpec((None, A_WIDTH, D_MODEL), lambda l: (l, 0, 0)),
        name="w_vt_prep",
    )(w_in)
    return w_main, w_vt


def kernel(x, c, positions, norm_gain, w_ada, b_ada, w_in, diff_lambda, diff_subln_gain, na_rpb,
           w_branch, w_out, final_gain):
    depth = w_in.shape[0]
    bn, s_len, _ = x.shape
    rows = s_len // GRID_W
    tabs = _rope_tables(positions)
    ada = _ada_all_layers(c, w_ada, b_ada)
    fg = final_gain.reshape(1, D_MODEL)
    w_main, w_vt = _split_w_in(w_in)
    w_branch = w_branch.astype(jnp.bfloat16)
    w_out = w_out.astype(jnp.bfloat16)
    for layer in range(depth):
        shift = ada[layer, :, None, :D_MODEL]
        scale = ada[layer, :, None, D_MODEL:2 * D_MODEL]
        gate = ada[layer, :, None, 2 * D_MODEL:]
        proj, vt, h = _in_proj(x, shift, scale, norm_gain[layer].reshape(1, D_MODEL), tabs, w_main,
                               w_vt, layer)
        lambda_init = 0.8 - 0.6 * math.exp(-0.3 * layer)
        ya = _diff_attention(proj, vt, diff_lambda[layer], diff_subln_gain[layer], lambda_init)
        yb = _band_mixture(proj)
        yc = _neighborhood_attention(proj, _na_tables(na_rpb[layer], rows))
        x = _merge(x, gate, h, ya, yb, yc, w_main, w_branch, w_out, fg, layer,
                   final_norm=(layer == depth - 1))
    return x
```

```python
import functools
import math

import numpy as np
import jax
import jax.numpy as jnp
from jax import lax
from jax.experimental import pallas as pl
from jax.experimental.pallas import tpu as pltpu

D_MODEL = 1024
HEAD_DIM = 64
ROT_DIM = HEAD_DIM // 4
ROT_HALF = ROT_DIM // 2
ROPE_THETA = 500000.0
RMS_EPS = 1e-6
MASK_VALUE = -1e30
LOG2_E = math.log2(math.e)

A_HEADS = 4
A_VDIM = 2 * HEAD_DIM
A_WIDTH = A_HEADS * A_VDIM
A_QK_COLS = A_HEADS * 2 * HEAD_DIM

B_PATTERNS = ((128, 1), (512, 4), (2048, 16))
B_GROUPS = 3
B_HEADS = 4
B_WIDTH = B_HEADS * HEAD_DIM
B_COLS = B_GROUPS * B_WIDTH

GRID_W = 64
C_HEADS = 4
NA_KH = 8
NA_KW = 16
C_WIDTH = C_HEADS * HEAD_DIM

N_BRANCHES = 3
BR_WIDTH = A_WIDTH + B_WIDTH + C_WIDTH

LANES = 128
SUBLANES_F32 = 8
SUBLANES_BF16 = 16
HEADS_PER_SLAB = LANES // HEAD_DIM
HEAD_BLOCK = 4 * HEAD_DIM
ROPE_LANES = (HEAD_BLOCK // HEAD_DIM) * ROT_HALF
VMEM_LIMIT = 56 * 1024 * 1024

COL_AQ = 0
COL_AK = COL_AQ + A_QK_COLS
COL_BQ = COL_AK + A_QK_COLS
COL_BK = COL_BQ + B_COLS
COL_BV = COL_BK + B_COLS
COL_CQ = COL_BV + B_COLS
COL_CK = COL_CQ + C_WIDTH
COL_CV = COL_CK + C_WIDTH
COL_Z = COL_CV + C_WIDTH
COL_G = COL_Z + BR_WIDTH
MAIN_COLS = COL_G + N_BRANCHES * D_MODEL
ATTN_COLS = COL_Z
GATE_COLS = MAIN_COLS - ATTN_COLS
ROPE_COLS = COL_BV

W_PREP_COLS = 512
PROJ_TM = 2048
PROJ_TN = 1024
A_TQ = 2048
A_KC = 512
A_VROWS = A_VDIM + SUBLANES_BF16
TABLE_TS = 1024
TABLE_PACK = LANES // ROPE_LANES
ADA_TN = 1024
B_TQ_MAX = 128
B_UNROLL_MAX = 8
B_REACH = 64
REGROUP_ROWS = 256
REGROUP_UNROLL = 4
B_CASES = 3
C_PAIR = 2 * GRID_W
C_KROWS = 10
C_WIN = C_KROWS * GRID_W
C_CASES = 5
C_RB = 32
MERGE_TM = 1024

_NT = (((1,), (1,)), ((), ()))


def _sigmoid(x):
    return 1.0 / (1.0 + jnp.exp2(x * (-LOG2_E)))


def _rope_layout_head(col):
    l = col % LANES
    c = col // LANES
    rest = HEAD_DIM - ROT_DIM
    return jnp.where(l < ROPE_LANES, l // ROT_HALF, 2 * c + jnp.where(l >= ROPE_LANES + rest, 1, 0))


def _rope_layout_source(col):
    l = col % LANES
    c = col // LANES
    rest = HEAD_DIM - ROT_DIM
    r = l - ROPE_LANES
    upper = jnp.where(r >= rest, 1, 0)
    rotary_src = HEAD_DIM * (l // ROT_HALF) + ROT_HALF * c + l % ROT_HALF
    other_src = HEAD_DIM * (2 * c + upper) + ROT_DIM + r - rest * upper
    return jnp.where(l < ROPE_LANES, rotary_src, other_src)


def _rope_table_kernel(pos_ref, inv_ref, c_ref, s_ref):
    ang = pos_ref[0] * inv_ref[...]
    c_ref[0] = jnp.cos(ang)
    s_ref[0] = jnp.sin(ang)


def _rope_tables(positions):
    bn, s_len = positions.shape
    inv = np.float32(ROPE_THETA) ** (-np.arange(0, ROT_DIM, 2, dtype=np.float32) / np.float32(ROT_DIM))
    inv_lane = np.tile(inv, LANES // ROT_HALF).reshape(1, LANES)
    pos = jnp.repeat(positions.astype(jnp.float32).reshape(bn, s_len // TABLE_PACK, TABLE_PACK),
                     ROPE_LANES, axis=-1)
    ts = TABLE_TS // TABLE_PACK
    tab = jax.ShapeDtypeStruct((bn, s_len // TABLE_PACK, LANES), jnp.float32)
    spec = pl.BlockSpec((1, ts, LANES), lambda b, i: (b, i, 0))
    return pl.pallas_call(
        _rope_table_kernel,
        out_shape=(tab, tab),
        grid=(bn, s_len // TABLE_TS),
        in_specs=[spec, pl.BlockSpec((1, LANES), lambda b, i: (0, 0))],
        out_specs=(spec, spec),
        name="rope_tables",
    )(pos, jnp.asarray(inv_lane))


def _ada_kernel(c_ref, w_ref, b_ref, o_ref):
    c = c_ref[...]
    c_act = c * _sigmoid(c)
    o_ref[0] = jnp.dot(c_act, w_ref[0], precision=lax.Precision.HIGHEST,
                       preferred_element_type=jnp.float32) + b_ref[0]


def _ada_all_layers(c, w_ada, b_ada):
    depth = w_ada.shape[0]
    bn = c.shape[0]
    rows = SUBLANES_F32 * pl.cdiv(bn, SUBLANES_F32)
    c_pad = jnp.pad(c, ((0, rows - bn), (0, 0)))
    tn = ADA_TN
    out = pl.pallas_call(
        _ada_kernel,
        out_shape=jax.ShapeDtypeStruct((depth, rows, 3 * D_MODEL), jnp.float32),
        grid=(depth, 3 * D_MODEL // tn),
        in_specs=[pl.BlockSpec((rows, D_MODEL), lambda l, j: (0, 0)),
                  pl.BlockSpec((1, D_MODEL, tn), lambda l, j: (l, 0, j)),
                  pl.BlockSpec((1, 1, tn), lambda l, j: (l, 0, j))],
        out_specs=pl.BlockSpec((1, rows, tn), lambda l, j: (l, 0, j)),
        name="adaln",
    )(c_pad, w_ada, b_ada.reshape(depth, 1, 3 * D_MODEL))
    return out[:, :bn]


def _in_proj_kernel(x_ref, shift_ref, scale_ref, gain_ref, cp_ref, sp_ref, w_ref, wvt_ref,
                    proj_ref, vt_ref, h_ref, c_ref, s_ref):
    j = pl.program_id(2)

    @pl.when(j == 0)
    def _():
        lane = lax.broadcasted_iota(jnp.int32, cp_ref.shape[1:], 1)
        rows = cp_ref.shape[1]
        for t in range(TABLE_PACK):
            shift = (LANES - ROPE_LANES * t) % LANES
            ct = cp_ref[0] if shift == 0 else pltpu.roll(cp_ref[0], shift, 1)
            st = sp_ref[0] if shift == 0 else pltpu.roll(sp_ref[0], shift, 1)
            c_ref[pl.ds(t, rows, stride=TABLE_PACK), :] = jnp.where(lane < ROPE_LANES, ct, 1.0)
            s_ref[pl.ds(t, rows, stride=TABLE_PACK), :] = jnp.where(lane < ROPE_LANES, st, 0.0)

        xf = x_ref[0]
        ms = jnp.mean(xf * xf, axis=-1, keepdims=True)
        y = xf * lax.rsqrt(ms + RMS_EPS) * gain_ref[...]
        h = (y * (1.0 + scale_ref[0]) + shift_ref[0]).astype(jnp.bfloat16)
        h_ref[0] = h
        vt = lax.dot_general(wvt_ref[...], h, _NT, preferred_element_type=jnp.float32)
        vt_ref[0] = vt.astype(vt_ref.dtype)

    acc = jnp.dot(h_ref[0], w_ref[...], preferred_element_type=jnp.float32)
    blocks_per_tile = PROJ_TN // HEAD_BLOCK
    cc = c_ref[...]
    ss = s_ref[...]
    for blk in range(blocks_per_tile):
        rotary = (j * blocks_per_tile + blk) < (ROPE_COLS // HEAD_BLOCK)
        cb = jnp.where(rotary, cc, 1.0)
        sb = jnp.where(rotary, ss, 0.0)
        lo0 = blk * HEAD_BLOCK
        hi0 = lo0 + LANES
        lo = acc[:, lo0:hi0]
        hi = acc[:, hi0:hi0 + LANES]
        proj_ref[0, :, lo0:hi0] = (lo * cb - hi * sb).astype(proj_ref.dtype)
        proj_ref[0, :, hi0:hi0 + LANES] = (hi * cb + lo * sb).astype(proj_ref.dtype)


def _in_proj(x, shift, scale, gain, tabs, w_main, w_vt, layer):
    bn, s_len, _ = x.shape
    tm, tn = PROJ_TM, PROJ_TN
    tab_spec = pl.BlockSpec((1, tm // TABLE_PACK, LANES), lambda b, i, j: (b, i, 0))
    mod_spec = pl.BlockSpec((1, 1, D_MODEL), lambda b, i, j: (b, 0, 0))
    return pl.pallas_call(
        _in_proj_kernel,
        out_shape=(jax.ShapeDtypeStruct((bn, s_len, ATTN_COLS), jnp.bfloat16),
                   jax.ShapeDtypeStruct((bn, A_WIDTH, s_len), jnp.bfloat16),
                   jax.ShapeDtypeStruct((bn, s_len, D_MODEL), jnp.bfloat16)),
        grid=(bn, s_len // tm, ATTN_COLS // tn),
        in_specs=[pl.BlockSpec((1, tm, D_MODEL), lambda b, i, j: (b, i, 0)),
                  mod_spec, mod_spec,
                  pl.BlockSpec((1, D_MODEL), lambda b, i, j: (0, 0)),
                  tab_spec, tab_spec,
                  pl.BlockSpec((None, D_MODEL, tn), lambda b, i, j: (layer, 0, j)),
                  pl.BlockSpec((None, A_WIDTH, D_MODEL), lambda b, i, j: (layer, 0, 0))],
        out_specs=(pl.BlockSpec((1, tm, tn), lambda b, i, j: (b, i, j)),
                   pl.BlockSpec((1, A_WIDTH, tm), lambda b, i, j: (b, 0, i)),
                   pl.BlockSpec((1, tm, D_MODEL), lambda b, i, j: (b, i, 0))),
        scratch_shapes=[pltpu.VMEM((tm, LANES), jnp.float32),
                        pltpu.VMEM((tm, LANES), jnp.float32)],
        compiler_params=pltpu.CompilerParams(
            dimension_semantics=("parallel", "parallel", "arbitrary"),
            vmem_limit_bytes=VMEM_LIMIT),
        name="in_proj",
    )(x, shift, scale, gain, *tabs, w_main, w_vt)


def _diff_attn_kernel(q_ref, k_ref, vt_ref, dl_ref, gain_ref, o_ref,
                      qm_ref, vta_ref, sa_ref, sb_ref, m_ref, acc_ref, *, lambda_init, s_len):
    h = pl.program_id(1)
    tq = q_ref.shape[1]
    n_chunks = s_len // A_KC

    @pl.when(pl.program_id(2) == 0)
    def _():
        vta_ref[:A_VDIM, :] = vt_ref[0]
        row = lax.broadcasted_iota(jnp.int32, (A_VROWS - A_VDIM, s_len), 0)
        vta_ref[A_VDIM:, :] = jnp.where(row == 0, 1.0, 0.0).astype(vta_ref.dtype)

    q = q_ref[0]
    pair = _rope_layout_head(lax.broadcasted_iota(jnp.int32, q.shape, 1))
    base = 2 * (h % 2)
    for c in range(2):
        qsel = jnp.where(pair == base + c, q, jnp.zeros_like(q)).astype(jnp.float32)
        qm_ref[c] = (qsel * (HEAD_DIM ** -0.5 * LOG2_E)).astype(qm_ref.dtype)
        m_ref[c] = jnp.full((1, tq), MASK_VALUE, jnp.float32)
        acc_ref[c] = jnp.zeros((A_VROWS, tq), jnp.float32)

    def scores(i, s_ref):
        kc = k_ref[0, pl.ds(pl.multiple_of(i * A_KC, A_KC), A_KC), :]
        for c in range(2):
            s_ref[c] = lax.dot_general(kc, qm_ref[c], _NT,
                                       preferred_element_type=jnp.float32)

    def update(i, s_ref):
        vt_c = vta_ref[:, pl.ds(pl.multiple_of(i * A_KC, A_KC), A_KC)]
        for c in range(2):
            st = s_ref[c]
            m_old = m_ref[c]
            m_new = jnp.maximum(m_old, jnp.max(st, axis=0, keepdims=True))
            alpha = jnp.exp2(m_old - m_new)
            e = jnp.exp2(st - m_new).astype(jnp.bfloat16)
            pv = jnp.dot(vt_c, e, preferred_element_type=jnp.float32)
            acc_ref[c] = alpha * acc_ref[c] + pv
            m_ref[c] = m_new

    scores(0, sa_ref)

    def body(j, carry):
        scores(2 * j + 1, sb_ref)
        update(2 * j, sa_ref)
        scores(2 * j + 2, sa_ref)
        update(2 * j + 1, sb_ref)
        return carry

    lax.fori_loop(0, n_chunks // 2 - 1, body, 0)
    scores(n_chunks - 1, sb_ref)
    update(n_chunks - 2, sa_ref)
    update(n_chunks - 1, sb_ref)

    dl = dl_ref[...]
    lam = (jnp.exp(jnp.sum(dl[0:1] * dl[1:2], axis=1, keepdims=True))
           - jnp.exp(jnp.sum(dl[2:3] * dl[3:4], axis=1, keepdims=True)) + lambda_init)
    maps = [acc_ref[c, :A_VDIM, :] * (1.0 / acc_ref[c, A_VDIM:A_VDIM + 1, :]) for c in range(2)]
    ot = maps[0] - lam * maps[1]
    ms = jnp.mean(ot * ot, axis=0, keepdims=True)
    y = ot * lax.rsqrt(ms + RMS_EPS) * gain_ref[...] * (1.0 - lambda_init)
    o_ref[0] = y.T.astype(o_ref.dtype)


def _diff_attention(proj, vt, diff_lambda, subln_gain, lambda_init):
    bn, s_len, _ = proj.shape
    tq = A_TQ
    qblk = COL_AQ // HEAD_BLOCK
    kblk = COL_AK // HEAD_BLOCK
    assert s_len % (2 * A_KC) == 0
    return pl.pallas_call(
        functools.partial(_diff_attn_kernel, lambda_init=lambda_init, s_len=s_len),
        out_shape=jax.ShapeDtypeStruct((bn, s_len, A_WIDTH), jnp.bfloat16),
        grid=(bn, A_HEADS, s_len // tq),
        in_specs=[pl.BlockSpec((1, tq, HEAD_BLOCK), lambda b, h, i: (b, i, qblk + h // 2)),
                  pl.BlockSpec((1, s_len, HEAD_BLOCK), lambda b, h, i: (b, 0, kblk + h // 2)),
                  pl.BlockSpec((1, A_VDIM, s_len), lambda b, h, i: (b, h, 0)),
                  pl.BlockSpec((4, HEAD_DIM), lambda b, h, i: (0, 0)),
                  pl.BlockSpec((A_VDIM, 1), lambda b, h, i: (0, 0))],
        out_specs=pl.BlockSpec((1, tq, A_VDIM), lambda b, h, i: (b, i, h)),
        scratch_shapes=[pltpu.VMEM((2, tq, HEAD_BLOCK), jnp.bfloat16),
                        pltpu.VMEM((A_VROWS, s_len), jnp.bfloat16),
                        pltpu.VMEM((2, A_KC, tq), jnp.float32),
                        pltpu.VMEM((2, A_KC, tq), jnp.float32),
                        pltpu.VMEM((2, 1, tq), jnp.float32),
                        pltpu.VMEM((2, A_VROWS, tq), jnp.float32)],
        compiler_params=pltpu.CompilerParams(
            dimension_semantics=("parallel", "parallel", "arbitrary"),
            vmem_limit_bytes=VMEM_LIMIT),
        name="diff_attn",
    )(proj, proj, vt, diff_lambda, subln_gain.reshape(A_VDIM, 1))


def _head_lane_masks(shape, rotary_layout=False):
    col = lax.broadcasted_iota(jnp.int32, shape, 1)
    head = _rope_layout_head(col) if rotary_layout else col // HEAD_DIM
    return [head == h for h in range(HEAD_BLOCK // HEAD_DIM)]


def _stack_heads(q, qmask, scale):
    q = (q.astype(jnp.float32) * scale).astype(q.dtype)
    zero = jnp.zeros_like(q)
    return jnp.concatenate([jnp.where(mk, q, zero) for mk in qmask], axis=0)


def _unstack_heads(x, qmask, tq):
    out = x[:tq]
    for h in range(1, len(qmask)):
        out = jnp.where(qmask[h], x[h * tq:(h + 1) * tq], out)
    return out


def _band_group(q_ref, k_ref, v_ref, y_ref, band_ref, qd_ref, kd_ref, vd_ref,
                run_o_ref, run_l_ref, *, dilation, s_len, first, last):
    length = s_len // dilation
    tq = min(B_TQ_MAX, length)
    win = min(tq + 2 * B_REACH, length)
    unroll = min(B_UNROLL_MAX, s_len // tq)
    nblk = length // tq
    assert length % tq == 0 and (s_len // tq) % unroll == 0
    assert REGROUP_ROWS % dilation == 0 and s_len % REGROUP_ROWS == 0
    assert dilation == 1 or (REGROUP_ROWS // dilation) % SUBLANES_BF16 == 0
    assert not (last and dilation > 1)
    qmask = _head_lane_masks((tq, HEAD_BLOCK), rotary_layout=True)

    if dilation > 1:
        per = REGROUP_ROWS // dilation
        out_row = lax.broadcasted_iota(jnp.int32, (REGROUP_ROWS, REGROUP_ROWS), 0)
        in_row = lax.broadcasted_iota(jnp.int32, (REGROUP_ROWS, REGROUP_ROWS), 1)
        onehot = jnp.where(in_row == (out_row % per) * dilation + out_row // per, 1.0, 0.0)
        onehot = onehot.astype(jnp.bfloat16)
        def regroup(i, carry):
            for u in range(REGROUP_UNROLL):
                blk = i * REGROUP_UNROLL + u
                r0 = pl.multiple_of(blk * REGROUP_ROWS, REGROUP_ROWS)
                for src, dst in ((q_ref, qd_ref), (k_ref, kd_ref), (v_ref, vd_ref)):
                    y = jnp.dot(onehot, src[0, pl.ds(r0, REGROUP_ROWS), :],
                                preferred_element_type=jnp.float32).astype(dst.dtype)
                    for m in range(dilation):
                        d0 = pl.multiple_of(m * length + blk * per, per)
                        dst[pl.ds(d0, per), :] = y[m * per:(m + 1) * per, :]
            return carry

        lax.fori_loop(0, s_len // (REGROUP_ROWS * REGROUP_UNROLL), regroup, 0)
        q_src, k_src, v_src = qd_ref, kd_ref, vd_ref
    else:
        q_src, k_src, v_src = q_ref.at[0], k_ref.at[0], v_ref.at[0]

    def one_block(t):
        m = t // nblk
        q0 = pl.multiple_of((t % nblk) * tq, tq)
        ks = pl.multiple_of(jnp.clip(q0 - B_REACH, 0, length - win), B_REACH)
        base = pl.multiple_of(m * length, tq)
        if dilation > 1:
            rows = pl.ds(q0 * dilation + m, tq, stride=dilation)
        else:
            rows = pl.ds(q0, tq)
        q = q_src[pl.ds(base + q0, tq), :]
        kw = k_src[pl.ds(base + ks, win), :]
        vw = v_src[pl.ds(base + ks, win), :]
        q = (q.astype(jnp.float32) * (HEAD_DIM ** -0.5 * LOG2_E)).astype(q.dtype)
        zero = jnp.zeros_like(q)
        band = band_ref[(q0 - ks) // B_REACH]
        for slab in range(HEAD_BLOCK // LANES):
            lanes = slice(slab * LANES, (slab + 1) * LANES)
            heads = range(slab * HEADS_PER_SLAB, (slab + 1) * HEADS_PER_SLAB)
            qs = jnp.concatenate([jnp.where(qmask[h], q, zero) for h in heads], axis=0)
            s = lax.dot_general(qs, kw, _NT, preferred_element_type=jnp.float32) + band
            mx = jnp.max(s, axis=-1, keepdims=True)
            e = jnp.exp2(s - mx)
            den = jnp.sum(e, axis=-1, keepdims=True)
            o = jnp.dot(e.astype(jnp.bfloat16), vw[:, lanes], preferred_element_type=jnp.float32)
            o = o * (1.0 / den)
            lse = jnp.broadcast_to(mx + jnp.log2(den), o.shape)
            first_head = lax.broadcasted_iota(jnp.int32, (tq, LANES), 1) < HEAD_DIM
            o_s = jnp.where(first_head, o[:tq], o[tq:])
            l_s = jnp.where(first_head, lse[:tq], lse[tq:])
            if not first:
                o_run, l_run = run_o_ref[slab, rows, :], run_l_ref[slab, rows, :]
                l_max = jnp.maximum(l_run, l_s)
                w_run, w_new = jnp.exp2(l_run - l_max), jnp.exp2(l_s - l_max)
                total = w_run + w_new
                o_s = (o_run * w_run + o_s * w_new) * (1.0 / total)
                l_s = l_max + jnp.log2(total)
            if last:
                y_ref[0, rows, lanes] = o_s.astype(y_ref.dtype)
            else:
                run_o_ref[slab, rows, :] = o_s
                run_l_ref[slab, rows, :] = l_s

    def body(i, carry):
        for u in range(unroll):
            one_block(i * unroll + u)
        return carry

    lax.fori_loop(0, dilation * nblk // unroll, body, 0)


def _band_mix_kernel(q_ref, k_ref, v_ref, y_ref, band_ref, *scratch, s_len, dilations):
    g = pl.program_id(1)

    tq = band_ref.shape[1] // HEADS_PER_SLAB
    row = lax.broadcasted_iota(jnp.int32, band_ref.shape[1:], 0) % tq
    col = lax.broadcasted_iota(jnp.int32, band_ref.shape[1:], 1)
    for case in range(B_CASES):
        valid = jnp.abs(col - case * B_REACH - row) <= B_REACH
        band_ref[case] = jnp.where(valid, 0.0, MASK_VALUE)

    for step, dilation in enumerate(dilations):
        @pl.when(g == step)
        def _(step=step, dilation=dilation):
            _band_group(q_ref, k_ref, v_ref, y_ref, band_ref, *scratch, dilation=dilation,
                        s_len=s_len, first=(step == 0), last=(step == len(dilations) - 1))


def _band_mixture(proj):
    bn, s_len, _ = proj.shape
    n_groups = len(B_PATTERNS)
    order = sorted(range(n_groups), key=lambda g: -B_PATTERNS[g][1])
    dilations = tuple(B_PATTERNS[g][1] for g in order)
    for window, dilation in B_PATTERNS:
        assert (window // 2) // dilation == B_REACH
    assert order == list(range(n_groups - 1, -1, -1))

    def in_spec(col0):
        blk = col0 // HEAD_BLOCK
        return pl.BlockSpec((1, s_len, HEAD_BLOCK), lambda b, g: (b, 0, blk + n_groups - 1 - g))

    slabs = HEAD_BLOCK // LANES
    regrouped = pltpu.VMEM((s_len, HEAD_BLOCK), jnp.bfloat16)
    state = pltpu.VMEM((slabs, s_len, LANES), jnp.float32)
    return pl.pallas_call(
        functools.partial(_band_mix_kernel, s_len=s_len, dilations=dilations),
        out_shape=jax.ShapeDtypeStruct((bn, s_len, B_WIDTH), jnp.bfloat16),
        grid=(bn, n_groups),
        in_specs=[in_spec(COL_BQ), in_spec(COL_BK), in_spec(COL_BV)],
        out_specs=pl.BlockSpec((1, s_len, B_WIDTH), lambda b, g: (b, 0, 0)),
        scratch_shapes=[pltpu.VMEM((B_CASES, HEADS_PER_SLAB * B_TQ_MAX, B_TQ_MAX + 2 * B_REACH), jnp.float32),
                        regrouped, regrouped, regrouped,
                        state, state],
        compiler_params=pltpu.CompilerParams(
            dimension_semantics=("parallel", "arbitrary"),
            vmem_limit_bytes=VMEM_LIMIT),
        name="band_mix",
    )(proj, proj, proj)


def _na_table_kernel(rpb_ref, tab_ref, *, rows):
    h = pl.program_id(0)
    n_dr = 2 * NA_KH - 1
    n_dc = 2 * NA_KW - 1
    shape = (GRID_W, LANES)
    qc = lax.broadcasted_iota(jnp.int32, shape, 0)
    lane = lax.broadcasted_iota(jnp.int32, shape, 1)
    kc = lane % GRID_W
    dc_i = jnp.clip(kc - qc + (NA_KW - 1), 0, n_dc - 1)
    cs = jnp.clip(qc - NA_KW // 2, 0, GRID_W - NA_KW)
    col_valid = (kc >= cs) & (kc < cs + NA_KW)
    masked = jnp.full(shape, MASK_VALUE, jnp.float32)

    row_bias = []
    for dr in range(n_dr):
        base = (h * n_dr + dr) * n_dc
        t = masked
        for dc in range(n_dc):
            t = jnp.where(dc_i == dc, rpb_ref[base + dc], t)
        row_bias.append(jnp.where(col_valid, t, MASK_VALUE))

    pair_rep = (0, 1, 2, rows // 2 - 2, rows // 2 - 1)
    for case in range(C_CASES):
        ip = pair_rep[case]
        ws = min(max(2 * ip - NA_KH // 2, 0), rows - C_KROWS)
        for qr_l in range(2):
            r = 2 * ip + qr_l
            rs = min(max(r - NA_KH // 2, 0), rows - NA_KH)
            for tile in range(C_WIN // LANES):
                halves = []
                for kr in (ws + 2 * tile, ws + 2 * tile + 1):
                    halves.append(row_bias[kr - r + NA_KH - 1] if rs <= kr < rs + NA_KH else masked)
                tab_ref[case, 0, qr_l * GRID_W:(qr_l + 1) * GRID_W, tile * LANES:(tile + 1) * LANES] = (
                    jnp.where(lane < GRID_W, halves[0], halves[1]))


def _na_tables(rpb, rows):
    flat = rpb.reshape(-1)
    return pl.pallas_call(
        functools.partial(_na_table_kernel, rows=rows),
        out_shape=jax.ShapeDtypeStruct((C_CASES, C_HEADS, C_PAIR, C_WIN), jnp.float32),
        grid=(C_HEADS,),
        in_specs=[pl.BlockSpec(memory_space=pltpu.SMEM)],
        out_specs=pl.BlockSpec((C_CASES, 1, C_PAIR, C_WIN), lambda h: (0, h, 0, 0)),
        name="na_tables",
    )(flat)


def _na_kernel(q_ref, k_ref, v_ref, tab_ref, o_ref, *, rows):
    rb = pl.program_id(1)
    qmask = _head_lane_masks((C_PAIR, HEAD_BLOCK))
    pairs_per_step = C_RB // 2
    for t in range(pairs_per_step):
        ip = rb * pairs_per_step + t
        ws = jnp.clip(2 * ip - NA_KH // 2, 0, rows - C_KROWS)
        case = ip - ws // 2
        k0 = pl.multiple_of(ws * GRID_W, C_PAIR)
        q = q_ref[0, t * C_PAIR:(t + 1) * C_PAIR, :]
        kw = k_ref[0, pl.ds(k0, C_WIN), :]
        vw = v_ref[0, pl.ds(k0, C_WIN), :]
        qs = _stack_heads(q, qmask, HEAD_DIM ** -0.5)
        s = lax.dot_general(qs, kw, _NT, preferred_element_type=jnp.float32)
        s = s + tab_ref[case]
        mx = jnp.max(s, axis=-1, keepdims=True)
        e = jnp.exp(s - mx)
        den = jnp.sum(e, axis=-1, keepdims=True)
        o = jnp.dot(e.astype(jnp.bfloat16), vw, preferred_element_type=jnp.float32)
        o_acc = _unstack_heads(o * (1.0 / den), qmask, C_PAIR)
        o_ref[0, t * C_PAIR:(t + 1) * C_PAIR, :] = o_acc.astype(o_ref.dtype)


def _neighborhood_attention(proj, tables):
    bn, s_len, _ = proj.shape
    rows = s_len // GRID_W
    tq = C_RB * GRID_W
    qblk = COL_CQ // HEAD_BLOCK
    kblk = COL_CK // HEAD_BLOCK
    vblk = COL_CV // HEAD_BLOCK
    return pl.pallas_call(
        functools.partial(_na_kernel, rows=rows),
        out_shape=jax.ShapeDtypeStruct((bn, s_len, C_WIDTH), jnp.bfloat16),
        grid=(bn, rows // C_RB),
        in_specs=[pl.BlockSpec((1, tq, HEAD_BLOCK), lambda b, i: (b, i, qblk)),
                  pl.BlockSpec((1, s_len, HEAD_BLOCK), lambda b, i: (b, 0, kblk)),
                  pl.BlockSpec((1, s_len, HEAD_BLOCK), lambda b, i: (b, 0, vblk)),
                  pl.BlockSpec((C_CASES, C_HEADS * C_PAIR, C_WIN), lambda b, i: (0, 0, 0))],
        out_specs=pl.BlockSpec((1, tq, C_WIDTH), lambda b, i: (b, i, 0)),
        compiler_params=pltpu.CompilerParams(
            dimension_semantics=("parallel", "arbitrary"),
            vmem_limit_bytes=VMEM_LIMIT),
        name="na_attn",
    )(proj, proj, proj, tables.reshape(C_CASES, C_HEADS * C_PAIR, C_WIN))


def _merge_kernel(x_ref, gate_ref, h_ref, ya_ref, yb_ref, yc_ref, wg_ref, wb_ref, wo_ref, fg_ref,
                  o_ref, *, final_norm):
    h = h_ref[0]

    def gate_cols(k):
        return jnp.dot(h, wg_ref[:, k * D_MODEL:(k + 1) * D_MODEL], preferred_element_type=jnp.float32)

    z = gate_cols(0)
    sz = z * _sigmoid(z)
    ya = (ya_ref[0].astype(jnp.float32) * sz[:, :A_WIDTH]).astype(jnp.bfloat16)
    ybz = (yb_ref[0].astype(jnp.float32) * sz[:, A_WIDTH:A_WIDTH + B_WIDTH]).astype(jnp.bfloat16)
    ycz = (yc_ref[0].astype(jnp.float32) * sz[:, A_WIDTH + B_WIDTH:]).astype(jnp.bfloat16)

    pa = jnp.dot(ya, wb_ref[:A_WIDTH, :], preferred_element_type=jnp.float32)
    pb = jnp.dot(ybz, wb_ref[A_WIDTH:A_WIDTH + B_WIDTH, :], preferred_element_type=jnp.float32)
    pc = jnp.dot(ycz, wb_ref[A_WIDTH + B_WIDTH:, :], preferred_element_type=jnp.float32)
    merged = _sigmoid(gate_cols(1)) * pa
    merged = merged + _sigmoid(gate_cols(2)) * pb
    merged = merged + _sigmoid(gate_cols(3)) * pc
    out = jnp.dot(merged.astype(jnp.bfloat16), wo_ref[...], preferred_element_type=jnp.float32)
    xn = x_ref[0] + gate_ref[0] * out
    if final_norm:
        ms = jnp.mean(xn * xn, axis=-1, keepdims=True)
        xn = xn * lax.rsqrt(ms + RMS_EPS) * fg_ref[...]
    o_ref[0] = xn


def _merge(x, gate, h, ya, yb, yc, w_main, w_branch, w_out, final_gain, layer, final_norm):
    bn, s_len, _ = x.shape
    tm = MERGE_TM
    assert ATTN_COLS % GATE_COLS == 0 and GATE_COLS == (1 + N_BRANCHES) * D_MODEL
    gate_blk = ATTN_COLS // GATE_COLS

    def tok(width):
        return pl.BlockSpec((1, tm, width), lambda b, i: (b, i, 0))

    return pl.pallas_call(
        functools.partial(_merge_kernel, final_norm=final_norm),
        out_shape=jax.ShapeDtypeStruct((bn, s_len, D_MODEL), jnp.float32),
        grid=(bn, s_len // tm),
        in_specs=[tok(D_MODEL),
                  pl.BlockSpec((1, 1, D_MODEL), lambda b, i: (b, 0, 0)),
                  tok(D_MODEL), tok(A_WIDTH), tok(B_WIDTH), tok(C_WIDTH),
                  pl.BlockSpec((None, D_MODEL, GATE_COLS), lambda b, i: (layer, 0, gate_blk)),
                  pl.BlockSpec((None, BR_WIDTH, D_MODEL), lambda b, i: (layer, 0, 0)),
                  pl.BlockSpec((None, D_MODEL, D_MODEL), lambda b, i: (layer, 0, 0)),
                  pl.BlockSpec((1, D_MODEL), lambda b, i: (0, 0))],
        out_specs=tok(D_MODEL),
        compiler_params=pltpu.CompilerParams(
            dimension_semantics=("parallel", "parallel"),
            vmem_limit_bytes=VMEM_LIMIT),
        name="merge",
    )(x, gate, h, ya, yb, yc, w_main, w_branch, w_out, final_gain)


def _w_main_kernel(w_ref, o_ref):
    d = pl.program_id(1)
    w = w_ref[...].astype(jnp.bfloat16)

    @pl.when(d < ROPE_COLS // W_PREP_COLS)
    def _():
        src = lax.broadcasted_iota(jnp.int32, (HEAD_BLOCK, HEAD_BLOCK), 0)
        dst = lax.broadcasted_iota(jnp.int32, (HEAD_BLOCK, HEAD_BLOCK), 1)
        onehot = jnp.where(src == _rope_layout_source(dst), 1.0, 0.0).astype(jnp.bfloat16)
        for blk in range(W_PREP_COLS // HEAD_BLOCK):
            cols = slice(blk * HEAD_BLOCK, (blk + 1) * HEAD_BLOCK)
            o_ref[:, cols] = jnp.dot(w[:, cols], onehot,
                                     preferred_element_type=jnp.float32).astype(o_ref.dtype)

    @pl.when(d >= ROPE_COLS // W_PREP_COLS)
    def _():
        o_ref[...] = w


def _w_vt_kernel(w_ref, o_ref):
    w = w_ref[...].astype(jnp.bfloat16)
    r = lax.broadcasted_iota(jnp.int32, (A_WIDTH, A_WIDTH), 0)
    c = lax.broadcasted_iota(jnp.int32, (A_WIDTH, A_WIDTH), 1)
    eye = jnp.where(r == c, 1.0, 0.0).astype(jnp.bfloat16)
    o_ref[...] = lax.dot_general(eye, w, _NT, preferred_element_type=jnp.float32).astype(o_ref.dtype)


def _split_w_in(w_in):
    depth = w_in.shape[0]
    av_blk = 2 * A_QK_COLS // W_PREP_COLS
    assert 2 * A_QK_COLS % W_PREP_COLS == 0 and A_WIDTH == W_PREP_COLS and ROPE_COLS % W_PREP_COLS == 0
    w_main = pl.pallas_call(
        _w_main_kernel,
        out_shape=jax.ShapeDtypeStruct((depth, D_MODEL, MAIN_COLS), jnp.bfloat16),
        grid=(depth, MAIN_COLS // W_PREP_COLS),
        in_specs=[pl.BlockSpec((None, D_MODEL, W_PREP_COLS),
                               lambda l, d: (l, 0, jnp.where(d < av_blk, d, d + 1)))],
        out_specs=pl.BlockSpec((None, D_MODEL, W_PREP_COLS), lambda l, d: (l, 0, d)),
        name="w_main_prep",
    )(w_in)
    w_vt = pl.pallas_call(
        _w_vt_kernel,
        out_shape=jax.ShapeDtypeStruct((depth, A_WIDTH, D_MODEL), jnp.bfloat16),
        grid=(depth,),
        in_specs=[pl.BlockSpec((None, D_MODEL, A_WIDTH), lambda l: (l, 0, av_blk))],
        out_specs=pl.BlockSpec((None, A_WIDTH, D_MODEL), lambda l: (l, 0, 0)),
        name="w_vt_prep",
    )(w_in)
    return w_main, w_vt


def kernel(x, c, positions, norm_gain, w_ada, b_ada, w_in, diff_lambda, diff_subln_gain, na_rpb,
           w_branch, w_out, final_gain):
    depth = w_in.shape[0]
    bn, s_len, _ = x.shape
    rows = s_len // GRID_W
    tabs = _rope_tables(positions)
    ada = _ada_all_layers(c, w_ada, b_ada)
    fg = final_gain.reshape(1, D_MODEL)
    w_main, w_vt = _split_w_in(w_in)
    w_branch = w_branch.astype(jnp.bfloat16)
    w_out = w_out.astype(jnp.bfloat16)
    for layer in range(depth):
        shift = ada[layer, :, None, :D_MODEL]
        scale = ada[layer, :, None, D_MODEL:2 * D_MODEL]
        gate = ada[layer, :, None, 2 * D_MODEL:]
        proj, vt, h = _in_proj(x, shift, scale, norm_gain[layer].reshape(1, D_MODEL), tabs, w_main,
                               w_vt, layer)
        lambda_init = 0.8 - 0.6 * math.exp(-0.3 * layer)
        ya = _diff_attention(proj, vt, diff_lambda[layer], diff_subln_gain[layer], lambda_init)
        yb = _band_mixture(proj)
        yc = _neighborhood_attention(proj, _na_tables(na_rpb[layer], rows))
        x = _merge(x, gate, h, ya, yb, yc, w_main, w_branch, w_out, fg, layer,
                   final_norm=(layer == depth - 1))
    return x
```

```python
import functools
import math

import numpy as np
import jax
import jax.numpy as jnp
from jax import lax
from jax.experimental import pallas as pl
from jax.experimental.pallas import tpu as pltpu

D_MODEL = 1024
HEAD_DIM = 64
ROT_DIM = HEAD_DIM // 4
ROT_HALF = ROT_DIM // 2
ROPE_THETA = 500000.0
RMS_EPS = 1e-6
MASK_VALUE = -1e30
LOG2_E = math.log2(math.e)

A_HEADS = 4
A_VDIM = 2 * HEAD_DIM
A_WIDTH = A_HEADS * A_VDIM
A_QK_COLS = A_HEADS * 2 * HEAD_DIM

B_PATTERNS = ((128, 1), (512, 4), (2048, 16))
B_GROUPS = 3
B_HEADS = 4
B_WIDTH = B_HEADS * HEAD_DIM
B_COLS = B_GROUPS * B_WIDTH

GRID_W = 64
C_HEADS = 4
NA_KH = 8
NA_KW = 16
C_WIDTH = C_HEADS * HEAD_DIM

N_BRANCHES = 3
BR_WIDTH = A_WIDTH + B_WIDTH + C_WIDTH

LANES = 128
SUBLANES_F32 = 8
SUBLANES_BF16 = 16
HEADS_PER_SLAB = LANES // HEAD_DIM
HEAD_BLOCK = 4 * HEAD_DIM
ROPE_LANES = (HEAD_BLOCK // HEAD_DIM) * ROT_HALF
VMEM_LIMIT = 56 * 1024 * 1024

COL_AQ = 0
COL_AK = COL_AQ + A_QK_COLS
COL_BQ = COL_AK + A_QK_COLS
COL_BK = COL_BQ + B_COLS
COL_BV = COL_BK + B_COLS
COL_CQ = COL_BV + B_COLS
COL_CK = COL_CQ + C_WIDTH
COL_CV = COL_CK + C_WIDTH
COL_Z = COL_CV + C_WIDTH
COL_G = COL_Z + BR_WIDTH
MAIN_COLS = COL_G + N_BRANCHES * D_MODEL
ATTN_COLS = COL_Z
GATE_COLS = MAIN_COLS - ATTN_COLS
ROPE_COLS = COL_BV

W_PREP_COLS = 512
PROJ_TM = 2048
PROJ_TN = 1024
A_TQ = 2048
A_KC = 512
A_VROWS = A_VDIM + SUBLANES_BF16
TABLE_TS = 1024
TABLE_PACK = LANES // ROPE_LANES
ADA_TN = 1024
B_TQ_MAX = 128
B_UNROLL_MAX = 8
B_REACH = 64
REGROUP_ROWS = 256
REGROUP_UNROLL = 4
B_CASES = 3
C_WIN = NA_KH * GRID_W
C_CASES = NA_KH
C_RB = 32
C_UNROLL = 16
MERGE_TM = 1024

_NT = (((1,), (1,)), ((), ()))


def _sigmoid(x):
    return 1.0 / (1.0 + jnp.exp2(x * (-LOG2_E)))


def _rope_layout_head(col):
    l = col % LANES
    c = col // LANES
    rest = HEAD_DIM - ROT_DIM
    return jnp.where(l < ROPE_LANES, l // ROT_HALF, 2 * c + jnp.where(l >= ROPE_LANES + rest, 1, 0))


def _rope_layout_source(col):
    l = col % LANES
    c = col // LANES
    rest = HEAD_DIM - ROT_DIM
    r = l - ROPE_LANES
    upper = jnp.where(r >= rest, 1, 0)
    rotary_src = HEAD_DIM * (l // ROT_HALF) + ROT_HALF * c + l % ROT_HALF
    other_src = HEAD_DIM * (2 * c + upper) + ROT_DIM + r - rest * upper
    return jnp.where(l < ROPE_LANES, rotary_src, other_src)


def _rope_table_kernel(pos_ref, inv_ref, c_ref, s_ref):
    ang = pos_ref[0] * inv_ref[...]
    c_ref[0] = jnp.cos(ang)
    s_ref[0] = jnp.sin(ang)


def _rope_tables(positions):
    bn, s_len = positions.shape
    inv = np.float32(ROPE_THETA) ** (-np.arange(0, ROT_DIM, 2, dtype=np.float32) / np.float32(ROT_DIM))
    inv_lane = np.tile(inv, LANES // ROT_HALF).reshape(1, LANES)
    pos = jnp.repeat(positions.astype(jnp.float32).reshape(bn, s_len // TABLE_PACK, TABLE_PACK),
                     ROPE_LANES, axis=-1)
    ts = TABLE_TS // TABLE_PACK
    tab = jax.ShapeDtypeStruct((bn, s_len // TABLE_PACK, LANES), jnp.float32)
    spec = pl.BlockSpec((1, ts, LANES), lambda b, i: (b, i, 0))
    return pl.pallas_call(
        _rope_table_kernel,
        out_shape=(tab, tab),
        grid=(bn, s_len // TABLE_TS),
        in_specs=[spec, pl.BlockSpec((1, LANES), lambda b, i: (0, 0))],
        out_specs=(spec, spec),
        name="rope_tables",
    )(pos, jnp.asarray(inv_lane))


def _ada_kernel(c_ref, w_ref, b_ref, o_ref):
    c = c_ref[...]
    c_act = c * _sigmoid(c)
    o_ref[0] = jnp.dot(c_act, w_ref[0], precision=lax.Precision.HIGHEST,
                       preferred_element_type=jnp.float32) + b_ref[0]


def _ada_all_layers(c, w_ada, b_ada):
    depth = w_ada.shape[0]
    bn = c.shape[0]
    rows = SUBLANES_F32 * pl.cdiv(bn, SUBLANES_F32)
    c_pad = jnp.pad(c, ((0, rows - bn), (0, 0)))
    tn = ADA_TN
    out = pl.pallas_call(
        _ada_kernel,
        out_shape=jax.ShapeDtypeStruct((depth, rows, 3 * D_MODEL), jnp.float32),
        grid=(depth, 3 * D_MODEL // tn),
        in_specs=[pl.BlockSpec((rows, D_MODEL), lambda l, j: (0, 0)),
                  pl.BlockSpec((1, D_MODEL, tn), lambda l, j: (l, 0, j)),
                  pl.BlockSpec((1, 1, tn), lambda l, j: (l, 0, j))],
        out_specs=pl.BlockSpec((1, rows, tn), lambda l, j: (l, 0, j)),
        name="adaln",
    )(c_pad, w_ada, b_ada.reshape(depth, 1, 3 * D_MODEL))
    return out[:, :bn]


def _in_proj_kernel(x_ref, shift_ref, scale_ref, gain_ref, cp_ref, sp_ref, w_ref, wvt_ref,
                    proj_ref, vt_ref, h_ref, c_ref, s_ref):
    j = pl.program_id(2)

    @pl.when(j == 0)
    def _():
        lane = lax.broadcasted_iota(jnp.int32, cp_ref.shape[1:], 1)
        rows = cp_ref.shape[1]
        for t in range(TABLE_PACK):
            shift = (LANES - ROPE_LANES * t) % LANES
            ct = cp_ref[0] if shift == 0 else pltpu.roll(cp_ref[0], shift, 1)
            st = sp_ref[0] if shift == 0 else pltpu.roll(sp_ref[0], shift, 1)
            c_ref[pl.ds(t, rows, stride=TABLE_PACK), :] = jnp.where(lane < ROPE_LANES, ct, 1.0)
            s_ref[pl.ds(t, rows, stride=TABLE_PACK), :] = jnp.where(lane < ROPE_LANES, st, 0.0)

        xf = x_ref[0]
        ms = jnp.mean(xf * xf, axis=-1, keepdims=True)
        y = xf * lax.rsqrt(ms + RMS_EPS) * gain_ref[...]
        h = (y * (1.0 + scale_ref[0]) + shift_ref[0]).astype(jnp.bfloat16)
        h_ref[0] = h
        vt = lax.dot_general(wvt_ref[...], h, _NT, preferred_element_type=jnp.float32)
        vt_ref[0] = vt.astype(vt_ref.dtype)

    acc = jnp.dot(h_ref[0], w_ref[...], preferred_element_type=jnp.float32)
    blocks_per_tile = PROJ_TN // HEAD_BLOCK
    cc = c_ref[...]
    ss = s_ref[...]
    for blk in range(blocks_per_tile):
        rotary = (j * blocks_per_tile + blk) < (ROPE_COLS // HEAD_BLOCK)
        cb = jnp.where(rotary, cc, 1.0)
        sb = jnp.where(rotary, ss, 0.0)
        lo0 = blk * HEAD_BLOCK
        hi0 = lo0 + LANES
        lo = acc[:, lo0:hi0]
        hi = acc[:, hi0:hi0 + LANES]
        proj_ref[0, :, lo0:hi0] = (lo * cb - hi * sb).astype(proj_ref.dtype)
        proj_ref[0, :, hi0:hi0 + LANES] = (hi * cb + lo * sb).astype(proj_ref.dtype)


def _in_proj(x, shift, scale, gain, tabs, w_main, w_vt, layer):
    bn, s_len, _ = x.shape
    tm, tn = PROJ_TM, PROJ_TN
    tab_spec = pl.BlockSpec((1, tm // TABLE_PACK, LANES), lambda b, i, j: (b, i, 0))
    mod_spec = pl.BlockSpec((1, 1, D_MODEL), lambda b, i, j: (b, 0, 0))
    return pl.pallas_call(
        _in_proj_kernel,
        out_shape=(jax.ShapeDtypeStruct((bn, s_len, ATTN_COLS), jnp.bfloat16),
                   jax.ShapeDtypeStruct((bn, A_WIDTH, s_len), jnp.bfloat16),
                   jax.ShapeDtypeStruct((bn, s_len, D_MODEL), jnp.bfloat16)),
        grid=(bn, s_len // tm, ATTN_COLS // tn),
        in_specs=[pl.BlockSpec((1, tm, D_MODEL), lambda b, i, j: (b, i, 0)),
                  mod_spec, mod_spec,
                  pl.BlockSpec((1, D_MODEL), lambda b, i, j: (0, 0)),
                  tab_spec, tab_spec,
                  pl.BlockSpec((None, D_MODEL, tn), lambda b, i, j: (layer, 0, j)),
                  pl.BlockSpec((None, A_WIDTH, D_MODEL), lambda b, i, j: (layer, 0, 0))],
        out_specs=(pl.BlockSpec((1, tm, tn), lambda b, i, j: (b, i, j)),
                   pl.BlockSpec((1, A_WIDTH, tm), lambda b, i, j: (b, 0, i)),
                   pl.BlockSpec((1, tm, D_MODEL), lambda b, i, j: (b, i, 0))),
        scratch_shapes=[pltpu.VMEM((tm, LANES), jnp.float32),
                        pltpu.VMEM((tm, LANES), jnp.float32)],
        compiler_params=pltpu.CompilerParams(
            dimension_semantics=("parallel", "parallel", "arbitrary"),
            vmem_limit_bytes=VMEM_LIMIT),
        name="in_proj",
    )(x, shift, scale, gain, *tabs, w_main, w_vt)


def _diff_attn_kernel(q_ref, k_ref, vt_ref, dl_ref, gain_ref, o_ref,
                      qm_ref, vta_ref, sa_ref, sb_ref, m_ref, acc_ref, *, lambda_init, s_len):
    h = pl.program_id(1)
    tq = q_ref.shape[1]
    n_chunks = s_len // A_KC

    @pl.when(pl.program_id(2) == 0)
    def _():
        vta_ref[:A_VDIM, :] = vt_ref[0]
        row = lax.broadcasted_iota(jnp.int32, (A_VROWS - A_VDIM, s_len), 0)
        vta_ref[A_VDIM:, :] = jnp.where(row == 0, 1.0, 0.0).astype(vta_ref.dtype)

    q = q_ref[0]
    pair = _rope_layout_head(lax.broadcasted_iota(jnp.int32, q.shape, 1))
    base = 2 * (h % 2)
    for c in range(2):
        qsel = jnp.where(pair == base + c, q, jnp.zeros_like(q)).astype(jnp.float32)
        qm_ref[c] = (qsel * (HEAD_DIM ** -0.5 * LOG2_E)).astype(qm_ref.dtype)
        m_ref[c] = jnp.full((1, tq), MASK_VALUE, jnp.float32)
        acc_ref[c] = jnp.zeros((A_VROWS, tq), jnp.float32)

    def scores(i, s_ref):
        kc = k_ref[0, pl.ds(pl.multiple_of(i * A_KC, A_KC), A_KC), :]
        for c in range(2):
            s_ref[c] = lax.dot_general(kc, qm_ref[c], _NT,
                                       preferred_element_type=jnp.float32)

    def update(i, s_ref):
        vt_c = vta_ref[:, pl.ds(pl.multiple_of(i * A_KC, A_KC), A_KC)]
        for c in range(2):
            st = s_ref[c]
            m_old = m_ref[c]
            m_new = jnp.maximum(m_old, jnp.max(st, axis=0, keepdims=True))
            alpha = jnp.exp2(m_old - m_new)
            e = jnp.exp2(st - m_new).astype(jnp.bfloat16)
            pv = jnp.dot(vt_c, e, preferred_element_type=jnp.float32)
            acc_ref[c] = alpha * acc_ref[c] + pv
            m_ref[c] = m_new

    scores(0, sa_ref)

    def body(j, carry):
        scores(2 * j + 1, sb_ref)
        update(2 * j, sa_ref)
        scores(2 * j + 2, sa_ref)
        update(2 * j + 1, sb_ref)
        return carry

    lax.fori_loop(0, n_chunks // 2 - 1, body, 0)
    scores(n_chunks - 1, sb_ref)
    update(n_chunks - 2, sa_ref)
    update(n_chunks - 1, sb_ref)

    dl = dl_ref[...]
    lam = (jnp.exp(jnp.sum(dl[0:1] * dl[1:2], axis=1, keepdims=True))
           - jnp.exp(jnp.sum(dl[2:3] * dl[3:4], axis=1, keepdims=True)) + lambda_init)
    maps = [acc_ref[c, :A_VDIM, :] * (1.0 / acc_ref[c, A_VDIM:A_VDIM + 1, :]) for c in range(2)]
    ot = maps[0] - lam * maps[1]
    ms = jnp.mean(ot * ot, axis=0, keepdims=True)
    y = ot * lax.rsqrt(ms + RMS_EPS) * gain_ref[...] * (1.0 - lambda_init)
    o_ref[0] = y.T.astype(o_ref.dtype)


def _diff_attention(proj, vt, diff_lambda, subln_gain, lambda_init):
    bn, s_len, _ = proj.shape
    tq = A_TQ
    qblk = COL_AQ // HEAD_BLOCK
    kblk = COL_AK // HEAD_BLOCK
    assert s_len % (2 * A_KC) == 0
    return pl.pallas_call(
        functools.partial(_diff_attn_kernel, lambda_init=lambda_init, s_len=s_len),
        out_shape=jax.ShapeDtypeStruct((bn, s_len, A_WIDTH), jnp.bfloat16),
        grid=(bn, A_HEADS, s_len // tq),
        in_specs=[pl.BlockSpec((1, tq, HEAD_BLOCK), lambda b, h, i: (b, i, qblk + h // 2)),
                  pl.BlockSpec((1, s_len, HEAD_BLOCK), lambda b, h, i: (b, 0, kblk + h // 2)),
                  pl.BlockSpec((1, A_VDIM, s_len), lambda b, h, i: (b, h, 0)),
                  pl.BlockSpec((4, HEAD_DIM), lambda b, h, i: (0, 0)),
                  pl.BlockSpec((A_VDIM, 1), lambda b, h, i: (0, 0))],
        out_specs=pl.BlockSpec((1, tq, A_VDIM), lambda b, h, i: (b, i, h)),
        scratch_shapes=[pltpu.VMEM((2, tq, HEAD_BLOCK), jnp.bfloat16),
                        pltpu.VMEM((A_VROWS, s_len), jnp.bfloat16),
                        pltpu.VMEM((2, A_KC, tq), jnp.float32),
                        pltpu.VMEM((2, A_KC, tq), jnp.float32),
                        pltpu.VMEM((2, 1, tq), jnp.float32),
                        pltpu.VMEM((2, A_VROWS, tq), jnp.float32)],
        compiler_params=pltpu.CompilerParams(
            dimension_semantics=("parallel", "parallel", "arbitrary"),
            vmem_limit_bytes=VMEM_LIMIT),
        name="diff_attn",
    )(proj, proj, vt, diff_lambda, subln_gain.reshape(A_VDIM, 1))


def _head_lane_masks(shape, rotary_layout=False):
    col = lax.broadcasted_iota(jnp.int32, shape, 1)
    head = _rope_layout_head(col) if rotary_layout else col // HEAD_DIM
    return [head == h for h in range(HEAD_BLOCK // HEAD_DIM)]


def _stack_heads(q, qmask, scale):
    q = (q.astype(jnp.float32) * scale).astype(q.dtype)
    zero = jnp.zeros_like(q)
    return jnp.concatenate([jnp.where(mk, q, zero) for mk in qmask], axis=0)


def _unstack_heads(x, qmask, tq):
    out = x[:tq]
    for h in range(1, len(qmask)):
        out = jnp.where(qmask[h], x[h * tq:(h + 1) * tq], out)
    return out


def _band_group(q_ref, k_ref, v_ref, y_ref, band_ref, qd_ref, kd_ref, vd_ref,
                run_o_ref, run_l_ref, *, dilation, s_len, first, last):
    length = s_len // dilation
    tq = min(B_TQ_MAX, length)
    win = min(tq + 2 * B_REACH, length)
    unroll = min(B_UNROLL_MAX, s_len // tq)
    nblk = length // tq
    assert length % tq == 0 and (s_len // tq) % unroll == 0
    assert REGROUP_ROWS % dilation == 0 and s_len % REGROUP_ROWS == 0
    assert dilation == 1 or (REGROUP_ROWS // dilation) % SUBLANES_BF16 == 0
    assert not (last and dilation > 1)
    qmask = _head_lane_masks((tq, HEAD_BLOCK), rotary_layout=True)

    if dilation > 1:
        per = REGROUP_ROWS // dilation
        out_row = lax.broadcasted_iota(jnp.int32, (REGROUP_ROWS, REGROUP_ROWS), 0)
        in_row = lax.broadcasted_iota(jnp.int32, (REGROUP_ROWS, REGROUP_ROWS), 1)
        onehot = jnp.where(in_row == (out_row % per) * dilation + out_row // per, 1.0, 0.0)
        onehot = onehot.astype(jnp.bfloat16)
        def regroup(i, carry):
            for u in range(REGROUP_UNROLL):
                blk = i * REGROUP_UNROLL + u
                r0 = pl.multiple_of(blk * REGROUP_ROWS, REGROUP_ROWS)
                for src, dst in ((q_ref, qd_ref), (k_ref, kd_ref), (v_ref, vd_ref)):
                    y = jnp.dot(onehot, src[0, pl.ds(r0, REGROUP_ROWS), :],
                                preferred_element_type=jnp.float32).astype(dst.dtype)
                    for m in range(dilation):
                        d0 = pl.multiple_of(m * length + blk * per, per)
                        dst[pl.ds(d0, per), :] = y[m * per:(m + 1) * per, :]
            return carry

        lax.fori_loop(0, s_len // (REGROUP_ROWS * REGROUP_UNROLL), regroup, 0)
        q_src, k_src, v_src = qd_ref, kd_ref, vd_ref
    else:
        q_src, k_src, v_src = q_ref.at[0], k_ref.at[0], v_ref.at[0]

    def one_block(t):
        m = t // nblk
        q0 = pl.multiple_of((t % nblk) * tq, tq)
        ks = pl.multiple_of(jnp.clip(q0 - B_REACH, 0, length - win), B_REACH)
        base = pl.multiple_of(m * length, tq)
        if dilation > 1:
            rows = pl.ds(q0 * dilation + m, tq, stride=dilation)
        else:
            rows = pl.ds(q0, tq)
        q = q_src[pl.ds(base + q0, tq), :]
        kw = k_src[pl.ds(base + ks, win), :]
        vw = v_src[pl.ds(base + ks, win), :]
        q = (q.astype(jnp.float32) * (HEAD_DIM ** -0.5 * LOG2_E)).astype(q.dtype)
        zero = jnp.zeros_like(q)
        band = band_ref[(q0 - ks) // B_REACH]
        for slab in range(HEAD_BLOCK // LANES):
            lanes = slice(slab * LANES, (slab + 1) * LANES)
            heads = range(slab * HEADS_PER_SLAB, (slab + 1) * HEADS_PER_SLAB)
            qs = jnp.concatenate([jnp.where(qmask[h], q, zero) for h in heads], axis=0)
            s = lax.dot_general(qs, kw, _NT, preferred_element_type=jnp.float32) + band
            mx = jnp.max(s, axis=-1, keepdims=True)
            e = jnp.exp2(s - mx)
            den = jnp.sum(e, axis=-1, keepdims=True)
            o = jnp.dot(e.astype(jnp.bfloat16), vw[:, lanes], preferred_element_type=jnp.float32)
            o = o * (1.0 / den)
            lse = jnp.broadcast_to(mx + jnp.log2(den), o.shape)
            first_head = lax.broadcasted_iota(jnp.int32, (tq, LANES), 1) < HEAD_DIM
            o_s = jnp.where(first_head, o[:tq], o[tq:])
            l_s = jnp.where(first_head, lse[:tq], lse[tq:])
            if not first:
                o_run, l_run = run_o_ref[slab, rows, :], run_l_ref[slab, rows, :]
                l_max = jnp.maximum(l_run, l_s)
                w_run, w_new = jnp.exp2(l_run - l_max), jnp.exp2(l_s - l_max)
                total = w_run + w_new
                o_s = (o_run * w_run + o_s * w_new) * (1.0 / total)
                l_s = l_max + jnp.log2(total)
            if last:
                y_ref[0, rows, lanes] = o_s.astype(y_ref.dtype)
            else:
                run_o_ref[slab, rows, :] = o_s
                run_l_ref[slab, rows, :] = l_s

    def body(i, carry):
        for u in range(unroll):
            one_block(i * unroll + u)
        return carry

    lax.fori_loop(0, dilation * nblk // unroll, body, 0)


def _band_mix_kernel(q_ref, k_ref, v_ref, y_ref, band_ref, *scratch, s_len, dilations):
    g = pl.program_id(1)

    tq = band_ref.shape[1] // HEADS_PER_SLAB
    row = lax.broadcasted_iota(jnp.int32, band_ref.shape[1:], 0) % tq
    col = lax.broadcasted_iota(jnp.int32, band_ref.shape[1:], 1)
    for case in range(B_CASES):
        valid = jnp.abs(col - case * B_REACH - row) <= B_REACH
        band_ref[case] = jnp.where(valid, 0.0, MASK_VALUE)

    for step, dilation in enumerate(dilations):
        @pl.when(g == step)
        def _(step=step, dilation=dilation):
            _band_group(q_ref, k_ref, v_ref, y_ref, band_ref, *scratch, dilation=dilation,
                        s_len=s_len, first=(step == 0), last=(step == len(dilations) - 1))


def _band_mixture(proj):
    bn, s_len, _ = proj.shape
    n_groups = len(B_PATTERNS)
    order = sorted(range(n_groups), key=lambda g: -B_PATTERNS[g][1])
    dilations = tuple(B_PATTERNS[g][1] for g in order)
    for window, dilation in B_PATTERNS:
        assert (window // 2) // dilation == B_REACH
    assert order == list(range(n_groups - 1, -1, -1))

    def in_spec(col0):
        blk = col0 // HEAD_BLOCK
        return pl.BlockSpec((1, s_len, HEAD_BLOCK), lambda b, g: (b, 0, blk + n_groups - 1 - g))

    slabs = HEAD_BLOCK // LANES
    regrouped = pltpu.VMEM((s_len, HEAD_BLOCK), jnp.bfloat16)
    state = pltpu.VMEM((slabs, s_len, LANES), jnp.float32)
    return pl.pallas_call(
        functools.partial(_band_mix_kernel, s_len=s_len, dilations=dilations),
        out_shape=jax.ShapeDtypeStruct((bn, s_len, B_WIDTH), jnp.bfloat16),
        grid=(bn, n_groups),
        in_specs=[in_spec(COL_BQ), in_spec(COL_BK), in_spec(COL_BV)],
        out_specs=pl.BlockSpec((1, s_len, B_WIDTH), lambda b, g: (b, 0, 0)),
        scratch_shapes=[pltpu.VMEM((B_CASES, HEADS_PER_SLAB * B_TQ_MAX, B_TQ_MAX + 2 * B_REACH), jnp.float32),
                        regrouped, regrouped, regrouped,
                        state, state],
        compiler_params=pltpu.CompilerParams(
            dimension_semantics=("parallel", "arbitrary"),
            vmem_limit_bytes=VMEM_LIMIT),
        name="band_mix",
    )(proj, proj, proj)


def _na_table_kernel(rpb_ref, tab_ref):
    h = pl.program_id(0)
    n_dr = 2 * NA_KH - 1
    n_dc = 2 * NA_KW - 1
    shape = (GRID_W, LANES)
    qc = lax.broadcasted_iota(jnp.int32, shape, 0)
    lane = lax.broadcasted_iota(jnp.int32, shape, 1)
    kc = lane % GRID_W
    dc_i = jnp.clip(kc - qc + (NA_KW - 1), 0, n_dc - 1)
    cs = jnp.clip(qc - NA_KW // 2, 0, GRID_W - NA_KW)
    col_valid = (kc >= cs) & (kc < cs + NA_KW)
    masked = jnp.full(shape, MASK_VALUE, jnp.float32)

    row_bias = []
    for dr in range(n_dr):
        base = (h * n_dr + dr) * n_dc
        t = masked
        for dc in range(n_dc):
            t = jnp.where(dc_i == dc, rpb_ref[base + dc], t)
        row_bias.append(jnp.where(col_valid, t, MASK_VALUE))

    for case in range(C_CASES):
        for tile in range(C_WIN // LANES):
            halves = [row_bias[j - case + NA_KH - 1] for j in (2 * tile, 2 * tile + 1)]
            tab_ref[case, 0, :, tile * LANES:(tile + 1) * LANES] = (
                jnp.where(lane < GRID_W, halves[0], halves[1]))


def _na_tables(rpb):
    flat = rpb.reshape(-1)
    return pl.pallas_call(
        _na_table_kernel,
        out_shape=jax.ShapeDtypeStruct((C_CASES, C_HEADS, GRID_W, C_WIN), jnp.float32),
        grid=(C_HEADS,),
        in_specs=[pl.BlockSpec(memory_space=pltpu.SMEM)],
        out_specs=pl.BlockSpec((C_CASES, 1, GRID_W, C_WIN), lambda h: (0, h, 0, 0)),
        name="na_tables",
    )(flat)


def _na_kernel(q_ref, k_ref, v_ref, tab_ref, o_ref, *, rows):
    rb = pl.program_id(1)
    qmask = _head_lane_masks((GRID_W, HEAD_BLOCK))

    def one_row(t):
        r = rb * C_RB + t
        rs = jnp.clip(r - NA_KH // 2, 0, rows - NA_KH)
        k0 = pl.multiple_of(rs * GRID_W, GRID_W)
        q0 = pl.multiple_of(t * GRID_W, GRID_W)
        q = q_ref[0, pl.ds(q0, GRID_W), :]
        kw = k_ref[0, pl.ds(k0, C_WIN), :]
        vw = v_ref[0, pl.ds(k0, C_WIN), :]
        qs = _stack_heads(q, qmask, HEAD_DIM ** -0.5)
        s = lax.dot_general(qs, kw, _NT, preferred_element_type=jnp.float32)
        s = s + tab_ref[r - rs]
        mx = jnp.max(s, axis=-1, keepdims=True)
        e = jnp.exp(s - mx)
        den = jnp.sum(e, axis=-1, keepdims=True)
        o = jnp.dot(e.astype(jnp.bfloat16), vw, preferred_element_type=jnp.float32)
        o_acc = _unstack_heads(o * (1.0 / den), qmask, GRID_W)
        o_ref[0, pl.ds(q0, GRID_W), :] = o_acc.astype(o_ref.dtype)

    def body(i, carry):
        for u in range(C_UNROLL):
            one_row(i * C_UNROLL + u)
        return carry

    lax.fori_loop(0, C_RB // C_UNROLL, body, 0)


def _neighborhood_attention(proj, tables):
    bn, s_len, _ = proj.shape
    rows = s_len // GRID_W
    tq = C_RB * GRID_W
    qblk = COL_CQ // HEAD_BLOCK
    kblk = COL_CK // HEAD_BLOCK
    vblk = COL_CV // HEAD_BLOCK
    return pl.pallas_call(
        functools.partial(_na_kernel, rows=rows),
        out_shape=jax.ShapeDtypeStruct((bn, s_len, C_WIDTH), jnp.bfloat16),
        grid=(bn, rows // C_RB),
        in_specs=[pl.BlockSpec((1, tq, HEAD_BLOCK), lambda b, i: (b, i, qblk)),
                  pl.BlockSpec((1, s_len, HEAD_BLOCK), lambda b, i: (b, 0, kblk)),
                  pl.BlockSpec((1, s_len, HEAD_BLOCK), lambda b, i: (b, 0, vblk)),
                  pl.BlockSpec((C_CASES, C_HEADS * GRID_W, C_WIN), lambda b, i: (0, 0, 0))],
        out_specs=pl.BlockSpec((1, tq, C_WIDTH), lambda b, i: (b, i, 0)),
        compiler_params=pltpu.CompilerParams(
            dimension_semantics=("parallel", "arbitrary"),
            vmem_limit_bytes=VMEM_LIMIT),
        name="na_attn",
    )(proj, proj, proj, tables.reshape(C_CASES, C_HEADS * GRID_W, C_WIN))


def _merge_kernel(x_ref, gate_ref, h_ref, ya_ref, yb_ref, yc_ref, wg_ref, wb_ref, wo_ref, fg_ref,
                  o_ref, *, final_norm):
    h = h_ref[0]

    def gate_cols(k):
        return jnp.dot(h, wg_ref[:, k * D_MODEL:(k + 1) * D_MODEL], preferred_element_type=jnp.float32)

    z = gate_cols(0)
    sz = z * _sigmoid(z)
    ya = (ya_ref[0].astype(jnp.float32) * sz[:, :A_WIDTH]).astype(jnp.bfloat16)
    ybz = (yb_ref[0].astype(jnp.float32) * sz[:, A_WIDTH:A_WIDTH + B_WIDTH]).astype(jnp.bfloat16)
    ycz = (yc_ref[0].astype(jnp.float32) * sz[:, A_WIDTH + B_WIDTH:]).astype(jnp.bfloat16)

    pa = jnp.dot(ya, wb_ref[:A_WIDTH, :], preferred_element_type=jnp.float32)
    pb = jnp.dot(ybz, wb_ref[A_WIDTH:A_WIDTH + B_WIDTH, :], preferred_element_type=jnp.float32)
    pc = jnp.dot(ycz, wb_ref[A_WIDTH + B_WIDTH:, :], preferred_element_type=jnp.float32)
    merged = _sigmoid(gate_cols(1)) * pa
    merged = merged + _sigmoid(gate_cols(2)) * pb
    merged = merged + _sigmoid(gate_cols(3)) * pc
    out = jnp.dot(merged.astype(jnp.bfloat16), wo_ref[...], preferred_element_type=jnp.float32)
    xn = x_ref[0] + gate_ref[0] * out
    if final_norm:
        ms = jnp.mean(xn * xn, axis=-1, keepdims=True)
        xn = xn * lax.rsqrt(ms + RMS_EPS) * fg_ref[...]
    o_ref[0] = xn


def _merge(x, gate, h, ya, yb, yc, w_main, w_branch, w_out, final_gain, layer, final_norm):
    bn, s_len, _ = x.shape
    tm = MERGE_TM
    assert ATTN_COLS % GATE_COLS == 0 and GATE_COLS == (1 + N_BRANCHES) * D_MODEL
    gate_blk = ATTN_COLS // GATE_COLS

    def tok(width):
        return pl.BlockSpec((1, tm, width), lambda b, i: (b, i, 0))

    return pl.pallas_call(
        functools.partial(_merge_kernel, final_norm=final_norm),
        out_shape=jax.ShapeDtypeStruct((bn, s_len, D_MODEL), jnp.float32),
        grid=(bn, s_len // tm),
        in_specs=[tok(D_MODEL),
                  pl.BlockSpec((1, 1, D_MODEL), lambda b, i: (b, 0, 0)),
                  tok(D_MODEL), tok(A_WIDTH), tok(B_WIDTH), tok(C_WIDTH),
                  pl.BlockSpec((None, D_MODEL, GATE_COLS), lambda b, i: (layer, 0, gate_blk)),
                  pl.BlockSpec((None, BR_WIDTH, D_MODEL), lambda b, i: (layer, 0, 0)),
                  pl.BlockSpec((None, D_MODEL, D_MODEL), lambda b, i: (layer, 0, 0)),
                  pl.BlockSpec((1, D_MODEL), lambda b, i: (0, 0))],
        out_specs=tok(D_MODEL),
        compiler_params=pltpu.CompilerParams(
            dimension_semantics=("parallel", "parallel"),
            vmem_limit_bytes=VMEM_LIMIT),
        name="merge",
    )(x, gate, h, ya, yb, yc, w_main, w_branch, w_out, final_gain)


def _w_main_kernel(w_ref, o_ref):
    d = pl.program_id(1)
    w = w_ref[...].astype(jnp.bfloat16)

    @pl.when(d < ROPE_COLS // W_PREP_COLS)
    def _():
        src = lax.broadcasted_iota(jnp.int32, (HEAD_BLOCK, HEAD_BLOCK), 0)
        dst = lax.broadcasted_iota(jnp.int32, (HEAD_BLOCK, HEAD_BLOCK), 1)
        onehot = jnp.where(src == _rope_layout_source(dst), 1.0, 0.0).astype(jnp.bfloat16)
        for blk in range(W_PREP_COLS // HEAD_BLOCK):
            cols = slice(blk * HEAD_BLOCK, (blk + 1) * HEAD_BLOCK)
            o_ref[:, cols] = jnp.dot(w[:, cols], onehot,
                                     preferred_element_type=jnp.float32).astype(o_ref.dtype)

    @pl.when(d >= ROPE_COLS // W_PREP_COLS)
    def _():
        o_ref[...] = w


def _w_vt_kernel(w_ref, o_ref):
    w = w_ref[...].astype(jnp.bfloat16)
    r = lax.broadcasted_iota(jnp.int32, (A_WIDTH, A_WIDTH), 0)
    c = lax.broadcasted_iota(jnp.int32, (A_WIDTH, A_WIDTH), 1)
    eye = jnp.where(r == c, 1.0, 0.0).astype(jnp.bfloat16)
    o_ref[...] = lax.dot_general(eye, w, _NT, preferred_element_type=jnp.float32).astype(o_ref.dtype)


def _split_w_in(w_in):
    depth = w_in.shape[0]
    av_blk = 2 * A_QK_COLS // W_PREP_COLS
    assert 2 * A_QK_COLS % W_PREP_COLS == 0 and A_WIDTH == W_PREP_COLS and ROPE_COLS % W_PREP_COLS == 0
    w_main = pl.pallas_call(
        _w_main_kernel,
        out_shape=jax.ShapeDtypeStruct((depth, D_MODEL, MAIN_COLS), jnp.bfloat16),
        grid=(depth, MAIN_COLS // W_PREP_COLS),
        in_specs=[pl.BlockSpec((None, D_MODEL, W_PREP_COLS),
                               lambda l, d: (l, 0, jnp.where(d < av_blk, d, d + 1)))],
        out_specs=pl.BlockSpec((None, D_MODEL, W_PREP_COLS), lambda l, d: (l, 0, d)),
        name="w_main_prep",
    )(w_in)
    w_vt = pl.pallas_call(
        _w_vt_kernel,
        out_shape=jax.ShapeDtypeStruct((depth, A_WIDTH, D_MODEL), jnp.bfloat16),
        grid=(depth,),
        in_specs=[pl.BlockSpec((None, D_MODEL, A_WIDTH), lambda l: (l, 0, av_blk))],
        out_specs=pl.BlockSpec((None, A_WIDTH, D_MODEL), lambda l: (l, 0, 0)),
        name="w_vt_prep",
    )(w_in)
    return w_main, w_vt


def kernel(x, c, positions, norm_gain, w_ada, b_ada, w_in, diff_lambda, diff_subln_gain, na_rpb,
           w_branch, w_out, final_gain):
    depth = w_in.shape[0]
    tabs = _rope_tables(positions)
    ada = _ada_all_layers(c, w_ada, b_ada)
    fg = final_gain.reshape(1, D_MODEL)
    w_main, w_vt = _split_w_in(w_in)
    w_branch = w_branch.astype(jnp.bfloat16)
    w_out = w_out.astype(jnp.bfloat16)
    for layer in range(depth):
        shift = ada[layer, :, None, :D_MODEL]
        scale = ada[layer, :, None, D_MODEL:2 * D_MODEL]
        gate = ada[layer, :, None, 2 * D_MODEL:]
        proj, vt, h = _in_proj(x, shift, scale, norm_gain[layer].reshape(1, D_MODEL), tabs, w_main,
                               w_vt, layer)
        lambda_init = 0.8 - 0.6 * math.exp(-0.3 * layer)
        ya = _diff_attention(proj, vt, diff_lambda[layer], diff_subln_gain[layer], lambda_init)
        yb = _band_mixture(proj)
        yc = _neighborhood_attention(proj, _na_tables(na_rpb[layer]))
        x = _merge(x, gate, h, ya, yb, yc, w_main, w_branch, w_out, fg, layer,
                   final_norm=(layer == depth - 1))
    return x
```

```python
import functools
import math

import numpy as np
import jax
import jax.numpy as jnp
from jax import lax
from jax.experimental import pallas as pl
from jax.experimental.pallas import tpu as pltpu

D_MODEL = 1024
HEAD_DIM = 64
ROT_DIM = HEAD_DIM // 4
ROT_HALF = ROT_DIM // 2
ROPE_THETA = 500000.0
RMS_EPS = 1e-6
MASK_VALUE = -1e30
LOG2_E = math.log2(math.e)

A_HEADS = 4
A_VDIM = 2 * HEAD_DIM
A_WIDTH = A_HEADS * A_VDIM
A_QK_COLS = A_HEADS * 2 * HEAD_DIM

B_PATTERNS = ((128, 1), (512, 4), (2048, 16))
B_GROUPS = 3
B_HEADS = 4
B_WIDTH = B_HEADS * HEAD_DIM
B_COLS = B_GROUPS * B_WIDTH

GRID_W = 64
C_HEADS = 4
NA_KH = 8
NA_KW = 16
C_WIDTH = C_HEADS * HEAD_DIM

N_BRANCHES = 3
BR_WIDTH = A_WIDTH + B_WIDTH + C_WIDTH

LANES = 128
SUBLANES_F32 = 8
SUBLANES_BF16 = 16
HEADS_PER_SLAB = LANES // HEAD_DIM
HEAD_BLOCK = 4 * HEAD_DIM
ROPE_LANES = (HEAD_BLOCK // HEAD_DIM) * ROT_HALF
VMEM_LIMIT = 56 * 1024 * 1024

COL_AQ = 0
COL_AK = COL_AQ + A_QK_COLS
COL_BQ = COL_AK + A_QK_COLS
COL_BK = COL_BQ + B_COLS
COL_BV = COL_BK + B_COLS
COL_CQ = COL_BV + B_COLS
COL_CK = COL_CQ + C_WIDTH
COL_CV = COL_CK + C_WIDTH
COL_Z = COL_CV + C_WIDTH
COL_G = COL_Z + BR_WIDTH
MAIN_COLS = COL_G + N_BRANCHES * D_MODEL
ATTN_COLS = COL_Z
GATE_COLS = MAIN_COLS - ATTN_COLS
ROPE_COLS = COL_BV

W_PREP_COLS = 512
PROJ_TM = 2048
PROJ_TN = 1024
A_TQ = 2048
A_KC = 512
A_VROWS = A_VDIM + SUBLANES_BF16
TABLE_TS = 1024
TABLE_PACK = LANES // ROPE_LANES
ADA_TN = 1024
B_TQ_MAX = 128
B_UNROLL_MAX = 8
B_REACH = 64
REGROUP_ROWS = 256
REGROUP_UNROLL = 4
B_CASES = 3
C_WIN = NA_KH * GRID_W
C_CASES = NA_KH
C_RB = 32
C_UNROLL = 16
MERGE_TM = 1024

_NT = (((1,), (1,)), ((), ()))


def _sigmoid(x):
    return 1.0 / (1.0 + jnp.exp2(x * (-LOG2_E)))


def _rope_layout_head(col):
    l = col % LANES
    c = col // LANES
    rest = HEAD_DIM - ROT_DIM
    return jnp.where(l < ROPE_LANES, l // ROT_HALF, 2 * c + jnp.where(l >= ROPE_LANES + rest, 1, 0))


def _rope_layout_source(col):
    l = col % LANES
    c = col // LANES
    rest = HEAD_DIM - ROT_DIM
    r = l - ROPE_LANES
    upper = jnp.where(r >= rest, 1, 0)
    rotary_src = HEAD_DIM * (l // ROT_HALF) + ROT_HALF * c + l % ROT_HALF
    other_src = HEAD_DIM * (2 * c + upper) + ROT_DIM + r - rest * upper
    return jnp.where(l < ROPE_LANES, rotary_src, other_src)


def _rope_table_kernel(pos_ref, inv_ref, c_ref, s_ref):
    ang = pos_ref[0] * inv_ref[...]
    c_ref[0] = jnp.cos(ang)
    s_ref[0] = jnp.sin(ang)


def _rope_tables(positions):
    bn, s_len = positions.shape
    inv = np.float32(ROPE_THETA) ** (-np.arange(0, ROT_DIM, 2, dtype=np.float32) / np.float32(ROT_DIM))
    inv_lane = np.tile(inv, LANES // ROT_HALF).reshape(1, LANES)
    pos = jnp.repeat(positions.astype(jnp.float32).reshape(bn, s_len // TABLE_PACK, TABLE_PACK),
                     ROPE_LANES, axis=-1)
    ts = TABLE_TS // TABLE_PACK
    tab = jax.ShapeDtypeStruct((bn, s_len // TABLE_PACK, LANES), jnp.float32)
    spec = pl.BlockSpec((1, ts, LANES), lambda b, i: (b, i, 0))
    return pl.pallas_call(
        _rope_table_kernel,
        out_shape=(tab, tab),
        grid=(bn, s_len // TABLE_TS),
        in_specs=[spec, pl.BlockSpec((1, LANES), lambda b, i: (0, 0))],
        out_specs=(spec, spec),
        name="rope_tables",
    )(pos, jnp.asarray(inv_lane))


def _ada_kernel(c_ref, w_ref, b_ref, o_ref):
    c = c_ref[...]
    c_act = c * _sigmoid(c)
    o_ref[0] = jnp.dot(c_act, w_ref[0], precision=lax.Precision.HIGHEST,
                       preferred_element_type=jnp.float32) + b_ref[0]


def _ada_all_layers(c, w_ada, b_ada):
    depth = w_ada.shape[0]
    bn = c.shape[0]
    rows = SUBLANES_F32 * pl.cdiv(bn, SUBLANES_F32)
    c_pad = jnp.pad(c, ((0, rows - bn), (0, 0)))
    tn = ADA_TN
    out = pl.pallas_call(
        _ada_kernel,
        out_shape=jax.ShapeDtypeStruct((depth, rows, 3 * D_MODEL), jnp.float32),
        grid=(depth, 3 * D_MODEL // tn),
        in_specs=[pl.BlockSpec((rows, D_MODEL), lambda l, j: (0, 0)),
                  pl.BlockSpec((1, D_MODEL, tn), lambda l, j: (l, 0, j)),
                  pl.BlockSpec((1, 1, tn), lambda l, j: (l, 0, j))],
        out_specs=pl.BlockSpec((1, rows, tn), lambda l, j: (l, 0, j)),
        name="adaln",
    )(c_pad, w_ada, b_ada.reshape(depth, 1, 3 * D_MODEL))
    return out[:, :bn]


def _in_proj_kernel(x_ref, shift_ref, scale_ref, gain_ref, cp_ref, sp_ref, w_ref, wvt_ref,
                    proj_ref, vt_ref, h_ref, c_ref, s_ref):
    j = pl.program_id(2)

    @pl.when(j == 0)
    def _():
        lane = lax.broadcasted_iota(jnp.int32, cp_ref.shape[1:], 1)
        rows = cp_ref.shape[1]
        for t in range(TABLE_PACK):
            shift = (LANES - ROPE_LANES * t) % LANES
            ct = cp_ref[0] if shift == 0 else pltpu.roll(cp_ref[0], shift, 1)
            st = sp_ref[0] if shift == 0 else pltpu.roll(sp_ref[0], shift, 1)
            c_ref[pl.ds(t, rows, stride=TABLE_PACK), :] = jnp.where(lane < ROPE_LANES, ct, 1.0)
            s_ref[pl.ds(t, rows, stride=TABLE_PACK), :] = jnp.where(lane < ROPE_LANES, st, 0.0)

        xf = x_ref[0]
        ms = jnp.mean(xf * xf, axis=-1, keepdims=True)
        y = xf * lax.rsqrt(ms + RMS_EPS) * gain_ref[...]
        h = (y * (1.0 + scale_ref[0]) + shift_ref[0]).astype(jnp.bfloat16)
        h_ref[0] = h
        vt = lax.dot_general(wvt_ref[...], h, _NT, preferred_element_type=jnp.float32)
        vt_ref[0] = vt.astype(vt_ref.dtype)

    acc = jnp.dot(h_ref[0], w_ref[...], preferred_element_type=jnp.float32)
    blocks_per_tile = PROJ_TN // HEAD_BLOCK
    cc = c_ref[...]
    ss = s_ref[...]
    for blk in range(blocks_per_tile):
        rotary = (j * blocks_per_tile + blk) < (ROPE_COLS // HEAD_BLOCK)
        cb = jnp.where(rotary, cc, 1.0)
        sb = jnp.where(rotary, ss, 0.0)
        lo0 = blk * HEAD_BLOCK
        hi0 = lo0 + LANES
        lo = acc[:, lo0:hi0]
        hi = acc[:, hi0:hi0 + LANES]
        proj_ref[0, :, lo0:hi0] = (lo * cb - hi * sb).astype(proj_ref.dtype)
        proj_ref[0, :, hi0:hi0 + LANES] = (hi * cb + lo * sb).astype(proj_ref.dtype)


def _in_proj(x, shift, scale, gain, tabs, w_main, w_vt, layer):
    bn, s_len, _ = x.shape
    tm, tn = PROJ_TM, PROJ_TN
    tab_spec = pl.BlockSpec((1, tm // TABLE_PACK, LANES), lambda b, i, j: (b, i, 0))
    mod_spec = pl.BlockSpec((1, 1, D_MODEL), lambda b, i, j: (b, 0, 0))
    return pl.pallas_call(
        _in_proj_kernel,
        out_shape=(jax.ShapeDtypeStruct((bn, s_len, ATTN_COLS), jnp.bfloat16),
                   jax.ShapeDtypeStruct((bn, A_WIDTH, s_len), jnp.bfloat16),
                   jax.ShapeDtypeStruct((bn, s_len, D_MODEL), jnp.bfloat16)),
        grid=(bn, s_len // tm, ATTN_COLS // tn),
        in_specs=[pl.BlockSpec((1, tm, D_MODEL), lambda b, i, j: (b, i, 0)),
                  mod_spec, mod_spec,
                  pl.BlockSpec((1, D_MODEL), lambda b, i, j: (0, 0)),
                  tab_spec, tab_spec,
                  pl.BlockSpec((None, D_MODEL, tn), lambda b, i, j: (layer, 0, j)),
                  pl.BlockSpec((None, A_WIDTH, D_MODEL), lambda b, i, j: (layer, 0, 0))],
        out_specs=(pl.BlockSpec((1, tm, tn), lambda b, i, j: (b, i, j)),
                   pl.BlockSpec((1, A_WIDTH, tm), lambda b, i, j: (b, 0, i)),
                   pl.BlockSpec((1, tm, D_MODEL), lambda b, i, j: (b, i, 0))),
        scratch_shapes=[pltpu.VMEM((tm, LANES), jnp.float32),
                        pltpu.VMEM((tm, LANES), jnp.float32)],
        compiler_params=pltpu.CompilerParams(
            dimension_semantics=("parallel", "parallel", "arbitrary"),
            vmem_limit_bytes=VMEM_LIMIT),
        name="in_proj",
    )(x, shift, scale, gain, *tabs, w_main, w_vt)


def _diff_attn_kernel(q_ref, k_ref, vt_ref, dl_ref, gain_ref, o_ref,
                      qm_ref, vta_ref, sa_ref, sb_ref, m_ref, acc_ref, *, lambda_init, s_len):
    h = pl.program_id(1)
    tq = q_ref.shape[1]
    n_chunks = s_len // A_KC

    @pl.when(pl.program_id(2) == 0)
    def _():
        vta_ref[:A_VDIM, :] = vt_ref[0]
        row = lax.broadcasted_iota(jnp.int32, (A_VROWS - A_VDIM, s_len), 0)
        vta_ref[A_VDIM:, :] = jnp.where(row == 0, 1.0, 0.0).astype(vta_ref.dtype)

    q = q_ref[0]
    pair = _rope_layout_head(lax.broadcasted_iota(jnp.int32, q.shape, 1))
    base = 2 * (h % 2)
    for c in range(2):
        qsel = jnp.where(pair == base + c, q, jnp.zeros_like(q)).astype(jnp.float32)
        qm_ref[c] = (qsel * (HEAD_DIM ** -0.5 * LOG2_E)).astype(qm_ref.dtype)

    def scores(i, s_ref):
        kc = k_ref[0, pl.ds(pl.multiple_of(i * A_KC, A_KC), A_KC), :]
        for c in range(2):
            s_ref[c] = lax.dot_general(kc, qm_ref[c], _NT,
                                       preferred_element_type=jnp.float32)

    def update(i, s_ref):
        vt_c = vta_ref[:, pl.ds(pl.multiple_of(i * A_KC, A_KC), A_KC)]
        for c in range(2):
            st = s_ref[c]
            m_old = m_ref[c]
            m_new = jnp.maximum(m_old, jnp.max(st, axis=0, keepdims=True))
            alpha = jnp.exp2(m_old - m_new)
            e = jnp.exp2(st - m_new).astype(jnp.bfloat16)
            pv = jnp.dot(vt_c, e, preferred_element_type=jnp.float32)
            acc_ref[c] = alpha * acc_ref[c] + pv
            m_ref[c] = m_new

    def first_update(s_ref):
        vt_c = vta_ref[:, :A_KC]
        for c in range(2):
            st = s_ref[c]
            m_new = jnp.max(st, axis=0, keepdims=True)
            e = jnp.exp2(st - m_new).astype(jnp.bfloat16)
            acc_ref[c] = jnp.dot(vt_c, e, preferred_element_type=jnp.float32)
            m_ref[c] = m_new

    scores(0, sa_ref)
    scores(1, sb_ref)
    first_update(sa_ref)

    def body(j, carry):
        scores(2 * j + 2, sa_ref)
        update(2 * j + 1, sb_ref)
        scores(2 * j + 3, sb_ref)
        update(2 * j + 2, sa_ref)
        return carry

    lax.fori_loop(0, n_chunks // 2 - 1, body, 0)
    update(n_chunks - 1, sb_ref)

    dl = dl_ref[...]
    lam = (jnp.exp(jnp.sum(dl[0:1] * dl[1:2], axis=1, keepdims=True))
           - jnp.exp(jnp.sum(dl[2:3] * dl[3:4], axis=1, keepdims=True)) + lambda_init)
    maps = [acc_ref[c, :A_VDIM, :] * (1.0 / acc_ref[c, A_VDIM:A_VDIM + 1, :]) for c in range(2)]
    ot = maps[0] - lam * maps[1]
    ms = jnp.mean(ot * ot, axis=0, keepdims=True)
    y = ot * lax.rsqrt(ms + RMS_EPS) * gain_ref[...] * (1.0 - lambda_init)
    o_ref[0] = y.T.astype(o_ref.dtype)


def _diff_attention(proj, vt, diff_lambda, subln_gain, lambda_init):
    bn, s_len, _ = proj.shape
    tq = A_TQ
    qblk = COL_AQ // HEAD_BLOCK
    kblk = COL_AK // HEAD_BLOCK
    assert s_len % (2 * A_KC) == 0
    return pl.pallas_call(
        functools.partial(_diff_attn_kernel, lambda_init=lambda_init, s_len=s_len),
        out_shape=jax.ShapeDtypeStruct((bn, s_len, A_WIDTH), jnp.bfloat16),
        grid=(bn, A_HEADS, s_len // tq),
        in_specs=[pl.BlockSpec((1, tq, HEAD_BLOCK), lambda b, h, i: (b, i, qblk + h // 2)),
                  pl.BlockSpec((1, s_len, HEAD_BLOCK), lambda b, h, i: (b, 0, kblk + h // 2)),
                  pl.BlockSpec((1, A_VDIM, s_len), lambda b, h, i: (b, h, 0)),
                  pl.BlockSpec((4, HEAD_DIM), lambda b, h, i: (0, 0)),
                  pl.BlockSpec((A_VDIM, 1), lambda b, h, i: (0, 0))],
        out_specs=pl.BlockSpec((1, tq, A_VDIM), lambda b, h, i: (b, i, h)),
        scratch_shapes=[pltpu.VMEM((2, tq, HEAD_BLOCK), jnp.bfloat16),
                        pltpu.VMEM((A_VROWS, s_len), jnp.bfloat16),
                        pltpu.VMEM((2, A_KC, tq), jnp.float32),
                        pltpu.VMEM((2, A_KC, tq), jnp.float32),
                        pltpu.VMEM((2, 1, tq), jnp.float32),
                        pltpu.VMEM((2, A_VROWS, tq), jnp.float32)],
        compiler_params=pltpu.CompilerParams(
            dimension_semantics=("parallel", "parallel", "arbitrary"),
            vmem_limit_bytes=VMEM_LIMIT),
        name="diff_attn",
    )(proj, proj, vt, diff_lambda, subln_gain.reshape(A_VDIM, 1))


def _head_lane_masks(shape, rotary_layout=False):
    col = lax.broadcasted_iota(jnp.int32, shape, 1)
    head = _rope_layout_head(col) if rotary_layout else col // HEAD_DIM
    return [head == h for h in range(HEAD_BLOCK // HEAD_DIM)]


def _stack_heads(q, qmask, scale):
    q = (q.astype(jnp.float32) * scale).astype(q.dtype)
    zero = jnp.zeros_like(q)
    return jnp.concatenate([jnp.where(mk, q, zero) for mk in qmask], axis=0)


def _unstack_heads(x, qmask, tq):
    out = x[:tq]
    for h in range(1, len(qmask)):
        out = jnp.where(qmask[h], x[h * tq:(h + 1) * tq], out)
    return out


def _band_group(q_ref, k_ref, v_ref, y_ref, band_ref, qd_ref, kd_ref, vd_ref,
                run_o_ref, run_l_ref, *, dilation, s_len, first, last):
    length = s_len // dilation
    tq = min(B_TQ_MAX, length)
    win = min(tq + 2 * B_REACH, length)
    unroll = min(B_UNROLL_MAX, s_len // tq)
    nblk = length // tq
    assert length % tq == 0 and (s_len // tq) % unroll == 0
    assert REGROUP_ROWS % dilation == 0 and s_len % REGROUP_ROWS == 0
    assert dilation == 1 or (REGROUP_ROWS // dilation) % SUBLANES_BF16 == 0
    assert not (last and dilation > 1)
    qmask = _head_lane_masks((tq, HEAD_BLOCK), rotary_layout=True)

    if dilation > 1:
        per = REGROUP_ROWS // dilation
        out_row = lax.broadcasted_iota(jnp.int32, (REGROUP_ROWS, REGROUP_ROWS), 0)
        in_row = lax.broadcasted_iota(jnp.int32, (REGROUP_ROWS, REGROUP_ROWS), 1)
        onehot = jnp.where(in_row == (out_row % per) * dilation + out_row // per, 1.0, 0.0)
        onehot = onehot.astype(jnp.bfloat16)
        def regroup(i, carry):
            for u in range(REGROUP_UNROLL):
                blk = i * REGROUP_UNROLL + u
                r0 = pl.multiple_of(blk * REGROUP_ROWS, REGROUP_ROWS)
                for src, dst in ((q_ref, qd_ref), (k_ref, kd_ref), (v_ref, vd_ref)):
                    y = jnp.dot(onehot, src[0, pl.ds(r0, REGROUP_ROWS), :],
                                preferred_element_type=jnp.float32).astype(dst.dtype)
                    for m in range(dilation):
                        d0 = pl.multiple_of(m * length + blk * per, per)
                        dst[pl.ds(d0, per), :] = y[m * per:(m + 1) * per, :]
            return carry

        lax.fori_loop(0, s_len // (REGROUP_ROWS * REGROUP_UNROLL), regroup, 0)
        q_src, k_src, v_src = qd_ref, kd_ref, vd_ref
    else:
        q_src, k_src, v_src = q_ref.at[0], k_ref.at[0], v_ref.at[0]

    def one_block(t):
        m = t // nblk
        q0 = pl.multiple_of((t % nblk) * tq, tq)
        ks = pl.multiple_of(jnp.clip(q0 - B_REACH, 0, length - win), B_REACH)
        base = pl.multiple_of(m * length, tq)
        if dilation > 1:
            rows = pl.ds(q0 * dilation + m, tq, stride=dilation)
        else:
            rows = pl.ds(q0, tq)
        q = q_src[pl.ds(base + q0, tq), :]
        kw = k_src[pl.ds(base + ks, win), :]
        vw = v_src[pl.ds(base + ks, win), :]
        q = (q.astype(jnp.float32) * (HEAD_DIM ** -0.5 * LOG2_E)).astype(q.dtype)
        zero = jnp.zeros_like(q)
        band = band_ref[(q0 - ks) // B_REACH]
        for slab in range(HEAD_BLOCK // LANES):
            lanes = slice(slab * LANES, (slab + 1) * LANES)
            heads = range(slab * HEADS_PER_SLAB, (slab + 1) * HEADS_PER_SLAB)
            qs = jnp.concatenate([jnp.where(qmask[h], q, zero) for h in heads], axis=0)
            s = lax.dot_general(qs, kw, _NT, preferred_element_type=jnp.float32) + band
            mx = jnp.max(s, axis=-1, keepdims=True)
            e = jnp.exp2(s - mx)
            den = jnp.sum(e, axis=-1, keepdims=True)
            o = jnp.dot(e.astype(jnp.bfloat16), vw[:, lanes], preferred_element_type=jnp.float32)
            o = o * (1.0 / den)
            lse = jnp.broadcast_to(mx + jnp.log2(den), o.shape)
            first_head = lax.broadcasted_iota(jnp.int32, (tq, LANES), 1) < HEAD_DIM
            o_s = jnp.where(first_head, o[:tq], o[tq:])
            l_s = jnp.where(first_head, lse[:tq], lse[tq:])
            if not first:
                o_run, l_run = run_o_ref[slab, rows, :], run_l_ref[slab, rows, :]
                l_max = jnp.maximum(l_run, l_s)
                w_run, w_new = jnp.exp2(l_run - l_max), jnp.exp2(l_s - l_max)
                total = w_run + w_new
                o_s = (o_run * w_run + o_s * w_new) * (1.0 / total)
                l_s = l_max + jnp.log2(total)
            if last:
                y_ref[0, rows, lanes] = o_s.astype(y_ref.dtype)
            else:
                run_o_ref[slab, rows, :] = o_s
                run_l_ref[slab, rows, :] = l_s

    def body(i, carry):
        for u in range(unroll):
            one_block(i * unroll + u)
        return carry

    lax.fori_loop(0, dilation * nblk // unroll, body, 0)


def _band_mix_kernel(q_ref, k_ref, v_ref, y_ref, band_ref, *scratch, s_len, dilations):
    g = pl.program_id(1)

    tq = band_ref.shape[1] // HEADS_PER_SLAB
    row = lax.broadcasted_iota(jnp.int32, band_ref.shape[1:], 0) % tq
    col = lax.broadcasted_iota(jnp.int32, band_ref.shape[1:], 1)
    for case in range(B_CASES):
        valid = jnp.abs(col - case * B_REACH - row) <= B_REACH
        band_ref[case] = jnp.where(valid, 0.0, MASK_VALUE)

    for step, dilation in enumerate(dilations):
        @pl.when(g == step)
        def _(step=step, dilation=dilation):
            _band_group(q_ref, k_ref, v_ref, y_ref, band_ref, *scratch, dilation=dilation,
                        s_len=s_len, first=(step == 0), last=(step == len(dilations) - 1))


def _band_mixture(proj):
    bn, s_len, _ = proj.shape
    n_groups = len(B_PATTERNS)
    order = sorted(range(n_groups), key=lambda g: -B_PATTERNS[g][1])
    dilations = tuple(B_PATTERNS[g][1] for g in order)
    for window, dilation in B_PATTERNS:
        assert (window // 2) // dilation == B_REACH
    assert order == list(range(n_groups - 1, -1, -1))

    def in_spec(col0):
        blk = col0 // HEAD_BLOCK
        return pl.BlockSpec((1, s_len, HEAD_BLOCK), lambda b, g: (b, 0, blk + n_groups - 1 - g))

    slabs = HEAD_BLOCK // LANES
    regrouped = pltpu.VMEM((s_len, HEAD_BLOCK), jnp.bfloat16)
    state = pltpu.VMEM((slabs, s_len, LANES), jnp.float32)
    return pl.pallas_call(
        functools.partial(_band_mix_kernel, s_len=s_len, dilations=dilations),
        out_shape=jax.ShapeDtypeStruct((bn, s_len, B_WIDTH), jnp.bfloat16),
        grid=(bn, n_groups),
        in_specs=[in_spec(COL_BQ), in_spec(COL_BK), in_spec(COL_BV)],
        out_specs=pl.BlockSpec((1, s_len, B_WIDTH), lambda b, g: (b, 0, 0)),
        scratch_shapes=[pltpu.VMEM((B_CASES, HEADS_PER_SLAB * B_TQ_MAX, B_TQ_MAX + 2 * B_REACH), jnp.float32),
                        regrouped, regrouped, regrouped,
                        state, state],
        compiler_params=pltpu.CompilerParams(
            dimension_semantics=("parallel", "arbitrary"),
            vmem_limit_bytes=VMEM_LIMIT),
        name="band_mix",
    )(proj, proj, proj)


def _na_table_kernel(rpb_ref, tab_ref):
    h = pl.program_id(0)
    n_dr = 2 * NA_KH - 1
    n_dc = 2 * NA_KW - 1
    shape = (GRID_W, LANES)
    qc = lax.broadcasted_iota(jnp.int32, shape, 0)
    lane = lax.broadcasted_iota(jnp.int32, shape, 1)
    kc = lane % GRID_W
    dc_i = jnp.clip(kc - qc + (NA_KW - 1), 0, n_dc - 1)
    cs = jnp.clip(qc - NA_KW // 2, 0, GRID_W - NA_KW)
    col_valid = (kc >= cs) & (kc < cs + NA_KW)
    masked = jnp.full(shape, MASK_VALUE, jnp.float32)

    row_bias = []
    for dr in range(n_dr):
        base = (h * n_dr + dr) * n_dc
        t = masked
        for dc in range(n_dc):
            t = jnp.where(dc_i == dc, rpb_ref[base + dc], t)
        row_bias.append(jnp.where(col_valid, t, MASK_VALUE))

    for case in range(C_CASES):
        for tile in range(C_WIN // LANES):
            halves = [row_bias[j - case + NA_KH - 1] for j in (2 * tile, 2 * tile + 1)]
            tab_ref[case, 0, :, tile * LANES:(tile + 1) * LANES] = (
                jnp.where(lane < GRID_W, halves[0], halves[1]))


def _na_tables(rpb):
    flat = rpb.reshape(-1)
    return pl.pallas_call(
        _na_table_kernel,
        out_shape=jax.ShapeDtypeStruct((C_CASES, C_HEADS, GRID_W, C_WIN), jnp.float32),
        grid=(C_HEADS,),
        in_specs=[pl.BlockSpec(memory_space=pltpu.SMEM)],
        out_specs=pl.BlockSpec((C_CASES, 1, GRID_W, C_WIN), lambda h: (0, h, 0, 0)),
        name="na_tables",
    )(flat)


def _na_kernel(q_ref, k_ref, v_ref, tab_ref, o_ref, *, rows):
    rb = pl.program_id(1)
    qmask = _head_lane_masks((GRID_W, HEAD_BLOCK))

    def one_row(t):
        r = rb * C_RB + t
        rs = jnp.clip(r - NA_KH // 2, 0, rows - NA_KH)
        k0 = pl.multiple_of(rs * GRID_W, GRID_W)
        q0 = pl.multiple_of(t * GRID_W, GRID_W)
        q = q_ref[0, pl.ds(q0, GRID_W), :]
        kw = k_ref[0, pl.ds(k0, C_WIN), :]
        vw = v_ref[0, pl.ds(k0, C_WIN), :]
        qs = _stack_heads(q, qmask, HEAD_DIM ** -0.5)
        s = lax.dot_general(qs, kw, _NT, preferred_element_type=jnp.float32)
        s = s + tab_ref[r - rs]
        mx = jnp.max(s, axis=-1, keepdims=True)
        e = jnp.exp(s - mx)
        den = jnp.sum(e, axis=-1, keepdims=True)
        o = jnp.dot(e.astype(jnp.bfloat16), vw, preferred_element_type=jnp.float32)
        o_acc = _unstack_heads(o * (1.0 / den), qmask, GRID_W)
        o_ref[0, pl.ds(q0, GRID_W), :] = o_acc.astype(o_ref.dtype)

    def body(i, carry):
        for u in range(C_UNROLL):
            one_row(i * C_UNROLL + u)
        return carry

    lax.fori_loop(0, C_RB // C_UNROLL, body, 0)


def _neighborhood_attention(proj, tables):
    bn, s_len, _ = proj.shape
    rows = s_len // GRID_W
    tq = C_RB * GRID_W
    qblk = COL_CQ // HEAD_BLOCK
    kblk = COL_CK // HEAD_BLOCK
    vblk = COL_CV // HEAD_BLOCK
    return pl.pallas_call(
        functools.partial(_na_kernel, rows=rows),
        out_shape=jax.ShapeDtypeStruct((bn, s_len, C_WIDTH), jnp.bfloat16),
        grid=(bn, rows // C_RB),
        in_specs=[pl.BlockSpec((1, tq, HEAD_BLOCK), lambda b, i: (b, i, qblk)),
                  pl.BlockSpec((1, s_len, HEAD_BLOCK), lambda b, i: (b, 0, kblk)),
                  pl.BlockSpec((1, s_len, HEAD_BLOCK), lambda b, i: (b, 0, vblk)),
                  pl.BlockSpec((C_CASES, C_HEADS * GRID_W, C_WIN), lambda b, i: (0, 0, 0))],
        out_specs=pl.BlockSpec((1, tq, C_WIDTH), lambda b, i: (b, i, 0)),
        compiler_params=pltpu.CompilerParams(
            dimension_semantics=("parallel", "arbitrary"),
            vmem_limit_bytes=VMEM_LIMIT),
        name="na_attn",
    )(proj, proj, proj, tables.reshape(C_CASES, C_HEADS * GRID_W, C_WIN))


def _merge_kernel(x_ref, gate_ref, h_ref, ya_ref, yb_ref, yc_ref, wg_ref, wb_ref, wo_ref, fg_ref,
                  o_ref, *, final_norm):
    h = h_ref[0]

    def gate_cols(k):
        return jnp.dot(h, wg_ref[:, k * D_MODEL:(k + 1) * D_MODEL], preferred_element_type=jnp.float32)

    z = gate_cols(0)
    sz = z * _sigmoid(z)
    ya = (ya_ref[0].astype(jnp.float32) * sz[:, :A_WIDTH]).astype(jnp.bfloat16)
    ybz = (yb_ref[0].astype(jnp.float32) * sz[:, A_WIDTH:A_WIDTH + B_WIDTH]).astype(jnp.bfloat16)
    ycz = (yc_ref[0].astype(jnp.float32) * sz[:, A_WIDTH + B_WIDTH:]).astype(jnp.bfloat16)

    pa = jnp.dot(ya, wb_ref[:A_WIDTH, :], preferred_element_type=jnp.float32)
    pb = jnp.dot(ybz, wb_ref[A_WIDTH:A_WIDTH + B_WIDTH, :], preferred_element_type=jnp.float32)
    pc = jnp.dot(ycz, wb_ref[A_WIDTH + B_WIDTH:, :], preferred_element_type=jnp.float32)
    merged = _sigmoid(gate_cols(1)) * pa
    merged = merged + _sigmoid(gate_cols(2)) * pb
    merged = merged + _sigmoid(gate_cols(3)) * pc
    out = jnp.dot(merged.astype(jnp.bfloat16), wo_ref[...], preferred_element_type=jnp.float32)
    xn = x_ref[0] + gate_ref[0] * out
    if final_norm:
        ms = jnp.mean(xn * xn, axis=-1, keepdims=True)
        xn = xn * lax.rsqrt(ms + RMS_EPS) * fg_ref[...]
    o_ref[0] = xn


def _merge(x, gate, h, ya, yb, yc, w_main, w_branch, w_out, final_gain, layer, final_norm):
    bn, s_len, _ = x.shape
    tm = MERGE_TM
    assert ATTN_COLS % GATE_COLS == 0 and GATE_COLS == (1 + N_BRANCHES) * D_MODEL
    gate_blk = ATTN_COLS // GATE_COLS

    def tok(width):
        return pl.BlockSpec((1, tm, width), lambda b, i: (b, i, 0))

    return pl.pallas_call(
        functools.partial(_merge_kernel, final_norm=final_norm),
        out_shape=jax.ShapeDtypeStruct((bn, s_len, D_MODEL), jnp.float32),
        grid=(bn, s_len // tm),
        in_specs=[tok(D_MODEL),
                  pl.BlockSpec((1, 1, D_MODEL), lambda b, i: (b, 0, 0)),
                  tok(D_MODEL), tok(A_WIDTH), tok(B_WIDTH), tok(C_WIDTH),
                  pl.BlockSpec((None, D_MODEL, GATE_COLS), lambda b, i: (layer, 0, gate_blk)),
                  pl.BlockSpec((None, BR_WIDTH, D_MODEL), lambda b, i: (layer, 0, 0)),
                  pl.BlockSpec((None, D_MODEL, D_MODEL), lambda b, i: (layer, 0, 0)),
                  pl.BlockSpec((1, D_MODEL), lambda b, i: (0, 0))],
        out_specs=tok(D_MODEL),
        compiler_params=pltpu.CompilerParams(
            dimension_semantics=("parallel", "parallel"),
            vmem_limit_bytes=VMEM_LIMIT),
        name="merge",
    )(x, gate, h, ya, yb, yc, w_main, w_branch, w_out, final_gain)


def _w_main_kernel(w_ref, o_ref):
    d = pl.program_id(1)
    w = w_ref[...].astype(jnp.bfloat16)

    @pl.when(d < ROPE_COLS // W_PREP_COLS)
    def _():
        src = lax.broadcasted_iota(jnp.int32, (HEAD_BLOCK, HEAD_BLOCK), 0)
        dst = lax.broadcasted_iota(jnp.int32, (HEAD_BLOCK, HEAD_BLOCK), 1)
        onehot = jnp.where(src == _rope_layout_source(dst), 1.0, 0.0).astype(jnp.bfloat16)
        for blk in range(W_PREP_COLS // HEAD_BLOCK):
            cols = slice(blk * HEAD_BLOCK, (blk + 1) * HEAD_BLOCK)
            o_ref[:, cols] = jnp.dot(w[:, cols], onehot,
                                     preferred_element_type=jnp.float32).astype(o_ref.dtype)

    @pl.when(d >= ROPE_COLS // W_PREP_COLS)
    def _():
        o_ref[...] = w


def _w_vt_kernel(w_ref, o_ref):
    w = w_ref[...].astype(jnp.bfloat16)
    r = lax.broadcasted_iota(jnp.int32, (A_WIDTH, A_WIDTH), 0)
    c = lax.broadcasted_iota(jnp.int32, (A_WIDTH, A_WIDTH), 1)
    eye = jnp.where(r == c, 1.0, 0.0).astype(jnp.bfloat16)
    o_ref[...] = lax.dot_general(eye, w, _NT, preferred_element_type=jnp.float32).astype(o_ref.dtype)


def _split_w_in(w_in):
    depth = w_in.shape[0]
    av_blk = 2 * A_QK_COLS // W_PREP_COLS
    assert 2 * A_QK_COLS % W_PREP_COLS == 0 and A_WIDTH == W_PREP_COLS and ROPE_COLS % W_PREP_COLS == 0
    w_main = pl.pallas_call(
        _w_main_kernel,
        out_shape=jax.ShapeDtypeStruct((depth, D_MODEL, MAIN_COLS), jnp.bfloat16),
        grid=(depth, MAIN_COLS // W_PREP_COLS),
        in_specs=[pl.BlockSpec((None, D_MODEL, W_PREP_COLS),
                               lambda l, d: (l, 0, jnp.where(d < av_blk, d, d + 1)))],
        out_specs=pl.BlockSpec((None, D_MODEL, W_PREP_COLS), lambda l, d: (l, 0, d)),
        name="w_main_prep",
    )(w_in)
    w_vt = pl.pallas_call(
        _w_vt_kernel,
        out_shape=jax.ShapeDtypeStruct((depth, A_WIDTH, D_MODEL), jnp.bfloat16),
        grid=(depth,),
        in_specs=[pl.BlockSpec((None, D_MODEL, A_WIDTH), lambda l: (l, 0, av_blk))],
        out_specs=pl.BlockSpec((None, A_WIDTH, D_MODEL), lambda l: (l, 0, 0)),
        name="w_vt_prep",
    )(w_in)
    return w_main, w_vt


def kernel(x, c, positions, norm_gain, w_ada, b_ada, w_in, diff_lambda, diff_subln_gain, na_rpb,
           w_branch, w_out, final_gain):
    depth = w_in.shape[0]
    tabs = _rope_tables(positions)
    ada = _ada_all_layers(c, w_ada, b_ada)
    fg = final_gain.reshape(1, D_MODEL)
    w_main, w_vt = _split_w_in(w_in)
    w_branch = w_branch.astype(jnp.bfloat16)
    w_out = w_out.astype(jnp.bfloat16)
    for layer in range(depth):
        shift = ada[layer, :, None, :D_MODEL]
        scale = ada[layer, :, None, D_MODEL:2 * D_MODEL]
        gate = ada[layer, :, None, 2 * D_MODEL:]
        proj, vt, h = _in_proj(x, shift, scale, norm_gain[layer].reshape(1, D_MODEL), tabs, w_main,
                               w_vt, layer)
        lambda_init = 0.8 - 0.6 * math.exp(-0.3 * layer)
        ya = _diff_attention(proj, vt, diff_lambda[layer], diff_subln_gain[layer], lambda_init)
        yb = _band_mixture(proj)
        yc = _neighborhood_attention(proj, _na_tables(na_rpb[layer]))
        x = _merge(x, gate, h, ya, yb, yc, w_main, w_branch, w_out, fg, layer,
                   final_norm=(layer == depth - 1))
    return x
```

```python
import functools
import math

import numpy as np
import jax
import jax.numpy as jnp
from jax import lax
from jax.experimental import pallas as pl
from jax.experimental.pallas import tpu as pltpu

D_MODEL = 1024
HEAD_DIM = 64
ROT_DIM = HEAD_DIM // 4
ROT_HALF = ROT_DIM // 2
ROPE_THETA = 500000.0
RMS_EPS = 1e-6
MASK_VALUE = -1e30
LOG2_E = math.log2(math.e)

A_HEADS = 4
A_VDIM = 2 * HEAD_DIM
A_WIDTH = A_HEADS * A_VDIM
A_QK_COLS = A_HEADS * 2 * HEAD_DIM

B_PATTERNS = ((128, 1), (512, 4), (2048, 16))
B_GROUPS = 3
B_HEADS = 4
B_WIDTH = B_HEADS * HEAD_DIM
B_COLS = B_GROUPS * B_WIDTH

GRID_W = 64
C_HEADS = 4
NA_KH = 8
NA_KW = 16
C_WIDTH = C_HEADS * HEAD_DIM

N_BRANCHES = 3
BR_WIDTH = A_WIDTH + B_WIDTH + C_WIDTH

LANES = 128
SUBLANES_F32 = 8
SUBLANES_BF16 = 16
HEADS_PER_SLAB = LANES // HEAD_DIM
HEAD_BLOCK = 4 * HEAD_DIM
ROPE_LANES = (HEAD_BLOCK // HEAD_DIM) * ROT_HALF
VMEM_LIMIT = 56 * 1024 * 1024

COL_AQ = 0
COL_AK = COL_AQ + A_QK_COLS
COL_BQ = COL_AK + A_QK_COLS
COL_BK = COL_BQ + B_COLS
COL_BV = COL_BK + B_COLS
COL_CQ = COL_BV + B_COLS
COL_CK = COL_CQ + C_WIDTH
COL_CV = COL_CK + C_WIDTH
COL_Z = COL_CV + C_WIDTH
COL_G = COL_Z + BR_WIDTH
MAIN_COLS = COL_G + N_BRANCHES * D_MODEL
ATTN_COLS = COL_Z
GATE_COLS = MAIN_COLS - ATTN_COLS
ROPE_COLS = COL_BV

W_PREP_COLS = 512
PROJ_TM = 2048
PROJ_TN = 1024
A_TQ = 2048
A_KC = 512
A_VROWS = A_VDIM + SUBLANES_BF16
TABLE_TS = 1024
TABLE_PACK = LANES // ROPE_LANES
ADA_TN = 1024
B_TQ_MAX = 128
B_UNROLL_MAX = 8
B_REACH = 64
REGROUP_ROWS = 256
REGROUP_UNROLL = 4
B_CASES = 3
C_WIN = NA_KH * GRID_W
C_CASES = NA_KH
C_RB = 32
C_UNROLL = 16
MERGE_TM = 1024

_NT = (((1,), (1,)), ((), ()))


def _sigmoid(x):
    return 1.0 / (1.0 + jnp.exp2(x * (-LOG2_E)))


def _rope_layout_head(col):
    l = col % LANES
    c = col // LANES
    rest = HEAD_DIM - ROT_DIM
    return jnp.where(l < ROPE_LANES, l // ROT_HALF, 2 * c + jnp.where(l >= ROPE_LANES + rest, 1, 0))


def _rope_layout_source(col):
    l = col % LANES
    c = col // LANES
    rest = HEAD_DIM - ROT_DIM
    r = l - ROPE_LANES
    upper = jnp.where(r >= rest, 1, 0)
    rotary_src = HEAD_DIM * (l // ROT_HALF) + ROT_HALF * c + l % ROT_HALF
    other_src = HEAD_DIM * (2 * c + upper) + ROT_DIM + r - rest * upper
    return jnp.where(l < ROPE_LANES, rotary_src, other_src)


def _rope_table_kernel(pos_ref, inv_ref, c_ref, s_ref):
    ang = pos_ref[0] * inv_ref[...]
    c_ref[0] = jnp.cos(ang)
    s_ref[0] = jnp.sin(ang)


def _rope_tables(positions):
    bn, s_len = positions.shape
    inv = np.float32(ROPE_THETA) ** (-np.arange(0, ROT_DIM, 2, dtype=np.float32) / np.float32(ROT_DIM))
    inv_lane = np.tile(inv, LANES // ROT_HALF).reshape(1, LANES)
    pos = jnp.repeat(positions.astype(jnp.float32).reshape(bn, s_len // TABLE_PACK, TABLE_PACK),
                     ROPE_LANES, axis=-1)
    ts = TABLE_TS // TABLE_PACK
    tab = jax.ShapeDtypeStruct((bn, s_len // TABLE_PACK, LANES), jnp.float32)
    spec = pl.BlockSpec((1, ts, LANES), lambda b, i: (b, i, 0))
    return pl.pallas_call(
        _rope_table_kernel,
        out_shape=(tab, tab),
        grid=(bn, s_len // TABLE_TS),
        in_specs=[spec, pl.BlockSpec((1, LANES), lambda b, i: (0, 0))],
        out_specs=(spec, spec),
        name="rope_tables",
    )(pos, jnp.asarray(inv_lane))


def _ada_kernel(c_ref, w_ref, b_ref, o_ref):
    c = c_ref[...]
    c_act = c * _sigmoid(c)
    o_ref[0] = jnp.dot(c_act, w_ref[0], precision=lax.Precision.HIGHEST,
                       preferred_element_type=jnp.float32) + b_ref[0]


def _ada_all_layers(c, w_ada, b_ada):
    depth = w_ada.shape[0]
    bn = c.shape[0]
    rows = SUBLANES_F32 * pl.cdiv(bn, SUBLANES_F32)
    c_pad = jnp.pad(c, ((0, rows - bn), (0, 0)))
    tn = ADA_TN
    out = pl.pallas_call(
        _ada_kernel,
        out_shape=jax.ShapeDtypeStruct((depth, rows, 3 * D_MODEL), jnp.float32),
        grid=(depth, 3 * D_MODEL // tn),
        in_specs=[pl.BlockSpec((rows, D_MODEL), lambda l, j: (0, 0)),
                  pl.BlockSpec((1, D_MODEL, tn), lambda l, j: (l, 0, j)),
                  pl.BlockSpec((1, 1, tn), lambda l, j: (l, 0, j))],
        out_specs=pl.BlockSpec((1, rows, tn), lambda l, j: (l, 0, j)),
        name="adaln",
    )(c_pad, w_ada, b_ada.reshape(depth, 1, 3 * D_MODEL))
    return out[:, :bn]


def _in_proj_kernel(x_ref, shift_ref, scale_ref, gain_ref, cp_ref, sp_ref, w_ref, wvt_ref,
                    proj_ref, vt_ref, h_ref, c_ref, s_ref):
    j = pl.program_id(2)

    @pl.when(j == 0)
    def _():
        lane = lax.broadcasted_iota(jnp.int32, cp_ref.shape[1:], 1)
        rows = cp_ref.shape[1]
        for t in range(TABLE_PACK):
            shift = (LANES - ROPE_LANES * t) % LANES
            ct = cp_ref[0] if shift == 0 else pltpu.roll(cp_ref[0], shift, 1)
            st = sp_ref[0] if shift == 0 else pltpu.roll(sp_ref[0], shift, 1)
            c_ref[pl.ds(t, rows, stride=TABLE_PACK), :] = jnp.where(lane < ROPE_LANES, ct, 1.0)
            s_ref[pl.ds(t, rows, stride=TABLE_PACK), :] = jnp.where(lane < ROPE_LANES, st, 0.0)

        xf = x_ref[0]
        ms = jnp.mean(xf * xf, axis=-1, keepdims=True)
        y = xf * lax.rsqrt(ms + RMS_EPS) * gain_ref[...]
        h = (y * (1.0 + scale_ref[0]) + shift_ref[0]).astype(jnp.bfloat16)
        h_ref[0] = h
        vt = lax.dot_general(wvt_ref[...], h, _NT, preferred_element_type=jnp.float32)
        vt_ref[0] = vt.astype(vt_ref.dtype)

    acc = jnp.dot(h_ref[0], w_ref[...], preferred_element_type=jnp.float32)
    blocks_per_tile = PROJ_TN // HEAD_BLOCK
    cc = c_ref[...]
    ss = s_ref[...]
    for blk in range(blocks_per_tile):
        rotary = (j * blocks_per_tile + blk) < (ROPE_COLS // HEAD_BLOCK)
        cb = jnp.where(rotary, cc, 1.0)
        sb = jnp.where(rotary, ss, 0.0)
        lo0 = blk * HEAD_BLOCK
        hi0 = lo0 + LANES
        lo = acc[:, lo0:hi0]
        hi = acc[:, hi0:hi0 + LANES]
        proj_ref[0, :, lo0:hi0] = (lo * cb - hi * sb).astype(proj_ref.dtype)
        proj_ref[0, :, hi0:hi0 + LANES] = (hi * cb + lo * sb).astype(proj_ref.dtype)


def _in_proj(x, shift, scale, gain, tabs, w_main, w_vt, layer):
    bn, s_len, _ = x.shape
    tm, tn = PROJ_TM, PROJ_TN
    tab_spec = pl.BlockSpec((1, tm // TABLE_PACK, LANES), lambda b, i, j: (b, i, 0))
    mod_spec = pl.BlockSpec((1, 1, D_MODEL), lambda b, i, j: (b, 0, 0))
    return pl.pallas_call(
        _in_proj_kernel,
        out_shape=(jax.ShapeDtypeStruct((bn, s_len, ATTN_COLS), jnp.bfloat16),
                   jax.ShapeDtypeStruct((bn, A_WIDTH, s_len), jnp.bfloat16),
                   jax.ShapeDtypeStruct((bn, s_len, D_MODEL), jnp.bfloat16)),
        grid=(bn, s_len // tm, ATTN_COLS // tn),
        in_specs=[pl.BlockSpec((1, tm, D_MODEL), lambda b, i, j: (b, i, 0)),
                  mod_spec, mod_spec,
                  pl.BlockSpec((1, D_MODEL), lambda b, i, j: (0, 0)),
                  tab_spec, tab_spec,
                  pl.BlockSpec((None, D_MODEL, tn), lambda b, i, j: (layer, 0, j)),
                  pl.BlockSpec((None, A_WIDTH, D_MODEL), lambda b, i, j: (layer, 0, 0))],
        out_specs=(pl.BlockSpec((1, tm, tn), lambda b, i, j: (b, i, j)),
                   pl.BlockSpec((1, A_WIDTH, tm), lambda b, i, j: (b, 0, i)),
                   pl.BlockSpec((1, tm, D_MODEL), lambda b, i, j: (b, i, 0))),
        scratch_shapes=[pltpu.VMEM((tm, LANES), jnp.float32),
                        pltpu.VMEM((tm, LANES), jnp.float32)],
        compiler_params=pltpu.CompilerParams(
            dimension_semantics=("parallel", "parallel", "arbitrary"),
            vmem_limit_bytes=VMEM_LIMIT),
        name="in_proj",
    )(x, shift, scale, gain, *tabs, w_main, w_vt)


def _diff_attn_kernel(q_ref, k_ref, vt_ref, dl_ref, gain_ref, o_ref,
                      qm_ref, vta_ref, sa_ref, sb_ref, m_ref, acc_ref, *, lambda_init, s_len):
    h = pl.program_id(1)
    tq = q_ref.shape[1]
    n_chunks = s_len // A_KC

    @pl.when(pl.program_id(2) == 0)
    def _():
        vta_ref[:A_VDIM, :] = vt_ref[0]
        row = lax.broadcasted_iota(jnp.int32, (A_VROWS - A_VDIM, s_len), 0)
        vta_ref[A_VDIM:, :] = jnp.where(row == 0, 1.0, 0.0).astype(vta_ref.dtype)

    q = q_ref[0]
    pair = _rope_layout_head(lax.broadcasted_iota(jnp.int32, q.shape, 1))
    base = 2 * (h % 2)
    qs = (q.astype(jnp.float32) * (HEAD_DIM ** -0.5 * LOG2_E)).astype(qm_ref.dtype)
    for c in range(2):
        qm_ref[c] = jnp.where(pair == base + c, qs, jnp.zeros_like(qs))

    def scores(i, s_ref):
        kc = k_ref[0, pl.ds(pl.multiple_of(i * A_KC, A_KC), A_KC), :]
        for c in range(2):
            s_ref[c] = lax.dot_general(kc, qm_ref[c], _NT,
                                       preferred_element_type=jnp.float32)

    def update(i, s_ref):
        vt_c = vta_ref[:, pl.ds(pl.multiple_of(i * A_KC, A_KC), A_KC)]
        for c in range(2):
            st = s_ref[c]
            m_old = m_ref[c]
            m_new = jnp.maximum(m_old, jnp.max(st, axis=0, keepdims=True))
            alpha = jnp.exp2(m_old - m_new)
            e = jnp.exp2(st - m_new).astype(jnp.bfloat16)
            pv = jnp.dot(vt_c, e, preferred_element_type=jnp.float32)
            acc_ref[c] = alpha * acc_ref[c] + pv
            m_ref[c] = m_new

    def first_update(s_ref):
        vt_c = vta_ref[:, :A_KC]
        for c in range(2):
            st = s_ref[c]
            m_new = jnp.max(st, axis=0, keepdims=True)
            e = jnp.exp2(st - m_new).astype(jnp.bfloat16)
            acc_ref[c] = jnp.dot(vt_c, e, preferred_element_type=jnp.float32)
            m_ref[c] = m_new

    scores(0, sa_ref)
    scores(1, sb_ref)
    first_update(sa_ref)

    def body(j, carry):
        scores(2 * j + 2, sa_ref)
        update(2 * j + 1, sb_ref)
        scores(2 * j + 3, sb_ref)
        update(2 * j + 2, sa_ref)
        return carry

    lax.fori_loop(0, n_chunks // 2 - 1, body, 0)
    update(n_chunks - 1, sb_ref)

    dl = dl_ref[...]
    lam = (jnp.exp(jnp.sum(dl[0:1] * dl[1:2], axis=1, keepdims=True))
           - jnp.exp(jnp.sum(dl[2:3] * dl[3:4], axis=1, keepdims=True)) + lambda_init)
    maps = [acc_ref[c, :A_VDIM, :] * (1.0 / acc_ref[c, A_VDIM:A_VDIM + 1, :]) for c in range(2)]
    ot = maps[0] - lam * maps[1]
    ms = jnp.mean(ot * ot, axis=0, keepdims=True)
    y = ot * lax.rsqrt(ms + RMS_EPS) * gain_ref[...] * (1.0 - lambda_init)
    o_ref[0] = y.T.astype(o_ref.dtype)


def _diff_attention(proj, vt, diff_lambda, subln_gain, lambda_init):
    bn, s_len, _ = proj.shape
    tq = A_TQ
    qblk = COL_AQ // HEAD_BLOCK
    kblk = COL_AK // HEAD_BLOCK
    assert s_len % (2 * A_KC) == 0
    return pl.pallas_call(
        functools.partial(_diff_attn_kernel, lambda_init=lambda_init, s_len=s_len),
        out_shape=jax.ShapeDtypeStruct((bn, s_len, A_WIDTH), jnp.bfloat16),
        grid=(bn, A_HEADS, s_len // tq),
        in_specs=[pl.BlockSpec((1, tq, HEAD_BLOCK), lambda b, h, i: (b, i, qblk + h // 2)),
                  pl.BlockSpec((1, s_len, HEAD_BLOCK), lambda b, h, i: (b, 0, kblk + h // 2)),
                  pl.BlockSpec((1, A_VDIM, s_len), lambda b, h, i: (b, h, 0)),
                  pl.BlockSpec((4, HEAD_DIM), lambda b, h, i: (0, 0)),
                  pl.BlockSpec((A_VDIM, 1), lambda b, h, i: (0, 0))],
        out_specs=pl.BlockSpec((1, tq, A_VDIM), lambda b, h, i: (b, i, h)),
        scratch_shapes=[pltpu.VMEM((2, tq, HEAD_BLOCK), jnp.bfloat16),
                        pltpu.VMEM((A_VROWS, s_len), jnp.bfloat16),
                        pltpu.VMEM((2, A_KC, tq), jnp.float32),
                        pltpu.VMEM((2, A_KC, tq), jnp.float32),
                        pltpu.VMEM((2, 1, tq), jnp.float32),
                        pltpu.VMEM((2, A_VROWS, tq), jnp.float32)],
        compiler_params=pltpu.CompilerParams(
            dimension_semantics=("parallel", "parallel", "arbitrary"),
            vmem_limit_bytes=VMEM_LIMIT),
        name="diff_attn",
    )(proj, proj, vt, diff_lambda, subln_gain.reshape(A_VDIM, 1))


def _head_lane_masks(shape, rotary_layout=False):
    col = lax.broadcasted_iota(jnp.int32, shape, 1)
    head = _rope_layout_head(col) if rotary_layout else col // HEAD_DIM
    return [head == h for h in range(HEAD_BLOCK // HEAD_DIM)]


def _stack_heads(q, qmask, scale):
    q = (q.astype(jnp.float32) * scale).astype(q.dtype)
    zero = jnp.zeros_like(q)
    return jnp.concatenate([jnp.where(mk, q, zero) for mk in qmask], axis=0)


def _unstack_heads(x, qmask, tq):
    out = x[:tq]
    for h in range(1, len(qmask)):
        out = jnp.where(qmask[h], x[h * tq:(h + 1) * tq], out)
    return out


def _band_group(q_ref, k_ref, v_ref, y_ref, band_ref, qd_ref, kd_ref, vd_ref,
                run_o_ref, run_l_ref, *, dilation, s_len, first, last):
    length = s_len // dilation
    tq = min(B_TQ_MAX, length)
    win = min(tq + 2 * B_REACH, length)
    unroll = min(B_UNROLL_MAX, s_len // tq)
    nblk = length // tq
    assert length % tq == 0 and (s_len // tq) % unroll == 0
    assert REGROUP_ROWS % dilation == 0 and s_len % REGROUP_ROWS == 0
    assert dilation == 1 or (REGROUP_ROWS // dilation) % SUBLANES_BF16 == 0
    assert not (last and dilation > 1)
    qmask = _head_lane_masks((tq, HEAD_BLOCK), rotary_layout=True)

    if dilation > 1:
        per = REGROUP_ROWS // dilation
        out_row = lax.broadcasted_iota(jnp.int32, (REGROUP_ROWS, REGROUP_ROWS), 0)
        in_row = lax.broadcasted_iota(jnp.int32, (REGROUP_ROWS, REGROUP_ROWS), 1)
        onehot = jnp.where(in_row == (out_row % per) * dilation + out_row // per, 1.0, 0.0)
        onehot = onehot.astype(jnp.bfloat16)
        def regroup(i, carry):
            for u in range(REGROUP_UNROLL):
                blk = i * REGROUP_UNROLL + u
                r0 = pl.multiple_of(blk * REGROUP_ROWS, REGROUP_ROWS)
                for src, dst in ((q_ref, qd_ref), (k_ref, kd_ref), (v_ref, vd_ref)):
                    y = jnp.dot(onehot, src[0, pl.ds(r0, REGROUP_ROWS), :],
                                preferred_element_type=jnp.float32).astype(dst.dtype)
                    for m in range(dilation):
                        d0 = pl.multiple_of(m * length + blk * per, per)
                        dst[pl.ds(d0, per), :] = y[m * per:(m + 1) * per, :]
            return carry

        lax.fori_loop(0, s_len // (REGROUP_ROWS * REGROUP_UNROLL), regroup, 0)
        q_src, k_src, v_src = qd_ref, kd_ref, vd_ref
    else:
        q_src, k_src, v_src = q_ref.at[0], k_ref.at[0], v_ref.at[0]

    def one_block(t):
        m = t // nblk
        q0 = pl.multiple_of((t % nblk) * tq, tq)
        ks = pl.multiple_of(jnp.clip(q0 - B_REACH, 0, length - win), B_REACH)
        base = pl.multiple_of(m * length, tq)
        if dilation > 1:
            rows = pl.ds(q0 * dilation + m, tq, stride=dilation)
        else:
            rows = pl.ds(q0, tq)
        q = q_src[pl.ds(base + q0, tq), :]
        kw = k_src[pl.ds(base + ks, win), :]
        vw = v_src[pl.ds(base + ks, win), :]
        q = (q.astype(jnp.float32) * (HEAD_DIM ** -0.5 * LOG2_E)).astype(q.dtype)
        zero = jnp.zeros_like(q)
        band = band_ref[(q0 - ks) // B_REACH]
        for slab in range(HEAD_BLOCK // LANES):
            lanes = slice(slab * LANES, (slab + 1) * LANES)
            heads = range(slab * HEADS_PER_SLAB, (slab + 1) * HEADS_PER_SLAB)
            qs = jnp.concatenate([jnp.where(qmask[h], q, zero) for h in heads], axis=0)
            s = lax.dot_general(qs, kw, _NT, preferred_element_type=jnp.float32) + band
            mx = jnp.max(s, axis=-1, keepdims=True)
            e = jnp.exp2(s - mx)
            den = jnp.sum(e, axis=-1, keepdims=True)
            o = jnp.dot(e.astype(jnp.bfloat16), vw[:, lanes], preferred_element_type=jnp.float32)
            o = o * (1.0 / den)
            lse = jnp.broadcast_to(mx + jnp.log2(den), o.shape)
            first_head = lax.broadcasted_iota(jnp.int32, (tq, LANES), 1) < HEAD_DIM
            o_s = jnp.where(first_head, o[:tq], o[tq:])
            l_s = jnp.where(first_head, lse[:tq], lse[tq:])
            if not first:
                o_run, l_run = run_o_ref[slab, rows, :], run_l_ref[slab, rows, :]
                l_max = jnp.maximum(l_run, l_s)
                w_run, w_new = jnp.exp2(l_run - l_max), jnp.exp2(l_s - l_max)
                total = w_run + w_new
                o_s = (o_run * w_run + o_s * w_new) * (1.0 / total)
                l_s = l_max + jnp.log2(total)
            if last:
                y_ref[0, rows, lanes] = o_s.astype(y_ref.dtype)
            else:
                run_o_ref[slab, rows, :] = o_s
                run_l_ref[slab, rows, :] = l_s

    def body(i, carry):
        for u in range(unroll):
            one_block(i * unroll + u)
        return carry

    lax.fori_loop(0, dilation * nblk // unroll, body, 0)


def _band_mix_kernel(q_ref, k_ref, v_ref, y_ref, band_ref, *scratch, s_len, dilations):
    g = pl.program_id(1)

    tq = band_ref.shape[1] // HEADS_PER_SLAB
    row = lax.broadcasted_iota(jnp.int32, band_ref.shape[1:], 0) % tq
    col = lax.broadcasted_iota(jnp.int32, band_ref.shape[1:], 1)
    for case in range(B_CASES):
        valid = jnp.abs(col - case * B_REACH - row) <= B_REACH
        band_ref[case] = jnp.where(valid, 0.0, MASK_VALUE)

    for step, dilation in enumerate(dilations):
        @pl.when(g == step)
        def _(step=step, dilation=dilation):
            _band_group(q_ref, k_ref, v_ref, y_ref, band_ref, *scratch, dilation=dilation,
                        s_len=s_len, first=(step == 0), last=(step == len(dilations) - 1))


def _band_mixture(proj):
    bn, s_len, _ = proj.shape
    n_groups = len(B_PATTERNS)
    order = sorted(range(n_groups), key=lambda g: -B_PATTERNS[g][1])
    dilations = tuple(B_PATTERNS[g][1] for g in order)
    for window, dilation in B_PATTERNS:
        assert (window // 2) // dilation == B_REACH
    assert order == list(range(n_groups - 1, -1, -1))

    def in_spec(col0):
        blk = col0 // HEAD_BLOCK
        return pl.BlockSpec((1, s_len, HEAD_BLOCK), lambda b, g: (b, 0, blk + n_groups - 1 - g))

    slabs = HEAD_BLOCK // LANES
    regrouped = pltpu.VMEM((s_len, HEAD_BLOCK), jnp.bfloat16)
    state = pltpu.VMEM((slabs, s_len, LANES), jnp.float32)
    return pl.pallas_call(
        functools.partial(_band_mix_kernel, s_len=s_len, dilations=dilations),
        out_shape=jax.ShapeDtypeStruct((bn, s_len, B_WIDTH), jnp.bfloat16),
        grid=(bn, n_groups),
        in_specs=[in_spec(COL_BQ), in_spec(COL_BK), in_spec(COL_BV)],
        out_specs=pl.BlockSpec((1, s_len, B_WIDTH), lambda b, g: (b, 0, 0)),
        scratch_shapes=[pltpu.VMEM((B_CASES, HEADS_PER_SLAB * B_TQ_MAX, B_TQ_MAX + 2 * B_REACH), jnp.float32),
                        regrouped, regrouped, regrouped,
                        state, state],
        compiler_params=pltpu.CompilerParams(
            dimension_semantics=("parallel", "arbitrary"),
            vmem_limit_bytes=VMEM_LIMIT),
        name="band_mix",
    )(proj, proj, proj)


def _na_table_kernel(rpb_ref, tab_ref):
    h = pl.program_id(0)
    n_dr = 2 * NA_KH - 1
    n_dc = 2 * NA_KW - 1
    shape = (GRID_W, LANES)
    qc = lax.broadcasted_iota(jnp.int32, shape, 0)
    lane = lax.broadcasted_iota(jnp.int32, shape, 1)
    kc = lane % GRID_W
    dc_i = jnp.clip(kc - qc + (NA_KW - 1), 0, n_dc - 1)
    cs = jnp.clip(qc - NA_KW // 2, 0, GRID_W - NA_KW)
    col_valid = (kc >= cs) & (kc < cs + NA_KW)
    masked = jnp.full(shape, MASK_VALUE, jnp.float32)

    row_bias = []
    for dr in range(n_dr):
        base = (h * n_dr + dr) * n_dc
        t = masked
        for dc in range(n_dc):
            t = jnp.where(dc_i == dc, rpb_ref[base + dc], t)
        row_bias.append(jnp.where(col_valid, t, MASK_VALUE))

    for case in range(C_CASES):
        for tile in range(C_WIN // LANES):
            halves = [row_bias[j - case + NA_KH - 1] for j in (2 * tile, 2 * tile + 1)]
            tab_ref[case, 0, :, tile * LANES:(tile + 1) * LANES] = (
                jnp.where(lane < GRID_W, halves[0], halves[1]))


def _na_tables(rpb):
    flat = rpb.reshape(-1)
    return pl.pallas_call(
        _na_table_kernel,
        out_shape=jax.ShapeDtypeStruct((C_CASES, C_HEADS, GRID_W, C_WIN), jnp.float32),
        grid=(C_HEADS,),
        in_specs=[pl.BlockSpec(memory_space=pltpu.SMEM)],
        out_specs=pl.BlockSpec((C_CASES, 1, GRID_W, C_WIN), lambda h: (0, h, 0, 0)),
        name="na_tables",
    )(flat)


def _na_kernel(q_ref, k_ref, v_ref, tab_ref, o_ref, *, rows):
    rb = pl.program_id(1)
    qmask = _head_lane_masks((GRID_W, HEAD_BLOCK))

    def one_row(t):
        r = rb * C_RB + t
        rs = jnp.clip(r - NA_KH // 2, 0, rows - NA_KH)
        k0 = pl.multiple_of(rs * GRID_W, GRID_W)
        q0 = pl.multiple_of(t * GRID_W, GRID_W)
        q = q_ref[0, pl.ds(q0, GRID_W), :]
        kw = k_ref[0, pl.ds(k0, C_WIN), :]
        vw = v_ref[0, pl.ds(k0, C_WIN), :]
        qs = _stack_heads(q, qmask, HEAD_DIM ** -0.5)
        s = lax.dot_general(qs, kw, _NT, preferred_element_type=jnp.float32)
        s = s + tab_ref[r - rs]
        mx = jnp.max(s, axis=-1, keepdims=True)
        e = jnp.exp(s - mx)
        den = jnp.sum(e, axis=-1, keepdims=True)
        o = jnp.dot(e.astype(jnp.bfloat16), vw, preferred_element_type=jnp.float32)
        o_acc = _unstack_heads(o * (1.0 / den), qmask, GRID_W)
        o_ref[0, pl.ds(q0, GRID_W), :] = o_acc.astype(o_ref.dtype)

    def body(i, carry):
        for u in range(C_UNROLL):
            one_row(i * C_UNROLL + u)
        return carry

    lax.fori_loop(0, C_RB // C_UNROLL, body, 0)


def _neighborhood_attention(proj, tables):
    bn, s_len, _ = proj.shape
    rows = s_len // GRID_W
    tq = C_RB * GRID_W
    qblk = COL_CQ // HEAD_BLOCK
    kblk = COL_CK // HEAD_BLOCK
    vblk = COL_CV // HEAD_BLOCK
    return pl.pallas_call(
        functools.partial(_na_kernel, rows=rows),
        out_shape=jax.ShapeDtypeStruct((bn, s_len, C_WIDTH), jnp.bfloat16),
        grid=(bn, rows // C_RB),
        in_specs=[pl.BlockSpec((1, tq, HEAD_BLOCK), lambda b, i: (b, i, qblk)),
                  pl.BlockSpec((1, s_len, HEAD_BLOCK), lambda b, i: (b, 0, kblk)),
                  pl.BlockSpec((1, s_len, HEAD_BLOCK), lambda b, i: (b, 0, vblk)),
                  pl.BlockSpec((C_CASES, C_HEADS * GRID_W, C_WIN), lambda b, i: (0, 0, 0))],
        out_specs=pl.BlockSpec((1, tq, C_WIDTH), lambda b, i: (b, i, 0)),
        compiler_params=pltpu.CompilerParams(
            dimension_semantics=("parallel", "arbitrary"),
            vmem_limit_bytes=VMEM_LIMIT),
        name="na_attn",
    )(proj, proj, proj, tables.reshape(C_CASES, C_HEADS * GRID_W, C_WIN))


def _merge_kernel(x_ref, gate_ref, h_ref, ya_ref, yb_ref, yc_ref, wg_ref, wb_ref, wo_ref, fg_ref,
                  o_ref, *, final_norm):
    h = h_ref[0]

    def gate_cols(k):
        return jnp.dot(h, wg_ref[:, k * D_MODEL:(k + 1) * D_MODEL], preferred_element_type=jnp.float32)

    z = gate_cols(0)
    sz = z * _sigmoid(z)
    ya = (ya_ref[0].astype(jnp.float32) * sz[:, :A_WIDTH]).astype(jnp.bfloat16)
    ybz = (yb_ref[0].astype(jnp.float32) * sz[:, A_WIDTH:A_WIDTH + B_WIDTH]).astype(jnp.bfloat16)
    ycz = (yc_ref[0].astype(jnp.float32) * sz[:, A_WIDTH + B_WIDTH:]).astype(jnp.bfloat16)

    pa = jnp.dot(ya, wb_ref[:A_WIDTH, :], preferred_element_type=jnp.float32)
    pb = jnp.dot(ybz, wb_ref[A_WIDTH:A_WIDTH + B_WIDTH, :], preferred_element_type=jnp.float32)
    pc = jnp.dot(ycz, wb_ref[A_WIDTH + B_WIDTH:, :], preferred_element_type=jnp.float32)
    merged = _sigmoid(gate_cols(1)) * pa
    merged = merged + _sigmoid(gate_cols(2)) * pb
    merged = merged + _sigmoid(gate_cols(3)) * pc
    out = jnp.dot(merged.astype(jnp.bfloat16), wo_ref[...], preferred_element_type=jnp.float32)
    xn = x_ref[0] + gate_ref[0] * out
    if final_norm:
        ms = jnp.mean(xn * xn, axis=-1, keepdims=True)
        xn = xn * lax.rsqrt(ms + RMS_EPS) * fg_ref[...]
    o_ref[0] = xn


def _merge(x, gate, h, ya, yb, yc, w_main, w_branch, w_out, final_gain, layer, final_norm):
    bn, s_len, _ = x.shape
    tm = MERGE_TM
    assert ATTN_COLS % GATE_COLS == 0 and GATE_COLS == (1 + N_BRANCHES) * D_MODEL
    gate_blk = ATTN_COLS // GATE_COLS

    def tok(width):
        return pl.BlockSpec((1, tm, width), lambda b, i: (b, i, 0))

    return pl.pallas_call(
        functools.partial(_merge_kernel, final_norm=final_norm),
        out_shape=jax.ShapeDtypeStruct((bn, s_len, D_MODEL), jnp.float32),
        grid=(bn, s_len // tm),
        in_specs=[tok(D_MODEL),
                  pl.BlockSpec((1, 1, D_MODEL), lambda b, i: (b, 0, 0)),
                  tok(D_MODEL), tok(A_WIDTH), tok(B_WIDTH), tok(C_WIDTH),
                  pl.BlockSpec((None, D_MODEL, GATE_COLS), lambda b, i: (layer, 0, gate_blk)),
                  pl.BlockSpec((None, BR_WIDTH, D_MODEL), lambda b, i: (layer, 0, 0)),
                  pl.BlockSpec((None, D_MODEL, D_MODEL), lambda b, i: (layer, 0, 0)),
                  pl.BlockSpec((1, D_MODEL), lambda b, i: (0, 0))],
        out_specs=tok(D_MODEL),
        compiler_params=pltpu.CompilerParams(
            dimension_semantics=("parallel", "parallel"),
            vmem_limit_bytes=VMEM_LIMIT),
        name="merge",
    )(x, gate, h, ya, yb, yc, w_main, w_branch, w_out, final_gain)


def _w_main_kernel(w_ref, o_ref):
    d = pl.program_id(1)
    w = w_ref[...].astype(jnp.bfloat16)

    @pl.when(d < ROPE_COLS // W_PREP_COLS)
    def _():
        src = lax.broadcasted_iota(jnp.int32, (HEAD_BLOCK, HEAD_BLOCK), 0)
        dst = lax.broadcasted_iota(jnp.int32, (HEAD_BLOCK, HEAD_BLOCK), 1)
        onehot = jnp.where(src == _rope_layout_source(dst), 1.0, 0.0).astype(jnp.bfloat16)
        for blk in range(W_PREP_COLS // HEAD_BLOCK):
            cols = slice(blk * HEAD_BLOCK, (blk + 1) * HEAD_BLOCK)
            o_ref[:, cols] = jnp.dot(w[:, cols], onehot,
                                     preferred_element_type=jnp.float32).astype(o_ref.dtype)

    @pl.when(d >= ROPE_COLS // W_PREP_COLS)
    def _():
        o_ref[...] = w


def _w_vt_kernel(w_ref, o_ref):
    w = w_ref[...].astype(jnp.bfloat16)
    r = lax.broadcasted_iota(jnp.int32, (A_WIDTH, A_WIDTH), 0)
    c = lax.broadcasted_iota(jnp.int32, (A_WIDTH, A_WIDTH), 1)
    eye = jnp.where(r == c, 1.0, 0.0).astype(jnp.bfloat16)
    o_ref[...] = lax.dot_general(eye, w, _NT, preferred_element_type=jnp.float32).astype(o_ref.dtype)


def _split_w_in(w_in):
    depth = w_in.shape[0]
    av_blk = 2 * A_QK_COLS // W_PREP_COLS
    assert 2 * A_QK_COLS % W_PREP_COLS == 0 and A_WIDTH == W_PREP_COLS and ROPE_COLS % W_PREP_COLS == 0
    w_main = pl.pallas_call(
        _w_main_kernel,
        out_shape=jax.ShapeDtypeStruct((depth, D_MODEL, MAIN_COLS), jnp.bfloat16),
        grid=(depth, MAIN_COLS // W_PREP_COLS),
        in_specs=[pl.BlockSpec((None, D_MODEL, W_PREP_COLS),
                               lambda l, d: (l, 0, jnp.where(d < av_blk, d, d + 1)))],
        out_specs=pl.BlockSpec((None, D_MODEL, W_PREP_COLS), lambda l, d: (l, 0, d)),
        name="w_main_prep",
    )(w_in)
    w_vt = pl.pallas_call(
        _w_vt_kernel,
        out_shape=jax.ShapeDtypeStruct((depth, A_WIDTH, D_MODEL), jnp.bfloat16),
        grid=(depth,),
        in_specs=[pl.BlockSpec((None, D_MODEL, A_WIDTH), lambda l: (l, 0, av_blk))],
        out_specs=pl.BlockSpec((None, A_WIDTH, D_MODEL), lambda l: (l, 0, 0)),
        name="w_vt_prep",
    )(w_in)
    return w_main, w_vt


def kernel(x, c, positions, norm_gain, w_ada, b_ada, w_in, diff_lambda, diff_subln_gain, na_rpb,
           w_branch, w_out, final_gain):
    depth = w_in.shape[0]
    tabs = _rope_tables(positions)
    ada = _ada_all_layers(c, w_ada, b_ada)
    fg = final_gain.reshape(1, D_MODEL)
    w_main, w_vt = _split_w_in(w_in)
    w_branch = w_branch.astype(jnp.bfloat16)
    w_out = w_out.astype(jnp.bfloat16)
    for layer in range(depth):
        shift = ada[layer, :, None, :D_MODEL]
        scale = ada[layer, :, None, D_MODEL:2 * D_MODEL]
        gate = ada[layer, :, None, 2 * D_MODEL:]
        proj, vt, h = _in_proj(x, shift, scale, norm_gain[layer].reshape(1, D_MODEL), tabs, w_main,
                               w_vt, layer)
        lambda_init = 0.8 - 0.6 * math.exp(-0.3 * layer)
        ya = _diff_attention(proj, vt, diff_lambda[layer], diff_subln_gain[layer], lambda_init)
        yb = _band_mixture(proj)
        yc = _neighborhood_attention(proj, _na_tables(na_rpb[layer]))
        x = _merge(x, gate, h, ya, yb, yc, w_main, w_branch, w_out, fg, layer,
                   final_norm=(layer == depth - 1))
    return x
```
